```python
import jax, jax.numpy as jnp
from jax import lax
import numpy as np

D_MODEL = 2048
BATCH = 8
SEQ = 2048
DEPTH = 2

GRID_W = 64
CTX_LEN = 256
N_MIXERS = 2
N_HGRN_LAYERS = (DEPTH + 1) // 2
N_CONV_LAYERS = DEPTH // 2
HGRN_EXPAND = 128
HGRN_HEADS = D_MODEL // HGRN_EXPAND
HGRN_HEAD_V = D_MODEL // HGRN_HEADS
HGRN_CHUNK = 64
CONV_WIDTH = 31
CONV_HALF = D_MODEL // 2
MOE_GROUPS = 8
MOE_PER_GROUP = 8
MOE_EXPERTS = MOE_GROUPS * MOE_PER_GROUP
MOE_TOPK = 2
MOE_HIDDEN = D_MODEL // 4
MOE_BLOCK = 256
EPS = 1e-6

kernel_name = 'hybrid_hgrn2_conformer_hmoe_dit'


def rmsnorm(x, g):
    xf = x.astype(jnp.float32)
    y = xf * lax.rsqrt(jnp.mean(xf * xf, axis=-1, keepdims=True) + EPS)
    return (y * g.astype(jnp.float32)).astype(x.dtype)


def layernorm(x, g, b):
    xf = x.astype(jnp.float32)
    mu = jnp.mean(xf, axis=-1, keepdims=True)
    xc = xf - mu
    y = xc * lax.rsqrt(jnp.mean(xc * xc, axis=-1, keepdims=True) + EPS)
    return (y * g.astype(jnp.float32) + b.astype(jnp.float32)).astype(x.dtype)


def modulate(h, shift, scale):
    return h * (1.0 + scale) + shift


def _forget(z, lb):
    f = lb + (1.0 - lb) * jax.nn.sigmoid(z.astype(jnp.float32))
    return jnp.log(f), 1.0 - f


def _chunked_gated_scan(q, k, v, log_f, s0):
    b_, t_, h_, _ = q.shape
    vd = v.shape[-1]
    nc = t_ // HGRN_CHUNK

    def to_chunks(a):
        a = a.astype(jnp.float32).reshape(b_, nc, HGRN_CHUNK, h_, a.shape[-1])
        return a.transpose(1, 0, 3, 2, 4)

    tril = jnp.tril(jnp.ones((HGRN_CHUNK, HGRN_CHUNK), dtype=bool))[:, :, None]

    def step(s, inp):
        qc, kc, vc, gc = inp
        bcum = jnp.cumsum(gc, axis=2)
        o_inter = jnp.einsum('bhck,bhkv->bhcv', qc * jnp.exp(bcum), s)
        diff = bcum[:, :, :, None, :] - bcum[:, :, None, :, :]
        decay = jnp.where(tril, jnp.exp(jnp.where(tril, diff, 0.0)), 0.0)
        scores = jnp.einsum('bhtk,bhtsk,bhsk->bhts', qc, decay, kc)
        o_intra = jnp.einsum('bhts,bhsv->bhtv', scores, vc)
        blast = bcum[:, :, -1:, :]
        s_new = jnp.exp(blast[:, :, 0, :])[..., None] * s + jnp.einsum(
            'bhck,bhcv->bhkv', kc * jnp.exp(blast - bcum), vc)
        return s_new, o_inter + o_intra

    s_fin, o = lax.scan(step, s0, (to_chunks(q), to_chunks(k), to_chunks(v), to_chunks(log_f)))
    o = o.transpose(1, 0, 3, 2, 4).reshape(b_, t_, h_, vd)
    return o, s_fin


def hgrn2_mixer(a_lat, a_ctx, w_in, lb, onorm_g, w_out, ctx_out):
    lb_fw = lb[0].reshape(HGRN_HEADS, HGRN_EXPAND)
    lb_bw = lb[1].reshape(HGRN_HEADS, HGRN_EXPAND)

    def project(a):
        b_, t_, _ = a.shape
        q, i_in, z_fw, z_bw, g = jnp.split(a @ w_in, 5, axis=-1)
        hk = lambda t: t.reshape(b_, t_, HGRN_HEADS, HGRN_EXPAND)
        lf_fw, k_fw = _forget(hk(z_fw), lb_fw)
        lf_bw, k_bw = _forget(hk(z_bw), lb_bw)
        v = i_in.reshape(b_, t_, HGRN_HEADS, HGRN_HEAD_V)
        return hk(jax.nn.silu(q)), v, lf_fw, k_fw, lf_bw, k_bw, g

    def bidir(p, s_fw, s_bw):
        q, v, lf_fw, k_fw, lf_bw, k_bw, _ = p
        flip = lambda t: t[:, ::-1]
        o_fw, s_fw_new = _chunked_gated_scan(q, k_fw, v, lf_fw, s_fw)
        o_bw, s_bw_new = _chunked_gated_scan(flip(q), flip(k_bw), flip(v), flip(lf_bw), s_bw)
        return o_fw + flip(o_bw), s_fw_new, s_bw_new

    def readout(o, g):
        b_, t_, _ = g.shape
        o = rmsnorm(o, onorm_g).astype(g.dtype)
        o = o * jax.nn.silu(g).reshape(b_, t_, HGRN_HEADS, HGRN_HEAD_V)
        return o.reshape(b_, t_, D_MODEL) @ w_out

    p_ctx = project(a_ctx)
    p_lat = project(a_lat)
    s0 = jnp.zeros((a_lat.shape[0], HGRN_HEADS, HGRN_EXPAND, HGRN_HEAD_V), jnp.float32)
    o_ctx, s_fw, s_bw = bidir(p_ctx, s0, s0)
    o_lat, _, _ = bidir(p_lat, s_fw, s_bw)
    y_lat = readout(o_lat, p_lat[-1])
    y_ctx = readout(o_ctx, p_ctx[-1]) if ctx_out else None
    return y_lat, y_ctx


def _dwconv_1d(u, w):
    pad = CONV_WIDTH // 2
    return lax.conv_general_dilated(
        u, w[:, None, :].astype(u.dtype), window_strides=(1,), padding=[(pad, pad)],
        dimension_numbers=('NWC', 'WIO', 'NWC'), feature_group_count=u.shape[-1])


def conformer_conv(a, w_pw1, w_dw, b_dw, ln_g, ln_b, w_pw2, on_grid):
    val, gate = jnp.split(a @ w_pw1, 2, axis=-1)
    u = val * jax.nn.sigmoid(gate)
    if on_grid:
        b_, t_, d = u.shape
        rows = t_ // GRID_W
        ug = u.reshape(b_, rows, GRID_W, d)
        uh = ug[..., :CONV_HALF].reshape(b_ * rows, GRID_W, CONV_HALF)
        uh = _dwconv_1d(uh, w_dw[:, :CONV_HALF]).reshape(b_, rows, GRID_W, CONV_HALF)
        uv = ug[..., CONV_HALF:].transpose(0, 2, 1, 3).reshape(b_ * GRID_W, rows, d - CONV_HALF)
        uv = _dwconv_1d(uv, w_dw[:, CONV_HALF:]).reshape(b_, GRID_W, rows, d - CONV_HALF).transpose(0, 2, 1, 3)
        u = jnp.concatenate([uh, uv], axis=-1).reshape(b_, t_, d)
    else:
        u = _dwconv_1d(u, w_dw)
    u = jax.nn.silu(layernorm(u + b_dw, ln_g, ln_b))
    return u @ w_pw2


def hier_moe(h, w_r1, b_r1, w_r2, b_r2, w_gate, w_up, w_down):
    n, d = h.shape
    f32 = jnp.float32
    lg1 = (h @ w_r1).astype(f32) + b_r1.astype(f32)
    p1 = jax.nn.softmax(lg1, axis=-1)
    grp = jnp.argmax(lg1, axis=-1)
    pg = jnp.take_along_axis(p1, grp[:, None], axis=1)
    lg2 = jnp.einsum('nd,gde->nge', h, w_r2).astype(f32) + b_r2.astype(f32)
    lg2 = jnp.take_along_axis(lg2, grp[:, None, None], axis=1)[:, 0]
    top_p, top_i = lax.top_k(jax.nn.softmax(lg2, axis=-1), MOE_TOPK)
    gate = pg * top_p / jnp.sum(top_p, axis=-1, keepdims=True)
    expert = grp[:, None].astype(jnp.int32) * MOE_PER_GROUP + top_i.astype(jnp.int32)

    e_flat = expert.reshape(-1)
    w_flat = gate.reshape(-1)
    tok_flat = jnp.repeat(jnp.arange(n, dtype=jnp.int32), MOE_TOPK)
    order = jnp.argsort(e_flat)
    e_s, tok_s, w_s = e_flat[order], tok_flat[order], w_flat[order]
    counts = jnp.zeros((MOE_EXPERTS,), jnp.int32).at[e_flat].add(1)
    padded = (counts + MOE_BLOCK - 1) // MOE_BLOCK * MOE_BLOCK
    pad_end = jnp.cumsum(padded)
    pad_start = pad_end - padded
    raw_start = jnp.cumsum(counts) - counts
    pos = pad_start[e_s] + jnp.arange(n * MOE_TOPK, dtype=jnp.int32) - raw_start[e_s]
    n_blocks = -(-(n * MOE_TOPK) // MOE_BLOCK) + MOE_EXPERTS
    slot_tok = jnp.full((n_blocks * MOE_BLOCK,), n, jnp.int32).at[pos].set(tok_s)
    slot_w = jnp.zeros((n_blocks * MOE_BLOCK,), f32).at[pos].set(w_s)
    blk_start = jnp.arange(n_blocks, dtype=jnp.int32) * MOE_BLOCK
    blk_e = jnp.minimum(jnp.searchsorted(pad_end, blk_start, side='right'), MOE_EXPERTS - 1)
    h_pad = jnp.concatenate([h, jnp.zeros((1, d), h.dtype)], axis=0)

    def run_block(args):
        e, toks = args
        xb = h_pad[toks]
        return (jax.nn.silu(xb @ w_gate[e]) * (xb @ w_up[e])) @ w_down[e]

    out = lax.map(run_block, (blk_e, slot_tok.reshape(n_blocks, MOE_BLOCK)))
    out = out.reshape(-1, d) * slot_w[:, None].astype(h.dtype)
    y = jnp.zeros((n + 1, d), h.dtype).at[slot_tok].add(out)
    return y[:n]


def setup_inputs(seed: int = 0) -> dict:
    key = jax.random.key(seed)
    ks = jax.random.split(key, 28)
    f32 = jnp.float32
    D = D_MODEL
    nrm = lambda k, shape, s: jax.random.normal(k, shape, f32) * s
    return {
        'x': nrm(ks[0], (BATCH, SEQ, D), 1.0),
        'c': nrm(ks[1], (BATCH, D), 1.0),
        'ctx': nrm(ks[2], (BATCH, CTX_LEN, D), 1.0),
        'c_ctx': nrm(ks[3], (D,), 1.0),
        'ada_w': nrm(ks[4], (DEPTH, D, 6 * D), 0.5 * D ** -0.5),
        'ada_b': nrm(ks[5], (DEPTH, 6 * D), 0.02),
        'norm1_g': 1.0 + nrm(ks[6], (DEPTH, D), 0.02),
        'norm2_g': 1.0 + nrm(ks[7], (DEPTH, D), 0.02),
        'hgrn_w_in': nrm(ks[8], (N_HGRN_LAYERS, D, 5 * D), D ** -0.5),
        'hgrn_lb': nrm(ks[9], (2, DEPTH + 1, D), 0.1),
        'hgrn_onorm_g': 1.0 + nrm(ks[10], (N_HGRN_LAYERS, HGRN_HEAD_V), 0.02),
        'hgrn_w_out': nrm(ks[11], (N_HGRN_LAYERS, D, D), D ** -0.5),
        'conv_w_pw1': nrm(ks[12], (N_CONV_LAYERS, D, 2 * D), D ** -0.5),
        'conv_w_dw': nrm(ks[13], (N_CONV_LAYERS, CONV_WIDTH, D), CONV_WIDTH ** -0.5),
        'conv_b_dw': nrm(ks[14], (N_CONV_LAYERS, D), 0.02),
        'conv_ln_g': 1.0 + nrm(ks[15], (N_CONV_LAYERS, D), 0.02),
        'conv_ln_b': nrm(ks[16], (N_CONV_LAYERS, D), 0.02),
        'conv_w_pw2': nrm(ks[17], (N_CONV_LAYERS, D, D), D ** -0.5),
        'moe_w_r1': nrm(ks[18], (DEPTH, D, MOE_GROUPS), D ** -0.5),
        'moe_b_r1': nrm(ks[19], (DEPTH, MOE_GROUPS), 0.01),
        'moe_w_r2': nrm(ks[20], (DEPTH, MOE_GROUPS, D, MOE_PER_GROUP), D ** -0.5),
        'moe_b_r2': nrm(ks[21], (DEPTH, MOE_GROUPS, MOE_PER_GROUP), 0.01),
        'moe_w_gate': nrm(ks[22], (DEPTH, MOE_EXPERTS, D, MOE_HIDDEN), D ** -0.5),
        'moe_w_up': nrm(ks[23], (DEPTH, MOE_EXPERTS, D, MOE_HIDDEN), D ** -0.5),
        'moe_w_down': nrm(ks[24], (DEPTH, MOE_EXPERTS, MOE_HIDDEN, D), MOE_HIDDEN ** -0.5),
        'final_g': 1.0 + nrm(ks[25], (D,), 0.02),
    }


def reference(x, c, ctx, c_ctx, ada_w, ada_b, norm1_g, norm2_g, hgrn_w_in, hgrn_lb, hgrn_onorm_g,
              hgrn_w_out, conv_w_pw1, conv_w_dw, conv_b_dw, conv_ln_g, conv_ln_b, conv_w_pw2,
              moe_w_r1, moe_b_r1, moe_w_r2, moe_b_r2, moe_w_gate, moe_w_up, moe_w_down, final_g):
    bsz, seq, d = x.shape
    lb_all = jnp.cumsum(jax.nn.softmax(hgrn_lb.astype(jnp.float32), axis=1), axis=1)
    s_c = jax.nn.silu(c)
    s_cc = jax.nn.silu(c_ctx)
    reads_ctx = [i % N_MIXERS == 0 for i in range(DEPTH)]
    h, hc = x, ctx
    for i in range(DEPTH):
        j = i // N_MIXERS
        ctx_later = any(reads_ctx[i + 1:])
        mod = jnp.split((s_c @ ada_w[i] + ada_b[i])[:, None, :], 6, axis=-1)
        mod_c = jnp.split((s_cc @ ada_w[i] + ada_b[i])[None, None, :], 6, axis=-1)
        a = modulate(rmsnorm(h, norm1_g[i]), mod[0], mod[1])
        ac = None
        if reads_ctx[i] or ctx_later:
            ac = modulate(rmsnorm(hc, norm1_g[i]), mod_c[0], mod_c[1])
        if i % N_MIXERS == 0:
            y, yc = hgrn2_mixer(a, ac, hgrn_w_in[j], lb_all[:, i], hgrn_onorm_g[j], hgrn_w_out[j], ctx_later)
        else:
            y = conformer_conv(a, conv_w_pw1[j], conv_w_dw[j], conv_b_dw[j], conv_ln_g[j], conv_ln_b[j],
                               conv_w_pw2[j], True)
            yc = None
            if ctx_later:
                yc = conformer_conv(ac, conv_w_pw1[j], conv_w_dw[j], conv_b_dw[j], conv_ln_g[j],
                                    conv_ln_b[j], conv_w_pw2[j], False)
        h = h + mod[2] * y
        b = modulate(rmsnorm(h, norm2_g[i]), mod[3], mod[4])
        moe_args = (moe_w_r1[i], moe_b_r1[i], moe_w_r2[i], moe_b_r2[i], moe_w_gate[i], moe_w_up[i], moe_w_down[i])
        if ctx_later:
            hc = hc + mod_c[2] * yc
            bc = modulate(rmsnorm(hc, norm2_g[i]), mod_c[3], mod_c[4])
            n_lat = bsz * seq
            z = hier_moe(jnp.concatenate([b.reshape(-1, d), bc.reshape(-1, d)], axis=0), *moe_args)
            h = h + mod[5] * z[:n_lat].reshape(bsz, seq, d)
            hc = hc + mod_c[5] * z[n_lat:].reshape(hc.shape)
        else:
            z = hier_moe(b.reshape(-1, d), *moe_args)
            h = h + mod[5] * z.reshape(bsz, seq, d)
    return rmsnorm(h, final_g)
```

```python
import functools

import jax
import jax.numpy as jnp
from jax import lax
from jax.experimental import pallas as pl
from jax.experimental.pallas import tpu as pltpu

F32 = jnp.float32
BF16 = jnp.bfloat16
U32 = jnp.uint32
I32 = jnp.int32

EPS = 1e-6
LANES = 128
HEAD_DIM = 128
SCAN_CHUNK = 64
GRID_W = 64
CONV_WIDTH = 31
MOE_GROUPS = 8
MOE_PER_GROUP = 8
MOE_EXPERTS = MOE_GROUPS * MOE_PER_GROUP
MOE_BLOCK = 256
MOD_ROWS = 16
VMEM_LIMIT = 56 * 1024 * 1024
SAFE_CHUNK_LOG_DECAY = -80.0
HI_MASK = 0xFFFF0000


def _pick(n, pref):
    t = pref
    while n % t:
        t //= 2
    return t


def _params(sem):
    return pltpu.CompilerParams(dimension_semantics=sem, vmem_limit_bytes=VMEM_LIMIT)


def _silu(x):
    return x * jax.nn.sigmoid(x)


def _dot(a, b):
    return jnp.dot(a, b, preferred_element_type=F32)


def _dot_nt(a, b):
    return lax.dot_general(a, b, (((1,), (1,)), ((), ())), preferred_element_type=F32)


def _dot_tn(a, b):
    return lax.dot_general(a, b, (((0,), (0,)), ((), ())), preferred_element_type=F32)


def _pack_bf16_pairs(x):
    n = x.shape[1] // 2
    bits = lax.bitcast_convert_type(x.astype(BF16).astype(F32), U32)
    return (bits[:, :n] >> 16) | (bits[:, n:] & jnp.uint32(HI_MASK))


def _unpack_bf16_pairs(w):
    lo = lax.bitcast_convert_type(w << 16, F32)
    hi = lax.bitcast_convert_type(w & jnp.uint32(HI_MASK), F32)
    return lo, hi


def _adaln_body(c_ref, w_ref, b_ref, o_ref):
    s = _silu(c_ref[...])
    o_ref[0] = _dot(s.astype(BF16), w_ref[0].astype(BF16)) + b_ref[0]


def _adaln(c_all, ada_w, ada_b):
    depth, d, n6 = ada_w.shape
    tn = _pick(n6, 1024)
    return pl.pallas_call(
        _adaln_body,
        grid=(depth, n6 // tn),
        in_specs=[
            pl.BlockSpec((MOD_ROWS, d), lambda l, j: (0, 0)),
            pl.BlockSpec((1, d, tn), lambda l, j: (l, 0, j)),
            pl.BlockSpec((1, 1, tn), lambda l, j: (l, 0, j)),
        ],
        out_specs=pl.BlockSpec((1, MOD_ROWS, tn), lambda l, j: (l, 0, j)),
        out_shape=jax.ShapeDtypeStruct((depth, MOD_ROWS, n6), F32),
        compiler_params=_params(("arbitrary", "arbitrary")),
        name="adaln",
    )(c_all, ada_w, ada_b.reshape(depth, 1, n6))


def _mod_spec(d, row_fn, sec):
    return pl.BlockSpec((1, 1, d), lambda b, i: (row_fn(b), 0, sec))


def _rms(x, g):
    return x * lax.rsqrt(jnp.mean(x * x, axis=-1, keepdims=True) + EPS) * g


def _norm_mod_body(x_ref, g_ref, sh_ref, sc_ref, o_ref):
    y = _rms(x_ref[0], g_ref[...])
    o_ref[0] = (y * (1.0 + sc_ref[0]) + sh_ref[0]).astype(o_ref.dtype)


def _norm_mod(x, g, mod3, row_fn, sec_shift, sec_scale):
    bsz, t, d = x.shape
    tm = _pick(t, 512)
    return pl.pallas_call(
        _norm_mod_body,
        grid=(bsz, t // tm),
        in_specs=[
            pl.BlockSpec((1, tm, d), lambda b, i: (b, i, 0)),
            pl.BlockSpec((1, d), lambda b, i: (0, 0)),
            _mod_spec(d, row_fn, sec_shift),
            _mod_spec(d, row_fn, sec_scale),
        ],
        out_specs=pl.BlockSpec((1, tm, d), lambda b, i: (b, i, 0)),
        out_shape=jax.ShapeDtypeStruct((bsz, t, d), BF16),
        compiler_params=_params(("arbitrary", "arbitrary")),
        name="norm_mod",
    )(x, g.reshape(1, d), mod3, mod3)


def _mm_body(a_ref, w_ref, o_ref, wb_ref):
    @pl.when(pl.program_id(1) == 0)
    def _():
        wb_ref[...] = w_ref[...].astype(BF16)

    o_ref[...] = _dot(a_ref[...], wb_ref[...]).astype(o_ref.dtype)


def _matmul(a, w):
    m, k = a.shape
    n = w.shape[1]
    tm, tn = _pick(m, 512), _pick(n, 1024)
    return pl.pallas_call(
        _mm_body,
        grid=(n // tn, m // tm),
        in_specs=[
            pl.BlockSpec((tm, k), lambda j, i: (i, 0)),
            pl.BlockSpec((k, tn), lambda j, i: (0, j)),
        ],
        out_specs=pl.BlockSpec((tm, tn), lambda j, i: (i, j)),
        out_shape=jax.ShapeDtypeStruct((m, n), BF16),
        scratch_shapes=[pltpu.VMEM((k, tn), BF16)],
        compiler_params=_params(("arbitrary", "arbitrary")),
        name="matmul",
    )(a, w)


def _mm_glu_body(a_ref, wv_ref, wg_ref, o_ref, wvb_ref, wgb_ref):
    @pl.when(pl.program_id(1) == 0)
    def _():
        wvb_ref[...] = wv_ref[...].astype(BF16)
        wgb_ref[...] = wg_ref[...].astype(BF16)

    a = a_ref[...]
    val = _dot(a, wvb_ref[...])
    gate = _dot(a, wgb_ref[...])
    o_ref[...] = (val * jax.nn.sigmoid(gate)).astype(o_ref.dtype)


def _matmul_glu(a, w):
    m, k = a.shape
    n = w.shape[1] // 2
    tm, tn = _pick(m, 512), _pick(n, 512)
    nj = n // tn
    return pl.pallas_call(
        _mm_glu_body,
        grid=(nj, m // tm),
        in_specs=[
            pl.BlockSpec((tm, k), lambda j, i: (i, 0)),
            pl.BlockSpec((k, tn), lambda j, i: (0, j)),
            pl.BlockSpec((k, tn), lambda j, i: (0, j + nj)),
        ],
        out_specs=pl.BlockSpec((tm, tn), lambda j, i: (i, j)),
        out_shape=jax.ShapeDtypeStruct((m, n), BF16),
        scratch_shapes=[pltpu.VMEM((k, tn), BF16), pltpu.VMEM((k, tn), BF16)],
        compiler_params=_params(("arbitrary", "arbitrary")),
        name="matmul_glu",
    )(a, w, w)


def _scan_body(qc_ref, vc_ref, zc_ref, ql_ref, vl_ref, zl_ref, lb_ref, o_ref,
               st_ref, b_sc, q_sc, k_sc, v_sc, *, nc, n_sub, heads):
    chunk = SCAN_CHUNK
    direction = pl.program_id(1)
    step = pl.program_id(2)
    fwd = direction == 0

    @pl.when(step == 0)
    def _():
        st_ref[...] = jnp.zeros_like(st_ref)

    lb = lb_ref[0]
    row = lax.broadcasted_iota(I32, (chunk, chunk), 0)
    col = lax.broadcasted_iota(I32, (chunk, chunk), 1)
    sign = 1 - 2 * direction
    tri = (row - col) * sign >= 0
    tri_b = jnp.where(tri, 1.0, 0.0).astype(BF16)
    t_idx = lax.broadcasted_iota(I32, (chunk, 1), 0)

    def process(q_ref, v_ref, z_ref, emit):
        def one_chunk(ci, carry):
            cidx = jnp.where(fwd, ci, n_sub - 1 - ci)
            rows = pl.ds(pl.multiple_of(cidx * chunk, chunk), chunk)
            z = z_ref[0, rows, :].astype(F32)
            f = lb + (1.0 - lb) * jax.nn.sigmoid(z)
            lf = jnp.log(f)
            kk = 1.0 - f
            hi = lf.astype(BF16)
            lo = (lf - hi.astype(F32)).astype(BF16)
            b = _dot(tri_b, hi) + _dot(tri_b, lo)
            tot = jnp.where(fwd, b[chunk - 1:chunk, :], b[0:1, :])
            ke = kk * jnp.exp(tot - b)
            ke_b = ke.astype(BF16)
            etot = jnp.exp(tot)
            v = v_ref[0, rows, :]
            if emit:
                q = q_ref[0, rows, :].astype(F32)
                qs = _silu(q)
                qe = qs * jnp.exp(b)
                qe_b = qe.astype(BF16)
                g = jnp.exp(-0.5 * tot)
                qh = (qe * g).astype(BF16)
                kh = (ke * g).astype(BF16)
                for h in range(heads):
                    sl = slice(h * HEAD_DIM, (h + 1) * HEAD_DIM)
                    o_inter = _dot_nt(qe_b[:, sl], st_ref[h].astype(BF16))
                    sc = jnp.where(tri, _dot_nt(qh[:, sl], kh[:, sl]), 0.0)
                    o_intra = _dot(sc.astype(BF16), v[:, sl])
                    o_ref[0, 0, rows, sl] = (o_inter + o_intra).astype(o_ref.dtype)

                @pl.when(jnp.min(tot) < SAFE_CHUNK_LOG_DECAY)
                def _():
                    b_sc[...] = b
                    q_sc[...] = qs
                    k_sc[...] = kk
                    v_sc[...] = v.astype(F32)

                    def head_loop(h, c2):
                        lanes = pl.ds(pl.multiple_of(h * HEAD_DIM, HEAD_DIM), HEAD_DIM)
                        b_h = b_sc[:, lanes]
                        q_h = q_sc[:, lanes]
                        o_inter = _dot_nt((q_h * jnp.exp(b_h)).astype(BF16),
                                          st_ref[h].astype(BF16))

                        k_h = k_sc[:, lanes]
                        v_h = v_sc[:, lanes]

                        def key_loop(s, acc):
                            pick = t_idx == s
                            b_s = jnp.sum(jnp.where(pick, b_h, 0.0), axis=0, keepdims=True)
                            k_s = jnp.sum(jnp.where(pick, k_h, 0.0), axis=0, keepdims=True)
                            v_s = jnp.sum(jnp.where(pick, v_h, 0.0), axis=0, keepdims=True)
                            valid = (t_idx - s) * sign >= 0
                            dec = jnp.where(valid, jnp.exp(jnp.where(valid, b_h - b_s, 0.0)), 0.0)
                            w = jnp.sum(q_h * dec * k_s, axis=1, keepdims=True)
                            return acc + w * v_s

                        o_intra = lax.fori_loop(0, chunk, key_loop,
                                                jnp.zeros((chunk, HEAD_DIM), F32))
                        o_ref[0, 0, rows, lanes] = (o_inter + o_intra).astype(o_ref.dtype)
                        return c2

                    lax.fori_loop(0, heads, head_loop, 0)

            for h in range(heads):
                sl = slice(h * HEAD_DIM, (h + 1) * HEAD_DIM)
                st_ref[h] = st_ref[h] * etot[:, sl] + _dot_tn(v[:, sl], ke_b[:, sl])
            return carry

        lax.fori_loop(0, n_sub, one_chunk, 0)

    @pl.when(step < nc)
    def _():
        process(qc_ref, vc_ref, zc_ref, False)

    @pl.when(step >= nc)
    def _():
        process(ql_ref, vl_ref, zl_ref, True)


def _scan(p_ctx, p_lat, lb):
    bsz, t_lat, d5 = p_lat.shape
    t_ctx = p_ctx.shape[1]
    d = d5 // 5
    heads = d // HEAD_DIM
    tb = _pick(t_ctx, 256)
    assert t_lat % tb == 0 and tb % SCAN_CHUNK == 0
    nc, nl = t_ctx // tb, t_lat // tb
    n_sub = tb // SCAN_CHUNK

    def ctx_blk(dr, s):
        sc = jnp.minimum(s, nc - 1)
        return jnp.where(dr == 0, sc, nc - 1 - sc)

    def lat_blk(dr, s):
        sl = jnp.maximum(s - nc, 0)
        return jnp.where(dr == 0, sl, nl - 1 - sl)

    def cspec(sec_fn):
        return pl.BlockSpec((1, tb, d), lambda b, dr, s: (b, ctx_blk(dr, s), sec_fn(dr)))

    def lspec(sec_fn):
        return pl.BlockSpec((1, tb, d), lambda b, dr, s: (b, lat_blk(dr, s), sec_fn(dr)))

    q_sec = lambda dr: 0
    v_sec = lambda dr: 1
    z_sec = lambda dr: 2 + dr
    return pl.pallas_call(
        functools.partial(_scan_body, nc=nc, n_sub=n_sub, heads=heads),
        grid=(bsz, 2, nc + nl),
        in_specs=[cspec(q_sec), cspec(v_sec), cspec(z_sec),
                  lspec(q_sec), lspec(v_sec), lspec(z_sec),
                  pl.BlockSpec((1, 1, d), lambda b, dr, s: (dr, 0, 0))],
        out_specs=pl.BlockSpec((1, 1, tb, d), lambda b, dr, s: (dr, b, lat_blk(dr, s), 0)),
        out_shape=jax.ShapeDtypeStruct((2, bsz, t_lat, d), BF16),
        scratch_shapes=[pltpu.VMEM((heads, HEAD_DIM, HEAD_DIM), F32),
                        pltpu.VMEM((SCAN_CHUNK, d), F32),
                        pltpu.VMEM((SCAN_CHUNK, d), F32),
                        pltpu.VMEM((SCAN_CHUNK, d), F32),
                        pltpu.VMEM((SCAN_CHUNK, d), F32)],
        compiler_params=_params(("arbitrary", "arbitrary", "arbitrary")),
        name="hgrn_scan",
    )(p_ctx, p_ctx, p_ctx, p_lat, p_lat, p_lat, lb.reshape(2, 1, d))


def _hgrn_readout_body(of_ref, ob_ref, g_ref, on_ref, w_ref, h_ref, gt_ref, o_ref,
                       wb_ref, a_ref, *, heads):
    @pl.when((pl.program_id(0) == 0) & (pl.program_id(1) == 0))
    def _():
        wb_ref[...] = w_ref[...].astype(BF16)

    on = on_ref[...]
    for h in range(heads):
        sl = slice(h * HEAD_DIM, (h + 1) * HEAD_DIM)
        o = of_ref[0, 0, :, sl].astype(F32) + ob_ref[0, 0, :, sl].astype(F32)
        o = _rms(o, on)
        a_ref[:, sl] = (o * _silu(g_ref[0, :, sl].astype(F32))).astype(BF16)
    y = _dot(a_ref[...], wb_ref[...])
    o_ref[0] = h_ref[0] + gt_ref[0] * y


def _hgrn_readout(o2, p_lat, onorm_g, w_out, h, mod3, row_fn, sec_gate):
    bsz, t, d = h.shape
    heads = d // HEAD_DIM
    tm = _pick(t, 256)
    return pl.pallas_call(
        functools.partial(_hgrn_readout_body, heads=heads),
        grid=(bsz, t // tm),
        in_specs=[
            pl.BlockSpec((1, 1, tm, d), lambda b, i: (0, b, i, 0)),
            pl.BlockSpec((1, 1, tm, d), lambda b, i: (1, b, i, 0)),
            pl.BlockSpec((1, tm, d), lambda b, i: (b, i, 4)),
            pl.BlockSpec((1, HEAD_DIM), lambda b, i: (0, 0)),
            pl.BlockSpec((d, d), lambda b, i: (0, 0), pipeline_mode=pl.Buffered(1)),
            pl.BlockSpec((1, tm, d), lambda b, i: (b, i, 0)),
            _mod_spec(d, row_fn, sec_gate),
        ],
        out_specs=pl.BlockSpec((1, tm, d), lambda b, i: (b, i, 0)),
        out_shape=jax.ShapeDtypeStruct((bsz, t, d), F32),
        scratch_shapes=[pltpu.VMEM((d, d), BF16), pltpu.VMEM((tm, d), BF16)],
        compiler_params=_params(("arbitrary", "arbitrary")),
        name="hgrn_readout",
    )(o2, o2, p_lat, onorm_g.reshape(1, HEAD_DIM), w_out, h, mod3)


def _conv_out_body(v_ref, lg_ref, lbias_ref, w_ref, h_ref, gt_ref, o_ref, wb_ref):
    @pl.when((pl.program_id(0) == 0) & (pl.program_id(1) == 0))
    def _():
        wb_ref[...] = w_ref[...].astype(BF16)

    x = v_ref[0].astype(F32)
    mu = jnp.mean(x, axis=-1, keepdims=True)
    xc = x - mu
    y = xc * lax.rsqrt(jnp.mean(xc * xc, axis=-1, keepdims=True) + EPS)
    y = _silu(y * lg_ref[...] + lbias_ref[...])
    o_ref[0] = h_ref[0] + gt_ref[0] * _dot(y.astype(BF16), wb_ref[...])


def _conv_out(v, ln_g, ln_b, w, h, mod3, row_fn, sec_gate):
    bsz, t, d = h.shape
    tm = _pick(t, 256)
    return pl.pallas_call(
        _conv_out_body,
        grid=(bsz, t // tm),
        in_specs=[
            pl.BlockSpec((1, tm, d), lambda b, i: (b, i, 0)),
            pl.BlockSpec((1, d), lambda b, i: (0, 0)),
            pl.BlockSpec((1, d), lambda b, i: (0, 0)),
            pl.BlockSpec((d, d), lambda b, i: (0, 0), pipeline_mode=pl.Buffered(1)),
            pl.BlockSpec((1, tm, d), lambda b, i: (b, i, 0)),
            _mod_spec(d, row_fn, sec_gate),
        ],
        out_specs=pl.BlockSpec((1, tm, d), lambda b, i: (b, i, 0)),
        out_shape=jax.ShapeDtypeStruct((bsz, t, d), F32),
        scratch_shapes=[pltpu.VMEM((d, d), BF16)],
        compiler_params=_params(("arbitrary", "arbitrary")),
        name="conv_out",
    )(v, ln_g.reshape(1, d), ln_b.reshape(1, d), w, h, mod3)


def _dwconv_body(u_ref, w_ref, b_ref, o_ref, pad_ref, *, rows, cb, half_blocks):
    half = CONV_WIDTH // 2
    j = pl.program_id(1)
    n_lane = cb // LANES
    bias = b_ref[...]

    def taps(read, r):
        for lb in range(n_lane):
            ln = slice(lb * LANES, (lb + 1) * LANES)
            acc = jnp.zeros((GRID_W, LANES), F32)
            for k in range(CONV_WIDTH):
                acc = acc + w_ref[k:k + 1, ln] * read(r, k, ln)
            o_ref[0, pl.ds(pl.multiple_of(r * GRID_W, GRID_W), GRID_W), ln] = (
                acc + bias[:, ln]).astype(o_ref.dtype)

    @pl.when(j < half_blocks)
    def _():
        zeros = jnp.zeros((rows, 16, cb), F32)
        pad_ref[0:rows, 0:16, :] = zeros
        pad_ref[0:rows, 16 + GRID_W:32 + GRID_W, :] = zeros
        pad_ref[0:rows, 16:16 + GRID_W, :] = u_ref[0].astype(F32).reshape(rows, GRID_W, cb)

        def row_loop(r, c):
            taps(lambda r_, k, ln: pad_ref[r_, 16 - half + k:16 - half + k + GRID_W, ln], r)
            return c

        lax.fori_loop(0, rows, row_loop, 0)

    @pl.when(j >= half_blocks)
    def _():
        zeros = jnp.zeros((half, GRID_W, cb), F32)
        pad_ref[0:half, 0:GRID_W, :] = zeros
        pad_ref[half + rows:2 * half + rows, 0:GRID_W, :] = zeros
        pad_ref[half:half + rows, 0:GRID_W, :] = u_ref[0].astype(F32).reshape(rows, GRID_W, cb)

        def row_loop(r, c):
            taps(lambda r_, k, ln: pad_ref[r_ + k, 0:GRID_W, ln], r)
            return c

        lax.fori_loop(0, rows, row_loop, 0)


def _dwconv(u, w_dw, b_dw):
    bsz, t, d = u.shape
    rows = t // GRID_W
    cb = _pick(d // 2, 256)
    half_blocks = (d // 2) // cb
    return pl.pallas_call(
        functools.partial(_dwconv_body, rows=rows, cb=cb, half_blocks=half_blocks),
        grid=(bsz, d // cb),
        in_specs=[
            pl.BlockSpec((1, t, cb), lambda b, j: (b, 0, j)),
            pl.BlockSpec((CONV_WIDTH, cb), lambda b, j: (0, j)),
            pl.BlockSpec((1, cb), lambda b, j: (0, j)),
        ],
        out_specs=pl.BlockSpec((1, t, cb), lambda b, j: (b, 0, j)),
        out_shape=jax.ShapeDtypeStruct((bsz, t, d), BF16),
        scratch_shapes=[pltpu.VMEM((rows + 2 * (CONV_WIDTH // 2) + 2, GRID_W + 32, cb), F32)],
        compiler_params=_params(("arbitrary", "arbitrary")),
        name="dwconv",
    )(u, w_dw, b_dw.reshape(1, d))


def _router_body(h_ref, g_ref, sh_ref, sc_ref, wr_ref, br_ref,
                 bp_ref, ri_ref, rw_ref, cnt_ref, carry_ref):
    tm = h_ref.shape[1]

    @pl.when((pl.program_id(0) == 0) & (pl.program_id(1) == 0))
    def _():
        carry_ref[...] = jnp.zeros_like(carry_ref)

    bmod = _rms(h_ref[0], g_ref[...]) * (1.0 + sc_ref[0]) + sh_ref[0]
    bp_ref[...] = _pack_bf16_pairs(bmod)

    b_hi = bmod.astype(BF16)
    b_lo = (bmod - b_hi.astype(F32)).astype(BF16)
    w = wr_ref[...]
    w_hi = w.astype(BF16)
    w_lo = (w - w_hi.astype(F32)).astype(BF16)
    logits = _dot(b_hi, w_hi) + _dot(b_hi, w_lo) + _dot(b_lo, w_hi) + br_ref[...]

    lane_i = lax.broadcasted_iota(I32, (tm, LANES), 1)
    lane = lane_i.astype(F32)
    neg = jnp.float32(-jnp.inf)
    big = jnp.float32(1 << 20)
    is_grp = (lane_i >= MOE_EXPERTS) & (lane_i < MOE_EXPERTS + MOE_GROUPS)
    m1 = jnp.max(jnp.where(is_grp, logits, neg), axis=1, keepdims=True)
    grp = jnp.min(jnp.where(is_grp & (logits == m1), lane - MOE_EXPERTS, big), axis=1, keepdims=True)
    pg = 1.0 / jnp.sum(jnp.where(is_grp, jnp.exp(logits - m1), 0.0), axis=1, keepdims=True)

    sel = (lane >= grp * MOE_PER_GROUP) & (lane < (grp + 1) * MOE_PER_GROUP)
    m2 = jnp.max(jnp.where(sel, logits, neg), axis=1, keepdims=True)
    ea = jnp.min(jnp.where(sel & (logits == m2), lane, big), axis=1, keepdims=True)
    sel2 = sel & (lane != ea)
    m3 = jnp.max(jnp.where(sel2, logits, neg), axis=1, keepdims=True)
    eb = jnp.min(jnp.where(sel2 & (logits == m3), lane, big), axis=1, keepdims=True)
    tr = jnp.exp(m3 - m2)
    wa = pg / (1.0 + tr)
    wb = wa * tr

    hit_a = lane == ea
    hit_b = lane == eb
    onehot = jnp.where(hit_a | hit_b, 1.0, 0.0)
    r_i = lax.broadcasted_iota(I32, (tm, tm), 0)
    c_i = lax.broadcasted_iota(I32, (tm, tm), 1)
    lower = jnp.where(r_i > c_i, 1.0, 0.0).astype(BF16)
    before = _dot(lower, onehot.astype(BF16)) + carry_ref[...]
    rank_a = jnp.sum(jnp.where(hit_a, before, 0.0), axis=1, keepdims=True).astype(I32)
    rank_b = jnp.sum(jnp.where(hit_b, before, 0.0), axis=1, keepdims=True).astype(I32)
    carry = carry_ref[...] + jnp.sum(onehot, axis=0, keepdims=True)
    carry_ref[...] = carry
    cnt_ref[...] = carry

    ri_ref[...] = jnp.where(lane_i == 0, ea.astype(I32), jnp.where(lane_i == 1, eb.astype(I32),
                            jnp.where(lane_i == 2, rank_a, jnp.where(lane_i == 3, rank_b, 0))))
    rw_ref[...] = jnp.where(lane_i == 0, wa, jnp.where(lane_i == 1, wb, 0.0))


def _router(h, g, mod3, row_fn, sec_shift, sec_scale, w_r, b_r):
    bsz, t, d = h.shape
    tm = _pick(t, 256)
    nt = t // tm
    n = bsz * t
    return pl.pallas_call(
        _router_body,
        grid=(bsz, nt),
        in_specs=[
            pl.BlockSpec((1, tm, d), lambda b, i: (b, i, 0)),
            pl.BlockSpec((1, d), lambda b, i: (0, 0)),
            _mod_spec(d, row_fn, sec_shift),
            _mod_spec(d, row_fn, sec_scale),
            pl.BlockSpec((d, LANES), lambda b, i: (0, 0)),
            pl.BlockSpec((1, LANES), lambda b, i: (0, 0)),
        ],
        out_specs=[
            pl.BlockSpec((tm, d // 2), lambda b, i: (b * nt + i, 0)),
            pl.BlockSpec((tm, LANES), lambda b, i: (b * nt + i, 0)),
            pl.BlockSpec((tm, LANES), lambda b, i: (b * nt + i, 0)),
            pl.BlockSpec((1, LANES), lambda b, i: (0, 0)),
        ],
        out_shape=[
            jax.ShapeDtypeStruct((n, d // 2), U32),
            jax.ShapeDtypeStruct((n, LANES), I32),
            jax.ShapeDtypeStruct((n, LANES), F32),
            jax.ShapeDtypeStruct((1, LANES), F32),
        ],
        scratch_shapes=[pltpu.VMEM((1, LANES), F32)],
        compiler_params=_params(("arbitrary", "arbitrary")),
        name="router",
    )(h, g.reshape(1, d), mod3, mod3, w_r, b_r)


def _dispatch_body(pos_ref, bp_ref, xs_ref, sem, *, tm):
    base = pl.program_id(0) * tm

    def row_copy(r, p):
        return pltpu.make_async_copy(bp_ref.at[pl.ds(r, 1), :], xs_ref.at[pl.ds(p, 1), :], sem)

    def issue(r, c):
        for k in range(2):
            row_copy(r, pos_ref[2 * (base + r) + k]).start()
        return c

    lax.fori_loop(0, tm, issue, 0)

    def drain(r, c):
        for k in range(2):
            row_copy(r, pos_ref[2 * (base + r) + k]).wait()
        return c

    lax.fori_loop(0, tm, drain, 0)


def _dispatch(pos, bp, n_slots):
    n, dh = bp.shape
    tm = _pick(n, 512)
    return pl.pallas_call(
        functools.partial(_dispatch_body, tm=tm),
        grid_spec=pltpu.PrefetchScalarGridSpec(
            num_scalar_prefetch=1,
            grid=(n // tm,),
            in_specs=[pl.BlockSpec((tm, dh), lambda i, pos_ref: (i, 0))],
            out_specs=pl.BlockSpec(memory_space=pl.ANY),
            scratch_shapes=[pltpu.SemaphoreType.DMA],
        ),
        out_shape=jax.ShapeDtypeStruct((n_slots, dh), U32),
        compiler_params=_params(("arbitrary",)),
        name="moe_dispatch",
    )(pos, bp)


def _expert_body(meta_ref, x_ref, wg_ref, wu_ref, wd_ref, y_ref, wgb_ref, wub_ref, wdb_ref):
    i = pl.program_id(0)
    dh = x_ref.shape[1]

    @pl.when(i < meta_ref[3, 0])
    def _():
        @pl.when(meta_ref[2, i] == 1)
        def _():
            wgb_ref[...] = wg_ref[0].astype(BF16)
            wub_ref[...] = wu_ref[0].astype(BF16)
            wdb_ref[...] = wd_ref[0].astype(BF16)

        rowid = lax.broadcasted_iota(I32, (MOE_BLOCK, 1), 0)
        xw = jnp.where(rowid < meta_ref[1, i], x_ref[...], jnp.uint32(0))
        x_lo, x_hi = _unpack_bf16_pairs(xw)
        x_lo = x_lo.astype(BF16)
        x_hi = x_hi.astype(BF16)
        gate = _dot(x_lo, wgb_ref[0:dh, :]) + _dot(x_hi, wgb_ref[dh:2 * dh, :])
        up = _dot(x_lo, wub_ref[0:dh, :]) + _dot(x_hi, wub_ref[dh:2 * dh, :])
        mid = (_silu(gate) * up).astype(BF16)
        y_ref[...] = _pack_bf16_pairs(_dot(mid, wdb_ref[...]))


def _experts(meta, xs, w_gate, w_up, w_down):
    n_slots, dh = xs.shape
    _, d, hid = w_gate.shape
    n_blocks = n_slots // MOE_BLOCK

    def blk(i, m):
        return jnp.minimum(i, m[3, 0] - 1)

    return pl.pallas_call(
        _expert_body,
        grid_spec=pltpu.PrefetchScalarGridSpec(
            num_scalar_prefetch=1,
            grid=(n_blocks,),
            in_specs=[
                pl.BlockSpec((MOE_BLOCK, dh), lambda i, m: (blk(i, m), 0)),
                pl.BlockSpec((1, d, hid), lambda i, m: (m[0, blk(i, m)], 0, 0)),
                pl.BlockSpec((1, d, hid), lambda i, m: (m[0, blk(i, m)], 0, 0)),
                pl.BlockSpec((1, hid, d), lambda i, m: (m[0, blk(i, m)], 0, 0)),
            ],
            out_specs=pl.BlockSpec((MOE_BLOCK, dh), lambda i, m: (blk(i, m), 0)),
            scratch_shapes=[pltpu.VMEM((d, hid), BF16), pltpu.VMEM((d, hid), BF16),
                            pltpu.VMEM((hid, d), BF16)],
        ),
        out_shape=jax.ShapeDtypeStruct((n_slots, dh), U32),
        compiler_params=_params(("arbitrary",)),
        name="moe_experts",
    )(meta, xs, w_gate, w_up, w_down)


def _combine_body(pos_ref, h_ref, rw_ref, gt_ref, fg_ref, ys_ref, o_ref, ya_ref, yb_ref, sem,
                  *, tm, nt, final):
    base = (pl.program_id(0) * nt + pl.program_id(1)) * tm
    dh = ya_ref.shape[1]

    def row_copy(r, k):
        dst = ya_ref if k == 0 else yb_ref
        p = pos_ref[2 * (base + r) + k]
        return pltpu.make_async_copy(ys_ref.at[pl.ds(p, 1), :], dst.at[pl.ds(r, 1), :], sem)

    def issue(r, c):
        for k in range(2):
            row_copy(r, k).start()
        return c

    lax.fori_loop(0, tm, issue, 0)

    def drain(r, c):
        for k in range(2):
            row_copy(r, k).wait()
        return c

    lax.fori_loop(0, tm, drain, 0)

    wa = rw_ref[:, 0:1]
    wb = rw_ref[:, 1:2]
    a_lo, a_hi = _unpack_bf16_pairs(ya_ref[...])
    b_lo, b_hi = _unpack_bf16_pairs(yb_ref[...])
    gt = gt_ref[0]
    lo = h_ref[0, :, 0:dh] + gt[:, 0:dh] * (wa * a_lo + wb * b_lo)
    hi = h_ref[0, :, dh:2 * dh] + gt[:, dh:2 * dh] * (wa * a_hi + wb * b_hi)
    if final:
        ms = (jnp.sum(lo * lo, axis=-1, keepdims=True)
              + jnp.sum(hi * hi, axis=-1, keepdims=True)) / (2 * dh)
        inv = lax.rsqrt(ms + EPS)
        lo = lo * inv * fg_ref[:, 0:dh]
        hi = hi * inv * fg_ref[:, dh:2 * dh]
    o_ref[0, :, 0:dh] = lo
    o_ref[0, :, dh:2 * dh] = hi


def _combine(pos, h, rw, mod3, row_fn, sec_gate, final_g, ys, final):
    bsz, t, d = h.shape
    tm = _pick(t, 256)
    nt = t // tm
    return pl.pallas_call(
        functools.partial(_combine_body, tm=tm, nt=nt, final=final),
        grid_spec=pltpu.PrefetchScalarGridSpec(
            num_scalar_prefetch=1,
            grid=(bsz, nt),
            in_specs=[
                pl.BlockSpec((1, tm, d), lambda b, i, p: (b, i, 0)),
                pl.BlockSpec((tm, LANES), lambda b, i, p: (b * nt + i, 0)),
                pl.BlockSpec((1, 1, d), lambda b, i, p: (row_fn(b), 0, sec_gate)),
                pl.BlockSpec((1, d), lambda b, i, p: (0, 0)),
                pl.BlockSpec(memory_space=pl.ANY),
            ],
            out_specs=pl.BlockSpec((1, tm, d), lambda b, i, p: (b, i, 0)),
            scratch_shapes=[pltpu.VMEM((tm, d // 2), U32), pltpu.VMEM((tm, d // 2), U32),
                            pltpu.SemaphoreType.DMA],
        ),
        out_shape=jax.ShapeDtypeStruct((bsz, t, d), F32),
        compiler_params=_params(("arbitrary", "arbitrary")),
        name="moe_combine",
    )(pos, h, rw, mod3, final_g.reshape(1, d), ys)


def _hier_moe(h, g, mod3, row_fn, w_r1, b_r1, w_r2, b_r2, w_gate, w_up, w_down, final_g, final):
    bsz, t, d = h.shape
    n = bsz * t
    w_r = jnp.concatenate([jnp.transpose(w_r2, (1, 0, 2)).reshape(d, MOE_EXPERTS), w_r1], axis=1)
    w_r = jnp.pad(w_r, ((0, 0), (0, LANES - w_r.shape[1])))
    b_r = jnp.pad(jnp.concatenate([b_r2.reshape(-1), b_r1]), (0, LANES - MOE_EXPERTS - MOE_GROUPS))
    bp, ri, rw, cnt = _router(h, g, mod3, row_fn, 3, 4, w_r, b_r.reshape(1, LANES))

    counts = cnt[0, :MOE_EXPERTS].astype(I32)
    padded = (counts + MOE_BLOCK - 1) // MOE_BLOCK * MOE_BLOCK
    pad_end = jnp.cumsum(padded)
    pad_start = pad_end - padded
    n_blocks = -(-(n * 2) // MOE_BLOCK) + MOE_EXPERTS
    blk_start = jnp.arange(n_blocks, dtype=I32) * MOE_BLOCK
    blk_e = jnp.minimum(jnp.searchsorted(pad_end, blk_start, side="right"), MOE_EXPERTS - 1).astype(I32)
    blk_valid = jnp.clip(pad_start[blk_e] + counts[blk_e] - blk_start, 0, MOE_BLOCK)
    blk_first = (blk_start == pad_start[blk_e]).astype(I32)
    n_used = jnp.full((n_blocks,), pad_end[-1] // MOE_BLOCK, I32)
    meta = jnp.stack([blk_e, blk_valid, blk_first, n_used])
    pos = (pad_start[ri[:, 0:2]] + ri[:, 2:4]).reshape(-1)

    xs = _dispatch(pos, bp, n_blocks * MOE_BLOCK)
    ys = _experts(meta, xs, w_gate, w_up, w_down)
    return _combine(pos, h, rw, mod3, row_fn, 5, final_g, ys, final)


def kernel(x, c, ctx, c_ctx, ada_w, ada_b, norm1_g, norm2_g, hgrn_w_in, hgrn_lb, hgrn_onorm_g,
           hgrn_w_out, conv_w_pw1, conv_w_dw, conv_b_dw, conv_ln_g, conv_ln_b, conv_w_pw2,
           moe_w_r1, moe_b_r1, moe_w_r2, moe_b_r2, moe_w_gate, moe_w_up, moe_w_down, final_g):
    bsz, t, d = x.shape
    t_ctx = ctx.shape[1]
    depth = ada_w.shape[0]
    assert depth == 2 and bsz < MOD_ROWS and d % (2 * LANES) == 0 and t % GRID_W == 0

    c_all = jnp.concatenate([c, c_ctx[None, :], jnp.zeros((MOD_ROWS - bsz - 1, d), F32)], axis=0)
    mod = _adaln(c_all, ada_w, ada_b)
    mod3 = mod.reshape(depth * MOD_ROWS, 1, 6 * d)
    lat_row = lambda l: (lambda b: l * MOD_ROWS + b)
    ctx_row = lambda l: (lambda b: l * MOD_ROWS + bsz)
    lb_all = jnp.cumsum(jax.nn.softmax(hgrn_lb.astype(F32), axis=1), axis=1)

    def moe(h, l, final):
        return _hier_moe(h, norm2_g[l], mod3, lat_row(l), moe_w_r1[l], moe_b_r1[l], moe_w_r2[l],
                         moe_b_r2[l], moe_w_gate[l], moe_w_up[l], moe_w_down[l], final_g, final)

    a_lat = _norm_mod(x, norm1_g[0], mod3, lat_row(0), 0, 1)
    a_ctx = _norm_mod(ctx, norm1_g[0], mod3, ctx_row(0), 0, 1)
    p_lat = _matmul(a_lat.reshape(bsz * t, d), hgrn_w_in[0]).reshape(bsz, t, 5 * d)
    p_ctx = _matmul(a_ctx.reshape(bsz * t_ctx, d), hgrn_w_in[0]).reshape(bsz, t_ctx, 5 * d)
    o2 = _scan(p_ctx, p_lat, lb_all[:, 0])
    h = _hgrn_readout(o2, p_lat, hgrn_onorm_g[0], hgrn_w_out[0], x, mod3, lat_row(0), 2)
    h = moe(h, 0, False)

    a = _norm_mod(h, norm1_g[1], mod3, lat_row(1), 0, 1)
    u = _matmul_glu(a.reshape(bsz * t, d), conv_w_pw1[0]).reshape(bsz, t, d)
    v = _dwconv(u, conv_w_dw[0], conv_b_dw[0])
    h = _conv_out(v, conv_ln_g[0], conv_ln_b[0], conv_w_pw2[0], h, mod3, lat_row(1), 2)
    return moe(h, 1, True)
```

```python
import functools

import jax
import jax.numpy as jnp
from jax import lax
from jax.experimental import pallas as pl
from jax.experimental.pallas import tpu as pltpu

F32 = jnp.float32
BF16 = jnp.bfloat16
U32 = jnp.uint32
I32 = jnp.int32

EPS = 1e-6
LANES = 128
HEAD_DIM = 128
SCAN_CHUNK = 64
GRID_W = 64
CONV_WIDTH = 31
MOE_GROUPS = 8
MOE_PER_GROUP = 8
MOE_EXPERTS = MOE_GROUPS * MOE_PER_GROUP
MOE_BLOCK = 256
MOD_ROWS = 16
VMEM_LIMIT = 56 * 1024 * 1024
SAFE_CHUNK_LOG_DECAY = -80.0
HI_MASK = 0xFFFF0000


def _pick(n, pref):
    t = pref
    while n % t:
        t //= 2
    return t


def _params(sem):
    return pltpu.CompilerParams(dimension_semantics=sem, vmem_limit_bytes=VMEM_LIMIT)


def _silu(x):
    return x * jax.nn.sigmoid(x)


def _dot(a, b):
    return jnp.dot(a, b, preferred_element_type=F32)


def _dot_nt(a, b):
    return lax.dot_general(a, b, (((1,), (1,)), ((), ())), preferred_element_type=F32)


def _dot_tn(a, b):
    return lax.dot_general(a, b, (((0,), (0,)), ((), ())), preferred_element_type=F32)


def _pack_bf16_pairs(x):
    n = x.shape[1] // 2
    bits = lax.bitcast_convert_type(x.astype(BF16).astype(F32), U32)
    return (bits[:, :n] >> 16) | (bits[:, n:] & jnp.uint32(HI_MASK))


def _unpack_bf16_pairs(w):
    lo = lax.bitcast_convert_type(w << 16, F32)
    hi = lax.bitcast_convert_type(w & jnp.uint32(HI_MASK), F32)
    return lo, hi


def _adaln_body(c_ref, w_ref, b_ref, o_ref):
    s = _silu(c_ref[...])
    o_ref[0] = _dot(s.astype(BF16), w_ref[0].astype(BF16)) + b_ref[0]


def _adaln(c_all, ada_w, ada_b):
    depth, d, n6 = ada_w.shape
    tn = _pick(n6, 1024)
    return pl.pallas_call(
        _adaln_body,
        grid=(depth, n6 // tn),
        in_specs=[
            pl.BlockSpec((MOD_ROWS, d), lambda l, j: (0, 0)),
            pl.BlockSpec((1, d, tn), lambda l, j: (l, 0, j)),
            pl.BlockSpec((1, 1, tn), lambda l, j: (l, 0, j)),
        ],
        out_specs=pl.BlockSpec((1, MOD_ROWS, tn), lambda l, j: (l, 0, j)),
        out_shape=jax.ShapeDtypeStruct((depth, MOD_ROWS, n6), F32),
        compiler_params=_params(("arbitrary", "arbitrary")),
        name="adaln",
    )(c_all, ada_w, ada_b.reshape(depth, 1, n6))


def _mod_spec(d, row_fn, sec):
    return pl.BlockSpec((1, 1, d), lambda b, i: (row_fn(b), 0, sec))


def _rms(x, g):
    return x * lax.rsqrt(jnp.mean(x * x, axis=-1, keepdims=True) + EPS) * g


def _norm_mod_body(x_ref, g_ref, sh_ref, sc_ref, o_ref):
    y = _rms(x_ref[0], g_ref[...])
    o_ref[0] = (y * (1.0 + sc_ref[0]) + sh_ref[0]).astype(o_ref.dtype)


def _norm_mod(x, g, mod3, row_fn, sec_shift, sec_scale):
    bsz, t, d = x.shape
    tm = _pick(t, 512)
    return pl.pallas_call(
        _norm_mod_body,
        grid=(bsz, t // tm),
        in_specs=[
            pl.BlockSpec((1, tm, d), lambda b, i: (b, i, 0)),
            pl.BlockSpec((1, d), lambda b, i: (0, 0)),
            _mod_spec(d, row_fn, sec_shift),
            _mod_spec(d, row_fn, sec_scale),
        ],
        out_specs=pl.BlockSpec((1, tm, d), lambda b, i: (b, i, 0)),
        out_shape=jax.ShapeDtypeStruct((bsz, t, d), BF16),
        compiler_params=_params(("arbitrary", "arbitrary")),
        name="norm_mod",
    )(x, g.reshape(1, d), mod3, mod3)


def _mm_body(a_ref, w_ref, o_ref, wb_ref):
    @pl.when(pl.program_id(1) == 0)
    def _():
        wb_ref[...] = w_ref[...].astype(BF16)

    o_ref[...] = _dot(a_ref[...], wb_ref[...]).astype(o_ref.dtype)


def _matmul(a, w):
    m, k = a.shape
    n = w.shape[1]
    tm, tn = _pick(m, 512), _pick(n, 1024)
    return pl.pallas_call(
        _mm_body,
        grid=(n // tn, m // tm),
        in_specs=[
            pl.BlockSpec((tm, k), lambda j, i: (i, 0)),
            pl.BlockSpec((k, tn), lambda j, i: (0, j)),
        ],
        out_specs=pl.BlockSpec((tm, tn), lambda j, i: (i, j)),
        out_shape=jax.ShapeDtypeStruct((m, n), BF16),
        scratch_shapes=[pltpu.VMEM((k, tn), BF16)],
        compiler_params=_params(("arbitrary", "arbitrary")),
        name="matmul",
    )(a, w)


def _mm_glu_body(a_ref, wv_ref, wg_ref, o_ref, wvb_ref, wgb_ref):
    @pl.when(pl.program_id(1) == 0)
    def _():
        wvb_ref[...] = wv_ref[...].astype(BF16)
        wgb_ref[...] = wg_ref[...].astype(BF16)

    a = a_ref[...]
    val = _dot(a, wvb_ref[...])
    gate = _dot(a, wgb_ref[...])
    o_ref[...] = (val * jax.nn.sigmoid(gate)).astype(o_ref.dtype)


def _matmul_glu(a, w):
    m, k = a.shape
    n = w.shape[1] // 2
    tm, tn = _pick(m, 512), _pick(n, 512)
    nj = n // tn
    return pl.pallas_call(
        _mm_glu_body,
        grid=(nj, m // tm),
        in_specs=[
            pl.BlockSpec((tm, k), lambda j, i: (i, 0)),
            pl.BlockSpec((k, tn), lambda j, i: (0, j)),
            pl.BlockSpec((k, tn), lambda j, i: (0, j + nj)),
        ],
        out_specs=pl.BlockSpec((tm, tn), lambda j, i: (i, j)),
        out_shape=jax.ShapeDtypeStruct((m, n), BF16),
        scratch_shapes=[pltpu.VMEM((k, tn), BF16), pltpu.VMEM((k, tn), BF16)],
        compiler_params=_params(("arbitrary", "arbitrary")),
        name="matmul_glu",
    )(a, w, w)


def _scan_body(qc_ref, vc_ref, zc_ref, ql_ref, vl_ref, zl_ref, lb_ref, o_ref,
               st_ref, flag_ref, b_sc, q_sc, k_sc, v_sc, *, nc, n_sub, heads):
    chunk = SCAN_CHUNK
    direction = pl.program_id(1)
    step = pl.program_id(2)
    fwd = direction == 0

    @pl.when(step == 0)
    def _():
        st_ref[0] = jnp.zeros(st_ref.shape[1:], F32)

    @pl.when(step > 0)
    def _():
        st_ref[0] = st_ref[n_sub]

    lb = lb_ref[0]
    row = lax.broadcasted_iota(I32, (chunk, chunk), 0)
    col = lax.broadcasted_iota(I32, (chunk, chunk), 1)
    sign = 1 - 2 * direction
    tri = (row - col) * sign >= 0
    tri_b = jnp.where(tri, 1.0, 0.0).astype(BF16)
    t_idx = lax.broadcasted_iota(I32, (chunk, 1), 0)

    def chunk_rows(ci):
        cidx = jnp.where(fwd, ci, n_sub - 1 - ci)
        return pl.ds(pl.multiple_of(cidx * chunk, chunk), chunk)

    def decay(z_ref, rows, sl):
        z = z_ref[0, rows, sl].astype(F32)
        f = lb[:, sl] + (1.0 - lb[:, sl]) * jax.nn.sigmoid(z)
        lf = jnp.log(f)
        hi = lf.astype(BF16)
        lo = (lf - hi.astype(F32)).astype(BF16)
        b = _dot(tri_b, hi) + _dot(tri_b, lo)
        tot = jnp.where(fwd, b[chunk - 1:chunk, :], b[0:1, :])
        return b, tot, 1.0 - f

    def fast_chunk(q_ref, v_ref, z_ref, ci, emit):
        rows = chunk_rows(ci)
        b, tot, kk = decay(z_ref, rows, slice(None))
        ke = kk * jnp.exp(tot - b)
        ke_b = ke.astype(BF16)
        etot = jnp.exp(tot)
        v = v_ref[0, rows, :]
        head_sl = [slice(h * HEAD_DIM, (h + 1) * HEAD_DIM) for h in range(heads)]
        st = [st_ref[ci, h] for h in range(heads)]
        kv = [_dot_tn(v[:, sl], ke_b[:, sl]) for sl in head_sl]
        if emit:
            qe = _silu(q_ref[0, rows, :].astype(F32)) * jnp.exp(b)
            qe_b = qe.astype(BF16)
            g = jnp.exp(-0.5 * tot)
            qh = (qe * g).astype(BF16)
            kh = (ke * g).astype(BF16)
            flag_ref[ci] = (jnp.min(tot) < SAFE_CHUNK_LOG_DECAY).astype(I32)
            sc = [_dot_nt(qh[:, sl], kh[:, sl]) for sl in head_sl]
            o_inter = [_dot_nt(qe_b[:, sl], st[h].astype(BF16)) for h, sl in enumerate(head_sl)]
            p = [jnp.where(tri, s, 0.0).astype(BF16) for s in sc]
            o_intra = [_dot(p[h], v[:, sl]) for h, sl in enumerate(head_sl)]
            for h, sl in enumerate(head_sl):
                o_ref[0, 0, rows, sl] = (o_inter[h] + o_intra[h]).astype(o_ref.dtype)
        for h, sl in enumerate(head_sl):
            st_ref[ci + 1, h] = st[h] * etot[:, sl] + kv[h]

    def exact_chunk(q_ref, v_ref, z_ref, ci):
        rows = chunk_rows(ci)
        b, _, kk = decay(z_ref, rows, slice(None))
        b_sc[...] = b
        q_sc[...] = _silu(q_ref[0, rows, :].astype(F32))
        k_sc[...] = kk
        v_sc[...] = v_ref[0, rows, :].astype(F32)

        def head_loop(h, c2):
            lanes = pl.ds(pl.multiple_of(h * HEAD_DIM, HEAD_DIM), HEAD_DIM)
            b_h = b_sc[:, lanes]
            q_h = q_sc[:, lanes]
            k_h = k_sc[:, lanes]
            v_h = v_sc[:, lanes]
            o_inter = _dot_nt((q_h * jnp.exp(b_h)).astype(BF16), st_ref[ci, h].astype(BF16))

            def key_loop(s, acc):
                pick = t_idx == s
                b_s = jnp.sum(jnp.where(pick, b_h, 0.0), axis=0, keepdims=True)
                k_s = jnp.sum(jnp.where(pick, k_h, 0.0), axis=0, keepdims=True)
                v_s = jnp.sum(jnp.where(pick, v_h, 0.0), axis=0, keepdims=True)
                valid = (t_idx - s) * sign >= 0
                dec = jnp.where(valid, jnp.exp(jnp.where(valid, b_h - b_s, 0.0)), 0.0)
                w = jnp.sum(q_h * dec * k_s, axis=1, keepdims=True)
                return acc + w * v_s

            o_intra = lax.fori_loop(0, chunk, key_loop, jnp.zeros((chunk, HEAD_DIM), F32))
            o_ref[0, 0, rows, lanes] = (o_inter + o_intra).astype(o_ref.dtype)
            return c2

        lax.fori_loop(0, heads, head_loop, 0)

    @pl.when(step < nc)
    def _():
        for ci in range(n_sub):
            fast_chunk(qc_ref, vc_ref, zc_ref, ci, False)

    @pl.when(step >= nc)
    def _():
        for ci in range(n_sub):
            fast_chunk(ql_ref, vl_ref, zl_ref, ci, True)

        def redo(ci, c):
            @pl.when(flag_ref[ci] == 1)
            def _():
                exact_chunk(ql_ref, vl_ref, zl_ref, ci)
            return c

        lax.fori_loop(0, n_sub, redo, 0)


def _scan(p_ctx, p_lat, lb):
    bsz, t_lat, d5 = p_lat.shape
    t_ctx = p_ctx.shape[1]
    d = d5 // 5
    heads = d // HEAD_DIM
    tb = _pick(t_ctx, 256)
    assert t_lat % tb == 0 and tb % SCAN_CHUNK == 0
    nc, nl = t_ctx // tb, t_lat // tb
    n_sub = tb // SCAN_CHUNK

    def ctx_blk(dr, s):
        sc = jnp.minimum(s, nc - 1)
        return jnp.where(dr == 0, sc, nc - 1 - sc)

    def lat_blk(dr, s):
        sl = jnp.maximum(s - nc, 0)
        return jnp.where(dr == 0, sl, nl - 1 - sl)

    def cspec(sec_fn):
        return pl.BlockSpec((1, tb, d), lambda b, dr, s: (b, ctx_blk(dr, s), sec_fn(dr)))

    def lspec(sec_fn):
        return pl.BlockSpec((1, tb, d), lambda b, dr, s: (b, lat_blk(dr, s), sec_fn(dr)))

    q_sec = lambda dr: 0
    v_sec = lambda dr: 1
    z_sec = lambda dr: 2 + dr
    return pl.pallas_call(
        functools.partial(_scan_body, nc=nc, n_sub=n_sub, heads=heads),
        grid=(bsz, 2, nc + nl),
        in_specs=[cspec(q_sec), cspec(v_sec), cspec(z_sec),
                  lspec(q_sec), lspec(v_sec), lspec(z_sec),
                  pl.BlockSpec((1, 1, d), lambda b, dr, s: (dr, 0, 0))],
        out_specs=pl.BlockSpec((1, 1, tb, d), lambda b, dr, s: (dr, b, lat_blk(dr, s), 0)),
        out_shape=jax.ShapeDtypeStruct((2, bsz, t_lat, d), BF16),
        scratch_shapes=[pltpu.VMEM((n_sub + 1, heads, HEAD_DIM, HEAD_DIM), F32),
                        pltpu.SMEM((n_sub,), I32),
                        pltpu.VMEM((SCAN_CHUNK, d), F32),
                        pltpu.VMEM((SCAN_CHUNK, d), F32),
                        pltpu.VMEM((SCAN_CHUNK, d), F32),
                        pltpu.VMEM((SCAN_CHUNK, d), F32)],
        compiler_params=_params(("arbitrary", "arbitrary", "arbitrary")),
        name="hgrn_scan",
    )(p_ctx, p_ctx, p_ctx, p_lat, p_lat, p_lat, lb.reshape(2, 1, d))


def _hgrn_readout_body(of_ref, ob_ref, g_ref, on_ref, w_ref, h_ref, gt_ref, o_ref,
                       wb_ref, a_ref, *, heads):
    @pl.when((pl.program_id(0) == 0) & (pl.program_id(1) == 0))
    def _():
        wb_ref[...] = w_ref[...].astype(BF16)

    on = on_ref[...]
    for h in range(heads):
        sl = slice(h * HEAD_DIM, (h + 1) * HEAD_DIM)
        o = of_ref[0, 0, :, sl].astype(F32) + ob_ref[0, 0, :, sl].astype(F32)
        o = _rms(o, on)
        a_ref[:, sl] = (o * _silu(g_ref[0, :, sl].astype(F32))).astype(BF16)
    y = _dot(a_ref[...], wb_ref[...])
    o_ref[0] = h_ref[0] + gt_ref[0] * y


def _hgrn_readout(o2, p_lat, onorm_g, w_out, h, mod3, row_fn, sec_gate):
    bsz, t, d = h.shape
    heads = d // HEAD_DIM
    tm = _pick(t, 256)
    return pl.pallas_call(
        functools.partial(_hgrn_readout_body, heads=heads),
        grid=(bsz, t // tm),
        in_specs=[
            pl.BlockSpec((1, 1, tm, d), lambda b, i: (0, b, i, 0)),
            pl.BlockSpec((1, 1, tm, d), lambda b, i: (1, b, i, 0)),
            pl.BlockSpec((1, tm, d), lambda b, i: (b, i, 4)),
            pl.BlockSpec((1, HEAD_DIM), lambda b, i: (0, 0)),
            pl.BlockSpec((d, d), lambda b, i: (0, 0), pipeline_mode=pl.Buffered(1)),
            pl.BlockSpec((1, tm, d), lambda b, i: (b, i, 0)),
            _mod_spec(d, row_fn, sec_gate),
        ],
        out_specs=pl.BlockSpec((1, tm, d), lambda b, i: (b, i, 0)),
        out_shape=jax.ShapeDtypeStruct((bsz, t, d), F32),
        scratch_shapes=[pltpu.VMEM((d, d), BF16), pltpu.VMEM((tm, d), BF16)],
        compiler_params=_params(("arbitrary", "arbitrary")),
        name="hgrn_readout",
    )(o2, o2, p_lat, onorm_g.reshape(1, HEAD_DIM), w_out, h, mod3)


def _conv_out_body(v_ref, lg_ref, lbias_ref, w_ref, h_ref, gt_ref, o_ref, wb_ref):
    @pl.when((pl.program_id(0) == 0) & (pl.program_id(1) == 0))
    def _():
        wb_ref[...] = w_ref[...].astype(BF16)

    x = v_ref[0].astype(F32)
    mu = jnp.mean(x, axis=-1, keepdims=True)
    xc = x - mu
    y = xc * lax.rsqrt(jnp.mean(xc * xc, axis=-1, keepdims=True) + EPS)
    y = _silu(y * lg_ref[...] + lbias_ref[...])
    o_ref[0] = h_ref[0] + gt_ref[0] * _dot(y.astype(BF16), wb_ref[...])


def _conv_out(v, ln_g, ln_b, w, h, mod3, row_fn, sec_gate):
    bsz, t, d = h.shape
    tm = _pick(t, 256)
    return pl.pallas_call(
        _conv_out_body,
        grid=(bsz, t // tm),
        in_specs=[
            pl.BlockSpec((1, tm, d), lambda b, i: (b, i, 0)),
            pl.BlockSpec((1, d), lambda b, i: (0, 0)),
            pl.BlockSpec((1, d), lambda b, i: (0, 0)),
            pl.BlockSpec((d, d), lambda b, i: (0, 0), pipeline_mode=pl.Buffered(1)),
            pl.BlockSpec((1, tm, d), lambda b, i: (b, i, 0)),
            _mod_spec(d, row_fn, sec_gate),
        ],
        out_specs=pl.BlockSpec((1, tm, d), lambda b, i: (b, i, 0)),
        out_shape=jax.ShapeDtypeStruct((bsz, t, d), F32),
        scratch_shapes=[pltpu.VMEM((d, d), BF16)],
        compiler_params=_params(("arbitrary", "arbitrary")),
        name="conv_out",
    )(v, ln_g.reshape(1, d), ln_b.reshape(1, d), w, h, mod3)


def _dwconv_body(u_ref, w_ref, b_ref, o_ref, pad_ref, *, rows, cb, half_blocks):
    half = CONV_WIDTH // 2
    j = pl.program_id(1)
    n_lane = cb // LANES
    bias = b_ref[...]

    def taps(read, r):
        for lb in range(n_lane):
            ln = slice(lb * LANES, (lb + 1) * LANES)
            acc = jnp.zeros((GRID_W, LANES), F32)
            for k in range(CONV_WIDTH):
                acc = acc + w_ref[k:k + 1, ln] * read(r, k, ln)
            o_ref[0, pl.ds(pl.multiple_of(r * GRID_W, GRID_W), GRID_W), ln] = (
                acc + bias[:, ln]).astype(o_ref.dtype)

    @pl.when(j < half_blocks)
    def _():
        zeros = jnp.zeros((rows, 16, cb), F32)
        pad_ref[0:rows, 0:16, :] = zeros
        pad_ref[0:rows, 16 + GRID_W:32 + GRID_W, :] = zeros
        pad_ref[0:rows, 16:16 + GRID_W, :] = u_ref[0].astype(F32).reshape(rows, GRID_W, cb)

        def row_loop(r, c):
            taps(lambda r_, k, ln: pad_ref[r_, 16 - half + k:16 - half + k + GRID_W, ln], r)
            return c

        lax.fori_loop(0, rows, row_loop, 0)

    @pl.when(j >= half_blocks)
    def _():
        zeros = jnp.zeros((half, GRID_W, cb), F32)
        pad_ref[0:half, 0:GRID_W, :] = zeros
        pad_ref[half + rows:2 * half + rows, 0:GRID_W, :] = zeros
        pad_ref[half:half + rows, 0:GRID_W, :] = u_ref[0].astype(F32).reshape(rows, GRID_W, cb)

        def row_loop(r, c):
            taps(lambda r_, k, ln: pad_ref[r_ + k, 0:GRID_W, ln], r)
            return c

        lax.fori_loop(0, rows, row_loop, 0)


def _dwconv(u, w_dw, b_dw):
    bsz, t, d = u.shape
    rows = t // GRID_W
    cb = _pick(d // 2, 256)
    half_blocks = (d // 2) // cb
    return pl.pallas_call(
        functools.partial(_dwconv_body, rows=rows, cb=cb, half_blocks=half_blocks),
        grid=(bsz, d // cb),
        in_specs=[
            pl.BlockSpec((1, t, cb), lambda b, j: (b, 0, j)),
            pl.BlockSpec((CONV_WIDTH, cb), lambda b, j: (0, j)),
            pl.BlockSpec((1, cb), lambda b, j: (0, j)),
        ],
        out_specs=pl.BlockSpec((1, t, cb), lambda b, j: (b, 0, j)),
        out_shape=jax.ShapeDtypeStruct((bsz, t, d), BF16),
        scratch_shapes=[pltpu.VMEM((rows + 2 * (CONV_WIDTH // 2) + 2, GRID_W + 32, cb), F32)],
        compiler_params=_params(("arbitrary", "arbitrary")),
        name="dwconv",
    )(u, w_dw, b_dw.reshape(1, d))


def _router_body(h_ref, g_ref, sh_ref, sc_ref, wr_ref, br_ref,
                 bp_ref, ri_ref, rw_ref, cnt_ref, carry_ref):
    tm = h_ref.shape[1]

    @pl.when((pl.program_id(0) == 0) & (pl.program_id(1) == 0))
    def _():
        carry_ref[...] = jnp.zeros_like(carry_ref)

    bmod = _rms(h_ref[0], g_ref[...]) * (1.0 + sc_ref[0]) + sh_ref[0]
    bp_ref[...] = _pack_bf16_pairs(bmod)

    b_hi = bmod.astype(BF16)
    b_lo = (bmod - b_hi.astype(F32)).astype(BF16)
    w = wr_ref[...]
    w_hi = w.astype(BF16)
    w_lo = (w - w_hi.astype(F32)).astype(BF16)
    logits = _dot(b_hi, w_hi) + _dot(b_hi, w_lo) + _dot(b_lo, w_hi) + br_ref[...]

    lane_i = lax.broadcasted_iota(I32, (tm, LANES), 1)
    lane = lane_i.astype(F32)
    neg = jnp.float32(-jnp.inf)
    big = jnp.float32(1 << 20)
    is_grp = (lane_i >= MOE_EXPERTS) & (lane_i < MOE_EXPERTS + MOE_GROUPS)
    m1 = jnp.max(jnp.where(is_grp, logits, neg), axis=1, keepdims=True)
    grp = jnp.min(jnp.where(is_grp & (logits == m1), lane - MOE_EXPERTS, big), axis=1, keepdims=True)
    pg = 1.0 / jnp.sum(jnp.where(is_grp, jnp.exp(logits - m1), 0.0), axis=1, keepdims=True)

    sel = (lane >= grp * MOE_PER_GROUP) & (lane < (grp + 1) * MOE_PER_GROUP)
    m2 = jnp.max(jnp.where(sel, logits, neg), axis=1, keepdims=True)
    ea = jnp.min(jnp.where(sel & (logits == m2), lane, big), axis=1, keepdims=True)
    sel2 = sel & (lane != ea)
    m3 = jnp.max(jnp.where(sel2, logits, neg), axis=1, keepdims=True)
    eb = jnp.min(jnp.where(sel2 & (logits == m3), lane, big), axis=1, keepdims=True)
    tr = jnp.exp(m3 - m2)
    wa = pg / (1.0 + tr)
    wb = wa * tr

    hit_a = lane == ea
    hit_b = lane == eb
    onehot = jnp.where(hit_a | hit_b, 1.0, 0.0)
    r_i = lax.broadcasted_iota(I32, (tm, tm), 0)
    c_i = lax.broadcasted_iota(I32, (tm, tm), 1)
    lower = jnp.where(r_i > c_i, 1.0, 0.0).astype(BF16)
    before = _dot(lower, onehot.astype(BF16)) + carry_ref[...]
    rank_a = jnp.sum(jnp.where(hit_a, before, 0.0), axis=1, keepdims=True).astype(I32)
    rank_b = jnp.sum(jnp.where(hit_b, before, 0.0), axis=1, keepdims=True).astype(I32)
    carry = carry_ref[...] + jnp.sum(onehot, axis=0, keepdims=True)
    carry_ref[...] = carry
    cnt_ref[...] = carry

    ri_ref[...] = jnp.where(lane_i == 0, ea.astype(I32), jnp.where(lane_i == 1, eb.astype(I32),
                            jnp.where(lane_i == 2, rank_a, jnp.where(lane_i == 3, rank_b, 0))))
    rw_ref[...] = jnp.where(lane_i == 0, wa, jnp.where(lane_i == 1, wb, 0.0))


def _router(h, g, mod3, row_fn, sec_shift, sec_scale, w_r, b_r):
    bsz, t, d = h.shape
    tm = _pick(t, 256)
    nt = t // tm
    n = bsz * t
    return pl.pallas_call(
        _router_body,
        grid=(bsz, nt),
        in_specs=[
            pl.BlockSpec((1, tm, d), lambda b, i: (b, i, 0)),
            pl.BlockSpec((1, d), lambda b, i: (0, 0)),
            _mod_spec(d, row_fn, sec_shift),
            _mod_spec(d, row_fn, sec_scale),
            pl.BlockSpec((d, LANES), lambda b, i: (0, 0)),
            pl.BlockSpec((1, LANES), lambda b, i: (0, 0)),
        ],
        out_specs=[
            pl.BlockSpec((tm, d // 2), lambda b, i: (b * nt + i, 0)),
            pl.BlockSpec((tm, LANES), lambda b, i: (b * nt + i, 0)),
            pl.BlockSpec((tm, LANES), lambda b, i: (b * nt + i, 0)),
            pl.BlockSpec((1, LANES), lambda b, i: (0, 0)),
        ],
        out_shape=[
            jax.ShapeDtypeStruct((n, d // 2), U32),
            jax.ShapeDtypeStruct((n, LANES), I32),
            jax.ShapeDtypeStruct((n, LANES), F32),
            jax.ShapeDtypeStruct((1, LANES), F32),
        ],
        scratch_shapes=[pltpu.VMEM((1, LANES), F32)],
        compiler_params=_params(("arbitrary", "arbitrary")),
        name="router",
    )(h, g.reshape(1, d), mod3, mod3, w_r, b_r)


def _dispatch_body(pos_ref, bp_ref, xs_ref, sem, *, tm):
    base = pl.program_id(0) * tm

    def row_copy(r, p):
        return pltpu.make_async_copy(bp_ref.at[pl.ds(r, 1), :], xs_ref.at[pl.ds(p, 1), :], sem)

    def issue(r, c):
        for k in range(2):
            row_copy(r, pos_ref[2 * (base + r) + k]).start()
        return c

    lax.fori_loop(0, tm, issue, 0)

    def drain(r, c):
        for k in range(2):
            row_copy(r, pos_ref[2 * (base + r) + k]).wait()
        return c

    lax.fori_loop(0, tm, drain, 0)


def _dispatch(pos, bp, n_slots):
    n, dh = bp.shape
    tm = _pick(n, 512)
    return pl.pallas_call(
        functools.partial(_dispatch_body, tm=tm),
        grid_spec=pltpu.PrefetchScalarGridSpec(
            num_scalar_prefetch=1,
            grid=(n // tm,),
            in_specs=[pl.BlockSpec((tm, dh), lambda i, pos_ref: (i, 0))],
            out_specs=pl.BlockSpec(memory_space=pl.ANY),
            scratch_shapes=[pltpu.SemaphoreType.DMA],
        ),
        out_shape=jax.ShapeDtypeStruct((n_slots, dh), U32),
        compiler_params=_params(("arbitrary",)),
        name="moe_dispatch",
    )(pos, bp)


def _expert_body(meta_ref, x_ref, wg_ref, wu_ref, wd_ref, y_ref, wgb_ref, wub_ref, wdb_ref):
    i = pl.program_id(0)
    dh = x_ref.shape[1]

    @pl.when(i < meta_ref[3, 0])
    def _():
        @pl.when(meta_ref[2, i] == 1)
        def _():
            wgb_ref[...] = wg_ref[0, 0].astype(BF16)
            wub_ref[...] = wu_ref[0, 0].astype(BF16)
            wdb_ref[...] = wd_ref[0, 0].astype(BF16)

        rowid = lax.broadcasted_iota(I32, (MOE_BLOCK, 1), 0)
        xw = jnp.where(rowid < meta_ref[1, i], x_ref[...], jnp.uint32(0))
        x_lo, x_hi = _unpack_bf16_pairs(xw)
        x_lo = x_lo.astype(BF16)
        x_hi = x_hi.astype(BF16)
        gate = _dot(x_lo, wgb_ref[0:dh, :]) + _dot(x_hi, wgb_ref[dh:2 * dh, :])
        up = _dot(x_lo, wub_ref[0:dh, :]) + _dot(x_hi, wub_ref[dh:2 * dh, :])
        mid = (_silu(gate) * up).astype(BF16)
        y_ref[...] = _pack_bf16_pairs(_dot(mid, wdb_ref[...]))


def _experts(meta, xs, w_gate, w_up, w_down, layer):
    n_slots, dh = xs.shape
    _, _, d, hid = w_gate.shape
    n_blocks = n_slots // MOE_BLOCK

    def blk(i, m):
        return jnp.minimum(i, m[3, 0] - 1)

    def wspec(shape):
        return pl.BlockSpec((1, 1) + shape, lambda i, m: (layer, m[0, blk(i, m)], 0, 0))

    return pl.pallas_call(
        _expert_body,
        grid_spec=pltpu.PrefetchScalarGridSpec(
            num_scalar_prefetch=1,
            grid=(n_blocks,),
            in_specs=[
                pl.BlockSpec((MOE_BLOCK, dh), lambda i, m: (blk(i, m), 0)),
                wspec((d, hid)), wspec((d, hid)), wspec((hid, d)),
            ],
            out_specs=pl.BlockSpec((MOE_BLOCK, dh), lambda i, m: (blk(i, m), 0)),
            scratch_shapes=[pltpu.VMEM((d, hid), BF16), pltpu.VMEM((d, hid), BF16),
                            pltpu.VMEM((hid, d), BF16)],
        ),
        out_shape=jax.ShapeDtypeStruct((n_slots, dh), U32),
        compiler_params=_params(("arbitrary",)),
        name="moe_experts",
    )(meta, xs, w_gate, w_up, w_down)


def _combine_body(pos_ref, h_ref, rw_ref, gt_ref, fg_ref, ys_ref, o_ref, ya_ref, yb_ref, sem,
                  *, tm, nt, final):
    base = (pl.program_id(0) * nt + pl.program_id(1)) * tm
    dh = ya_ref.shape[1]

    def row_copy(r, k):
        dst = ya_ref if k == 0 else yb_ref
        p = pos_ref[2 * (base + r) + k]
        return pltpu.make_async_copy(ys_ref.at[pl.ds(p, 1), :], dst.at[pl.ds(r, 1), :], sem)

    def issue(r, c):
        for k in range(2):
            row_copy(r, k).start()
        return c

    lax.fori_loop(0, tm, issue, 0)

    def drain(r, c):
        for k in range(2):
            row_copy(r, k).wait()
        return c

    lax.fori_loop(0, tm, drain, 0)

    wa = rw_ref[:, 0:1]
    wb = rw_ref[:, 1:2]
    a_lo, a_hi = _unpack_bf16_pairs(ya_ref[...])
    b_lo, b_hi = _unpack_bf16_pairs(yb_ref[...])
    gt = gt_ref[0]
    lo = h_ref[0, :, 0:dh] + gt[:, 0:dh] * (wa * a_lo + wb * b_lo)
    hi = h_ref[0, :, dh:2 * dh] + gt[:, dh:2 * dh] * (wa * a_hi + wb * b_hi)
    if final:
        ms = (jnp.sum(lo * lo, axis=-1, keepdims=True)
              + jnp.sum(hi * hi, axis=-1, keepdims=True)) / (2 * dh)
        inv = lax.rsqrt(ms + EPS)
        lo = lo * inv * fg_ref[:, 0:dh]
        hi = hi * inv * fg_ref[:, dh:2 * dh]
    o_ref[0, :, 0:dh] = lo
    o_ref[0, :, dh:2 * dh] = hi


def _combine(pos, h, rw, mod3, row_fn, sec_gate, final_g, ys, final):
    bsz, t, d = h.shape
    tm = _pick(t, 256)
    nt = t // tm
    return pl.pallas_call(
        functools.partial(_combine_body, tm=tm, nt=nt, final=final),
        grid_spec=pltpu.PrefetchScalarGridSpec(
            num_scalar_prefetch=1,
            grid=(bsz, nt),
            in_specs=[
                pl.BlockSpec((1, tm, d), lambda b, i, p: (b, i, 0)),
                pl.BlockSpec((tm, LANES), lambda b, i, p: (b * nt + i, 0)),
                pl.BlockSpec((1, 1, d), lambda b, i, p: (row_fn(b), 0, sec_gate)),
                pl.BlockSpec((1, d), lambda b, i, p: (0, 0)),
                pl.BlockSpec(memory_space=pl.ANY),
            ],
            out_specs=pl.BlockSpec((1, tm, d), lambda b, i, p: (b, i, 0)),
            scratch_shapes=[pltpu.VMEM((tm, d // 2), U32), pltpu.VMEM((tm, d // 2), U32),
                            pltpu.SemaphoreType.DMA],
        ),
        out_shape=jax.ShapeDtypeStruct((bsz, t, d), F32),
        compiler_params=_params(("arbitrary", "arbitrary")),
        name="moe_combine",
    )(pos, h, rw, mod3, final_g.reshape(1, d), ys)


def _hier_moe(h, g, mod3, row_fn, w_r1, b_r1, w_r2, b_r2, w_gate, w_up, w_down, layer, final_g, final):
    bsz, t, d = h.shape
    n = bsz * t
    w_r = jnp.concatenate([jnp.transpose(w_r2, (1, 0, 2)).reshape(d, MOE_EXPERTS), w_r1], axis=1)
    w_r = jnp.pad(w_r, ((0, 0), (0, LANES - w_r.shape[1])))
    b_r = jnp.pad(jnp.concatenate([b_r2.reshape(-1), b_r1]), (0, LANES - MOE_EXPERTS - MOE_GROUPS))
    bp, ri, rw, cnt = _router(h, g, mod3, row_fn, 3, 4, w_r, b_r.reshape(1, LANES))

    counts = cnt[0, :MOE_EXPERTS].astype(I32)
    padded = (counts + MOE_BLOCK - 1) // MOE_BLOCK * MOE_BLOCK
    pad_end = jnp.cumsum(padded)
    pad_start = pad_end - padded
    n_blocks = -(-(n * 2) // MOE_BLOCK) + MOE_EXPERTS
    blk_start = jnp.arange(n_blocks, dtype=I32) * MOE_BLOCK
    blk_e = jnp.minimum(jnp.searchsorted(pad_end, blk_start, side="right"), MOE_EXPERTS - 1).astype(I32)
    blk_valid = jnp.clip(pad_start[blk_e] + counts[blk_e] - blk_start, 0, MOE_BLOCK)
    blk_first = (blk_start == pad_start[blk_e]).astype(I32)
    n_used = jnp.full((n_blocks,), pad_end[-1] // MOE_BLOCK, I32)
    meta = jnp.stack([blk_e, blk_valid, blk_first, n_used])
    hit = ri[:, 0:2, None] == jnp.arange(MOE_EXPERTS, dtype=I32)
    pos = (jnp.sum(jnp.where(hit, pad_start, 0), axis=-1) + ri[:, 2:4]).reshape(-1)

    xs = _dispatch(pos, bp, n_blocks * MOE_BLOCK)
    ys = _experts(meta, xs, w_gate, w_up, w_down, layer)
    return _combine(pos, h, rw, mod3, row_fn, 5, final_g, ys, final)


def kernel(x, c, ctx, c_ctx, ada_w, ada_b, norm1_g, norm2_g, hgrn_w_in, hgrn_lb, hgrn_onorm_g,
           hgrn_w_out, conv_w_pw1, conv_w_dw, conv_b_dw, conv_ln_g, conv_ln_b, conv_w_pw2,
           moe_w_r1, moe_b_r1, moe_w_r2, moe_b_r2, moe_w_gate, moe_w_up, moe_w_down, final_g):
    bsz, t, d = x.shape
    t_ctx = ctx.shape[1]
    depth = ada_w.shape[0]
    assert depth == 2 and bsz < MOD_ROWS and d % (2 * LANES) == 0 and t % GRID_W == 0

    c_all = jnp.concatenate([c, c_ctx[None, :], jnp.zeros((MOD_ROWS - bsz - 1, d), F32)], axis=0)
    mod = _adaln(c_all, ada_w, ada_b)
    mod3 = mod.reshape(depth * MOD_ROWS, 1, 6 * d)
    lat_row = lambda l: (lambda b: l * MOD_ROWS + b)
    ctx_row = lambda l: (lambda b: l * MOD_ROWS + bsz)
    lb_all = jnp.cumsum(jax.nn.softmax(hgrn_lb.astype(F32), axis=1), axis=1)

    def moe(h, l, final):
        return _hier_moe(h, norm2_g[l], mod3, lat_row(l), moe_w_r1[l], moe_b_r1[l], moe_w_r2[l],
                         moe_b_r2[l], moe_w_gate, moe_w_up, moe_w_down, l, final_g, final)

    a_lat = _norm_mod(x, norm1_g[0], mod3, lat_row(0), 0, 1)
    a_ctx = _norm_mod(ctx, norm1_g[0], mod3, ctx_row(0), 0, 1)
    p_lat = _matmul(a_lat.reshape(bsz * t, d), hgrn_w_in[0]).reshape(bsz, t, 5 * d)
    p_ctx = _matmul(a_ctx.reshape(bsz * t_ctx, d), hgrn_w_in[0]).reshape(bsz, t_ctx, 5 * d)
    o2 = _scan(p_ctx, p_lat, lb_all[:, 0])
    h = _hgrn_readout(o2, p_lat, hgrn_onorm_g[0], hgrn_w_out[0], x, mod3, lat_row(0), 2)
    h = moe(h, 0, False)

    a = _norm_mod(h, norm1_g[1], mod3, lat_row(1), 0, 1)
    u = _matmul_glu(a.reshape(bsz * t, d), conv_w_pw1[0]).reshape(bsz, t, d)
    v = _dwconv(u, conv_w_dw[0], conv_b_dw[0])
    h = _conv_out(v, conv_ln_g[0], conv_ln_b[0], conv_w_pw2[0], h, mod3, lat_row(1), 2)
    return moe(h, 1, True)
```

```python
import functools

import jax
import jax.numpy as jnp
from jax import lax
from jax.experimental import pallas as pl
from jax.experimental.pallas import tpu as pltpu

F32 = jnp.float32
BF16 = jnp.bfloat16
U32 = jnp.uint32
I32 = jnp.int32

EPS = 1e-6
LANES = 128
HEAD_DIM = 128
SCAN_CHUNK = 64
GRID_W = 64
CONV_WIDTH = 31
MOE_GROUPS = 8
MOE_PER_GROUP = 8
MOE_EXPERTS = MOE_GROUPS * MOE_PER_GROUP
MOE_BLOCK = 256
MOD_ROWS = 16
VMEM_LIMIT = 56 * 1024 * 1024
SAFE_CHUNK_LOG_DECAY = -80.0
HI_MASK = 0xFFFF0000


def _pick(n, pref):
    t = pref
    while n % t:
        t //= 2
    return t


def _params(sem):
    return pltpu.CompilerParams(dimension_semantics=sem, vmem_limit_bytes=VMEM_LIMIT)


def _silu(x):
    return x * jax.nn.sigmoid(x)


def _dot(a, b):
    return jnp.dot(a, b, preferred_element_type=F32)


def _dot_nt(a, b):
    return lax.dot_general(a, b, (((1,), (1,)), ((), ())), preferred_element_type=F32)


def _dot_tn(a, b):
    return lax.dot_general(a, b, (((0,), (0,)), ((), ())), preferred_element_type=F32)


def _pack_bf16_pairs(x):
    n = x.shape[1] // 2
    bits = lax.bitcast_convert_type(x.astype(BF16).astype(F32), U32)
    return (bits[:, :n] >> 16) | (bits[:, n:] & jnp.uint32(HI_MASK))


def _unpack_bf16_pairs(w):
    lo = lax.bitcast_convert_type(w << 16, F32)
    hi = lax.bitcast_convert_type(w & jnp.uint32(HI_MASK), F32)
    return lo, hi


def _store_token_tiles(ref, w, first=0, stride=None):
    m, n = w.shape[0], w.shape[1] // LANES
    stride = n if stride is None else stride
    for s in range(n):
        ref[pl.ds(first + s, m, stride=stride), :] = w[:, s * LANES:(s + 1) * LANES]


def _load_token_tiles(ref, m, n, first=0, stride=None):
    stride = n if stride is None else stride
    return jnp.concatenate([ref[pl.ds(first + s, m, stride=stride), :] for s in range(n)], axis=1)


def _adaln_body(c_ref, w_ref, b_ref, o_ref):
    s = _silu(c_ref[...])
    o_ref[0] = _dot(s.astype(BF16), w_ref[0].astype(BF16)) + b_ref[0]


def _adaln(c_all, ada_w, ada_b):
    depth, d, n6 = ada_w.shape
    tn = _pick(n6, 1024)
    return pl.pallas_call(
        _adaln_body,
        grid=(depth, n6 // tn),
        in_specs=[
            pl.BlockSpec((MOD_ROWS, d), lambda l, j: (0, 0)),
            pl.BlockSpec((1, d, tn), lambda l, j: (l, 0, j)),
            pl.BlockSpec((1, 1, tn), lambda l, j: (l, 0, j)),
        ],
        out_specs=pl.BlockSpec((1, MOD_ROWS, tn), lambda l, j: (l, 0, j)),
        out_shape=jax.ShapeDtypeStruct((depth, MOD_ROWS, n6), F32),
        compiler_params=_params(("arbitrary", "arbitrary")),
        name="adaln",
    )(c_all, ada_w, ada_b.reshape(depth, 1, n6))


def _mod_spec(d, row_fn, sec):
    return pl.BlockSpec((1, 1, d), lambda b, i: (row_fn(b), 0, sec))


def _rms(x, g):
    return x * lax.rsqrt(jnp.mean(x * x, axis=-1, keepdims=True) + EPS) * g


def _norm_mod_body(x_ref, g_ref, sh_ref, sc_ref, o_ref):
    y = _rms(x_ref[0], g_ref[...])
    o_ref[0] = (y * (1.0 + sc_ref[0]) + sh_ref[0]).astype(o_ref.dtype)


def _norm_mod(x, g, mod3, row_fn, sec_shift, sec_scale):
    bsz, t, d = x.shape
    tm = _pick(t, 512)
    return pl.pallas_call(
        _norm_mod_body,
        grid=(bsz, t // tm),
        in_specs=[
            pl.BlockSpec((1, tm, d), lambda b, i: (b, i, 0)),
            pl.BlockSpec((1, d), lambda b, i: (0, 0)),
            _mod_spec(d, row_fn, sec_shift),
            _mod_spec(d, row_fn, sec_scale),
        ],
        out_specs=pl.BlockSpec((1, tm, d), lambda b, i: (b, i, 0)),
        out_shape=jax.ShapeDtypeStruct((bsz, t, d), BF16),
        compiler_params=_params(("arbitrary", "arbitrary")),
        name="norm_mod",
    )(x, g.reshape(1, d), mod3, mod3)


def _mm_body(a_ref, w_ref, o_ref, wb_ref):
    @pl.when(pl.program_id(1) == 0)
    def _():
        wb_ref[...] = w_ref[...].astype(BF16)

    o_ref[...] = _dot(a_ref[...], wb_ref[...]).astype(o_ref.dtype)


def _matmul(a, w):
    m, k = a.shape
    n = w.shape[1]
    tm, tn = _pick(m, 512), _pick(n, 1024)
    return pl.pallas_call(
        _mm_body,
        grid=(n // tn, m // tm),
        in_specs=[
            pl.BlockSpec((tm, k), lambda j, i: (i, 0)),
            pl.BlockSpec((k, tn), lambda j, i: (0, j)),
        ],
        out_specs=pl.BlockSpec((tm, tn), lambda j, i: (i, j)),
        out_shape=jax.ShapeDtypeStruct((m, n), BF16),
        scratch_shapes=[pltpu.VMEM((k, tn), BF16)],
        compiler_params=_params(("arbitrary", "arbitrary")),
        name="matmul",
    )(a, w)


def _mm_glu_body(a_ref, wv_ref, wg_ref, o_ref, wvb_ref, wgb_ref):
    @pl.when(pl.program_id(1) == 0)
    def _():
        wvb_ref[...] = wv_ref[...].astype(BF16)
        wgb_ref[...] = wg_ref[...].astype(BF16)

    a = a_ref[...]
    val = _dot(a, wvb_ref[...])
    gate = _dot(a, wgb_ref[...])
    o_ref[...] = (val * jax.nn.sigmoid(gate)).astype(o_ref.dtype)


def _matmul_glu(a, w):
    m, k = a.shape
    n = w.shape[1] // 2
    tm, tn = _pick(m, 512), _pick(n, 512)
    nj = n // tn
    return pl.pallas_call(
        _mm_glu_body,
        grid=(nj, m // tm),
        in_specs=[
            pl.BlockSpec((tm, k), lambda j, i: (i, 0)),
            pl.BlockSpec((k, tn), lambda j, i: (0, j)),
            pl.BlockSpec((k, tn), lambda j, i: (0, j + nj)),
        ],
        out_specs=pl.BlockSpec((tm, tn), lambda j, i: (i, j)),
        out_shape=jax.ShapeDtypeStruct((m, n), BF16),
        scratch_shapes=[pltpu.VMEM((k, tn), BF16), pltpu.VMEM((k, tn), BF16)],
        compiler_params=_params(("arbitrary", "arbitrary")),
        name="matmul_glu",
    )(a, w, w)


def _scan_body(qc_ref, vc_ref, zc_ref, ql_ref, vl_ref, zl_ref, lb_ref, o_ref,
               st_ref, flag_ref, b_sc, q_sc, k_sc, v_sc, *, nc, n_sub, heads):
    chunk = SCAN_CHUNK
    direction = pl.program_id(1)
    step = pl.program_id(2)
    fwd = direction == 0

    @pl.when(step == 0)
    def _():
        st_ref[0] = jnp.zeros(st_ref.shape[1:], F32)

    @pl.when(step > 0)
    def _():
        st_ref[0] = st_ref[n_sub]

    lb = lb_ref[0]
    row = lax.broadcasted_iota(I32, (chunk, chunk), 0)
    col = lax.broadcasted_iota(I32, (chunk, chunk), 1)
    sign = 1 - 2 * direction
    tri = (row - col) * sign >= 0
    tri_b = jnp.where(tri, 1.0, 0.0).astype(BF16)
    t_idx = lax.broadcasted_iota(I32, (chunk, 1), 0)

    def chunk_rows(ci):
        cidx = jnp.where(fwd, ci, n_sub - 1 - ci)
        return pl.ds(pl.multiple_of(cidx * chunk, chunk), chunk)

    def decay(z_ref, rows, sl):
        z = z_ref[0, rows, sl].astype(F32)
        f = lb[:, sl] + (1.0 - lb[:, sl]) * jax.nn.sigmoid(z)
        lf = jnp.log(f)
        hi = lf.astype(BF16)
        lo = (lf - hi.astype(F32)).astype(BF16)
        b = _dot(tri_b, hi) + _dot(tri_b, lo)
        tot = jnp.where(fwd, b[chunk - 1:chunk, :], b[0:1, :])
        return b, tot, 1.0 - f

    def fast_chunk(q_ref, v_ref, z_ref, ci, emit):
        rows = chunk_rows(ci)
        b, tot, kk = decay(z_ref, rows, slice(None))
        ke = kk * jnp.exp(tot - b)
        ke_b = ke.astype(BF16)
        etot = jnp.exp(tot)
        v = v_ref[0, rows, :]
        head_sl = [slice(h * HEAD_DIM, (h + 1) * HEAD_DIM) for h in range(heads)]
        st = [st_ref[ci, h] for h in range(heads)]
        kv = [_dot_tn(v[:, sl], ke_b[:, sl]) for sl in head_sl]
        if emit:
            qe = _silu(q_ref[0, rows, :].astype(F32)) * jnp.exp(b)
            qe_b = qe.astype(BF16)
            g = jnp.exp(-0.5 * tot)
            qh = (qe * g).astype(BF16)
            kh = (ke * g).astype(BF16)
            flag_ref[ci] = (jnp.min(tot) < SAFE_CHUNK_LOG_DECAY).astype(I32)
            sc = [_dot_nt(qh[:, sl], kh[:, sl]) for sl in head_sl]
            o_inter = [_dot_nt(qe_b[:, sl], st[h].astype(BF16)) for h, sl in enumerate(head_sl)]
            p = [jnp.where(tri, s, 0.0).astype(BF16) for s in sc]
            o_intra = [_dot(p[h], v[:, sl]) for h, sl in enumerate(head_sl)]
            for h, sl in enumerate(head_sl):
                o_ref[0, 0, rows, sl] = (o_inter[h] + o_intra[h]).astype(o_ref.dtype)
        for h, sl in enumerate(head_sl):
            st_ref[ci + 1, h] = st[h] * etot[:, sl] + kv[h]

    def exact_chunk(q_ref, v_ref, z_ref, ci):
        rows = chunk_rows(ci)
        b, _, kk = decay(z_ref, rows, slice(None))
        b_sc[...] = b
        q_sc[...] = _silu(q_ref[0, rows, :].astype(F32))
        k_sc[...] = kk
        v_sc[...] = v_ref[0, rows, :].astype(F32)

        def head_loop(h, c2):
            lanes = pl.ds(pl.multiple_of(h * HEAD_DIM, HEAD_DIM), HEAD_DIM)
            b_h = b_sc[:, lanes]
            q_h = q_sc[:, lanes]
            k_h = k_sc[:, lanes]
            v_h = v_sc[:, lanes]
            o_inter = _dot_nt((q_h * jnp.exp(b_h)).astype(BF16), st_ref[ci, h].astype(BF16))

            def key_loop(s, acc):
                pick = t_idx == s
                b_s = jnp.sum(jnp.where(pick, b_h, 0.0), axis=0, keepdims=True)
                k_s = jnp.sum(jnp.where(pick, k_h, 0.0), axis=0, keepdims=True)
                v_s = jnp.sum(jnp.where(pick, v_h, 0.0), axis=0, keepdims=True)
                valid = (t_idx - s) * sign >= 0
                dec = jnp.where(valid, jnp.exp(jnp.where(valid, b_h - b_s, 0.0)), 0.0)
                w = jnp.sum(q_h * dec * k_s, axis=1, keepdims=True)
                return acc + w * v_s

            o_intra = lax.fori_loop(0, chunk, key_loop, jnp.zeros((chunk, HEAD_DIM), F32))
            o_ref[0, 0, rows, lanes] = (o_inter + o_intra).astype(o_ref.dtype)
            return c2

        lax.fori_loop(0, heads, head_loop, 0)

    @pl.when(step < nc)
    def _():
        for ci in range(n_sub):
            fast_chunk(qc_ref, vc_ref, zc_ref, ci, False)

    @pl.when(step >= nc)
    def _():
        for ci in range(n_sub):
            fast_chunk(ql_ref, vl_ref, zl_ref, ci, True)

        def redo(ci, c):
            @pl.when(flag_ref[ci] == 1)
            def _():
                exact_chunk(ql_ref, vl_ref, zl_ref, ci)
            return c

        lax.fori_loop(0, n_sub, redo, 0)


def _scan(p_ctx, p_lat, lb):
    bsz, t_lat, d5 = p_lat.shape
    t_ctx = p_ctx.shape[1]
    d = d5 // 5
    heads = d // HEAD_DIM
    tb = _pick(t_ctx, 256)
    assert t_lat % tb == 0 and tb % SCAN_CHUNK == 0
    nc, nl = t_ctx // tb, t_lat // tb
    n_sub = tb // SCAN_CHUNK

    def ctx_blk(dr, s):
        sc = jnp.minimum(s, nc - 1)
        return jnp.where(dr == 0, sc, nc - 1 - sc)

    def lat_blk(dr, s):
        sl = jnp.maximum(s - nc, 0)
        return jnp.where(dr == 0, sl, nl - 1 - sl)

    def cspec(sec_fn):
        return pl.BlockSpec((1, tb, d), lambda b, dr, s: (b, ctx_blk(dr, s), sec_fn(dr)))

    def lspec(sec_fn):
        return pl.BlockSpec((1, tb, d), lambda b, dr, s: (b, lat_blk(dr, s), sec_fn(dr)))

    q_sec = lambda dr: 0
    v_sec = lambda dr: 1
    z_sec = lambda dr: 2 + dr
    return pl.pallas_call(
        functools.partial(_scan_body, nc=nc, n_sub=n_sub, heads=heads),
        grid=(bsz, 2, nc + nl),
        in_specs=[cspec(q_sec), cspec(v_sec), cspec(z_sec),
                  lspec(q_sec), lspec(v_sec), lspec(z_sec),
                  pl.BlockSpec((1, 1, d), lambda b, dr, s: (dr, 0, 0))],
        out_specs=pl.BlockSpec((1, 1, tb, d), lambda b, dr, s: (dr, b, lat_blk(dr, s), 0)),
        out_shape=jax.ShapeDtypeStruct((2, bsz, t_lat, d), BF16),
        scratch_shapes=[pltpu.VMEM((n_sub + 1, heads, HEAD_DIM, HEAD_DIM), F32),
                        pltpu.SMEM((n_sub,), I32),
                        pltpu.VMEM((SCAN_CHUNK, d), F32),
                        pltpu.VMEM((SCAN_CHUNK, d), F32),
                        pltpu.VMEM((SCAN_CHUNK, d), F32),
                        pltpu.VMEM((SCAN_CHUNK, d), F32)],
        compiler_params=_params(("arbitrary", "arbitrary", "arbitrary")),
        name="hgrn_scan",
    )(p_ctx, p_ctx, p_ctx, p_lat, p_lat, p_lat, lb.reshape(2, 1, d))


def _hgrn_readout_body(of_ref, ob_ref, g_ref, on_ref, w_ref, h_ref, gt_ref, o_ref,
                       wb_ref, a_ref, *, heads):
    @pl.when((pl.program_id(0) == 0) & (pl.program_id(1) == 0))
    def _():
        wb_ref[...] = w_ref[...].astype(BF16)

    on = on_ref[...]
    for h in range(heads):
        sl = slice(h * HEAD_DIM, (h + 1) * HEAD_DIM)
        o = of_ref[0, 0, :, sl].astype(F32) + ob_ref[0, 0, :, sl].astype(F32)
        o = _rms(o, on)
        a_ref[:, sl] = (o * _silu(g_ref[0, :, sl].astype(F32))).astype(BF16)
    y = _dot(a_ref[...], wb_ref[...])
    o_ref[0] = h_ref[0] + gt_ref[0] * y


def _hgrn_readout(o2, p_lat, onorm_g, w_out, h, mod3, row_fn, sec_gate):
    bsz, t, d = h.shape
    heads = d // HEAD_DIM
    tm = _pick(t, 256)
    return pl.pallas_call(
        functools.partial(_hgrn_readout_body, heads=heads),
        grid=(bsz, t // tm),
        in_specs=[
            pl.BlockSpec((1, 1, tm, d), lambda b, i: (0, b, i, 0)),
            pl.BlockSpec((1, 1, tm, d), lambda b, i: (1, b, i, 0)),
            pl.BlockSpec((1, tm, d), lambda b, i: (b, i, 4)),
            pl.BlockSpec((1, HEAD_DIM), lambda b, i: (0, 0)),
            pl.BlockSpec((d, d), lambda b, i: (0, 0), pipeline_mode=pl.Buffered(1)),
            pl.BlockSpec((1, tm, d), lambda b, i: (b, i, 0)),
            _mod_spec(d, row_fn, sec_gate),
        ],
        out_specs=pl.BlockSpec((1, tm, d), lambda b, i: (b, i, 0)),
        out_shape=jax.ShapeDtypeStruct((bsz, t, d), F32),
        scratch_shapes=[pltpu.VMEM((d, d), BF16), pltpu.VMEM((tm, d), BF16)],
        compiler_params=_params(("arbitrary", "arbitrary")),
        name="hgrn_readout",
    )(o2, o2, p_lat, onorm_g.reshape(1, HEAD_DIM), w_out, h, mod3)


def _conv_out_body(v_ref, lg_ref, lbias_ref, w_ref, h_ref, gt_ref, o_ref, wb_ref):
    @pl.when((pl.program_id(0) == 0) & (pl.program_id(1) == 0))
    def _():
        wb_ref[...] = w_ref[...].astype(BF16)

    x = v_ref[0].astype(F32)
    mu = jnp.mean(x, axis=-1, keepdims=True)
    xc = x - mu
    y = xc * lax.rsqrt(jnp.mean(xc * xc, axis=-1, keepdims=True) + EPS)
    y = _silu(y * lg_ref[...] + lbias_ref[...])
    o_ref[0] = h_ref[0] + gt_ref[0] * _dot(y.astype(BF16), wb_ref[...])


def _conv_out(v, ln_g, ln_b, w, h, mod3, row_fn, sec_gate):
    bsz, t, d = h.shape
    tm = _pick(t, 256)
    return pl.pallas_call(
        _conv_out_body,
        grid=(bsz, t // tm),
        in_specs=[
            pl.BlockSpec((1, tm, d), lambda b, i: (b, i, 0)),
            pl.BlockSpec((1, d), lambda b, i: (0, 0)),
            pl.BlockSpec((1, d), lambda b, i: (0, 0)),
            pl.BlockSpec((d, d), lambda b, i: (0, 0), pipeline_mode=pl.Buffered(1)),
            pl.BlockSpec((1, tm, d), lambda b, i: (b, i, 0)),
            _mod_spec(d, row_fn, sec_gate),
        ],
        out_specs=pl.BlockSpec((1, tm, d), lambda b, i: (b, i, 0)),
        out_shape=jax.ShapeDtypeStruct((bsz, t, d), F32),
        scratch_shapes=[pltpu.VMEM((d, d), BF16)],
        compiler_params=_params(("arbitrary", "arbitrary")),
        name="conv_out",
    )(v, ln_g.reshape(1, d), ln_b.reshape(1, d), w, h, mod3)


def _dwconv_body(u_ref, w_ref, b_ref, o_ref, pad_ref, *, rows, cb, half_blocks):
    half = CONV_WIDTH // 2
    j = pl.program_id(1)
    n_lane = cb // LANES
    bias = b_ref[...]

    def taps(read, r):
        for lb in range(n_lane):
            ln = slice(lb * LANES, (lb + 1) * LANES)
            acc = jnp.zeros((GRID_W, LANES), F32)
            for k in range(CONV_WIDTH):
                acc = acc + w_ref[k:k + 1, ln] * read(r, k, ln)
            o_ref[0, pl.ds(pl.multiple_of(r * GRID_W, GRID_W), GRID_W), ln] = (
                acc + bias[:, ln]).astype(o_ref.dtype)

    @pl.when(j < half_blocks)
    def _():
        zeros = jnp.zeros((rows, 16, cb), F32)
        pad_ref[0:rows, 0:16, :] = zeros
        pad_ref[0:rows, 16 + GRID_W:32 + GRID_W, :] = zeros
        pad_ref[0:rows, 16:16 + GRID_W, :] = u_ref[0].astype(F32).reshape(rows, GRID_W, cb)

        def row_loop(r, c):
            taps(lambda r_, k, ln: pad_ref[r_, 16 - half + k:16 - half + k + GRID_W, ln], r)
            return c

        lax.fori_loop(0, rows, row_loop, 0)

    @pl.when(j >= half_blocks)
    def _():
        zeros = jnp.zeros((half, GRID_W, cb), F32)
        pad_ref[0:half, 0:GRID_W, :] = zeros
        pad_ref[half + rows:2 * half + rows, 0:GRID_W, :] = zeros
        pad_ref[half:half + rows, 0:GRID_W, :] = u_ref[0].astype(F32).reshape(rows, GRID_W, cb)

        def row_loop(r, c):
            taps(lambda r_, k, ln: pad_ref[r_ + k, 0:GRID_W, ln], r)
            return c

        lax.fori_loop(0, rows, row_loop, 0)


def _dwconv(u, w_dw, b_dw):
    bsz, t, d = u.shape
    rows = t // GRID_W
    cb = _pick(d // 2, 256)
    half_blocks = (d // 2) // cb
    return pl.pallas_call(
        functools.partial(_dwconv_body, rows=rows, cb=cb, half_blocks=half_blocks),
        grid=(bsz, d // cb),
        in_specs=[
            pl.BlockSpec((1, t, cb), lambda b, j: (b, 0, j)),
            pl.BlockSpec((CONV_WIDTH, cb), lambda b, j: (0, j)),
            pl.BlockSpec((1, cb), lambda b, j: (0, j)),
        ],
        out_specs=pl.BlockSpec((1, t, cb), lambda b, j: (b, 0, j)),
        out_shape=jax.ShapeDtypeStruct((bsz, t, d), BF16),
        scratch_shapes=[pltpu.VMEM((rows + 2 * (CONV_WIDTH // 2) + 2, GRID_W + 32, cb), F32)],
        compiler_params=_params(("arbitrary", "arbitrary")),
        name="dwconv",
    )(u, w_dw, b_dw.reshape(1, d))


def _router_body(h_ref, g_ref, sh_ref, sc_ref, wr_ref, br_ref,
                 bp_ref, ri_ref, rw_ref, cnt_ref, carry_ref):
    tm = h_ref.shape[1]

    @pl.when((pl.program_id(0) == 0) & (pl.program_id(1) == 0))
    def _():
        carry_ref[...] = jnp.zeros_like(carry_ref)

    bmod = _rms(h_ref[0], g_ref[...]) * (1.0 + sc_ref[0]) + sh_ref[0]
    _store_token_tiles(bp_ref, _pack_bf16_pairs(bmod))

    b_hi = bmod.astype(BF16)
    b_lo = (bmod - b_hi.astype(F32)).astype(BF16)
    w = wr_ref[...]
    w_hi = w.astype(BF16)
    w_lo = (w - w_hi.astype(F32)).astype(BF16)
    logits = _dot(b_hi, w_hi) + _dot(b_hi, w_lo) + _dot(b_lo, w_hi) + br_ref[...]

    lane_i = lax.broadcasted_iota(I32, (tm, LANES), 1)
    lane = lane_i.astype(F32)
    neg = jnp.float32(-jnp.inf)
    big = jnp.float32(1 << 20)
    is_grp = (lane_i >= MOE_EXPERTS) & (lane_i < MOE_EXPERTS + MOE_GROUPS)
    m1 = jnp.max(jnp.where(is_grp, logits, neg), axis=1, keepdims=True)
    grp = jnp.min(jnp.where(is_grp & (logits == m1), lane - MOE_EXPERTS, big), axis=1, keepdims=True)
    pg = 1.0 / jnp.sum(jnp.where(is_grp, jnp.exp(logits - m1), 0.0), axis=1, keepdims=True)

    sel = (lane >= grp * MOE_PER_GROUP) & (lane < (grp + 1) * MOE_PER_GROUP)
    m2 = jnp.max(jnp.where(sel, logits, neg), axis=1, keepdims=True)
    ea = jnp.min(jnp.where(sel & (logits == m2), lane, big), axis=1, keepdims=True)
    sel2 = sel & (lane != ea)
    m3 = jnp.max(jnp.where(sel2, logits, neg), axis=1, keepdims=True)
    eb = jnp.min(jnp.where(sel2 & (logits == m3), lane, big), axis=1, keepdims=True)
    tr = jnp.exp(m3 - m2)
    wa = pg / (1.0 + tr)
    wb = wa * tr

    hit_a = lane == ea
    hit_b = lane == eb
    onehot = jnp.where(hit_a | hit_b, 1.0, 0.0)
    r_i = lax.broadcasted_iota(I32, (tm, tm), 0)
    c_i = lax.broadcasted_iota(I32, (tm, tm), 1)
    lower = jnp.where(r_i > c_i, 1.0, 0.0).astype(BF16)
    before = _dot(lower, onehot.astype(BF16)) + carry_ref[...]
    rank_a = jnp.sum(jnp.where(hit_a, before, 0.0), axis=1, keepdims=True).astype(I32)
    rank_b = jnp.sum(jnp.where(hit_b, before, 0.0), axis=1, keepdims=True).astype(I32)
    carry = carry_ref[...] + jnp.sum(onehot, axis=0, keepdims=True)
    carry_ref[...] = carry
    cnt_ref[...] = carry

    ri_ref[...] = jnp.where(lane_i == 0, ea.astype(I32), jnp.where(lane_i == 1, eb.astype(I32),
                            jnp.where(lane_i == 2, rank_a, jnp.where(lane_i == 3, rank_b, 0))))
    rw_ref[...] = jnp.where(lane_i == 0, wa, jnp.where(lane_i == 1, wb, 0.0))


def _router(h, g, mod3, row_fn, sec_shift, sec_scale, w_r, b_r):
    bsz, t, d = h.shape
    tm = _pick(t, 256)
    nt = t // tm
    n = bsz * t
    rpt = d // 2 // LANES
    return pl.pallas_call(
        _router_body,
        grid=(bsz, nt),
        in_specs=[
            pl.BlockSpec((1, tm, d), lambda b, i: (b, i, 0)),
            pl.BlockSpec((1, d), lambda b, i: (0, 0)),
            _mod_spec(d, row_fn, sec_shift),
            _mod_spec(d, row_fn, sec_scale),
            pl.BlockSpec((d, LANES), lambda b, i: (0, 0)),
            pl.BlockSpec((1, LANES), lambda b, i: (0, 0)),
        ],
        out_specs=[
            pl.BlockSpec((tm * rpt, LANES), lambda b, i: (b * nt + i, 0)),
            pl.BlockSpec((tm, LANES), lambda b, i: (b * nt + i, 0)),
            pl.BlockSpec((tm, LANES), lambda b, i: (b * nt + i, 0)),
            pl.BlockSpec((1, LANES), lambda b, i: (0, 0)),
        ],
        out_shape=[
            jax.ShapeDtypeStruct((n * rpt, LANES), U32),
            jax.ShapeDtypeStruct((n, LANES), I32),
            jax.ShapeDtypeStruct((n, LANES), F32),
            jax.ShapeDtypeStruct((1, LANES), F32),
        ],
        scratch_shapes=[pltpu.VMEM((1, LANES), F32)],
        compiler_params=_params(("arbitrary", "arbitrary")),
        name="router",
    )(h, g.reshape(1, d), mod3, mod3, w_r, b_r)


def _expert_body(meta_ref, code_ref, bp_ref, wg_ref, wu_ref, wd_ref, y_ref,
                 wgb_ref, wub_ref, wdb_ref, xbuf, ybuf, xsem, ysem, *, n_tok, rpt):
    e = pl.program_id(0)
    blk0 = meta_ref[0, e]
    n_blk = meta_ref[0, e + 1] - blk0
    n_used = meta_ref[2, 0]
    dh = rpt * LANES
    tile_rows = MOE_BLOCK * rpt

    def tile(ref, i):
        return ref.at[pl.ds(pl.multiple_of(i * rpt, rpt), rpt), :]

    def gather(g, slot):
        def body(r, c):
            code = code_ref[g * MOE_BLOCK + r]
            tok = jnp.where(code < 2 * n_tok, code >> 1, 0)
            pltpu.make_async_copy(tile(bp_ref, tok), tile(xbuf.at[slot], r), xsem.at[slot]).start()
            return c

        lax.fori_loop(0, MOE_BLOCK, body, 0, unroll=8)

    def scatter(g):
        def body(r, c):
            code = code_ref[g * MOE_BLOCK + r]
            pltpu.make_async_copy(tile(ybuf, r), tile(y_ref, code), ysem).start()
            return c

        lax.fori_loop(0, MOE_BLOCK, body, 0, unroll=8)

    def wait_gather(slot):
        pltpu.make_async_copy(bp_ref.at[pl.ds(0, tile_rows), :], xbuf.at[slot], xsem.at[slot]).wait()

    def wait_scatter():
        pltpu.make_async_copy(ybuf, y_ref.at[pl.ds(0, tile_rows), :], ysem).wait()

    @pl.when(e == 0)
    def _():
        gather(0, 0)

    @pl.when(n_blk > 0)
    def _():
        wgb_ref[...] = wg_ref[0, 0].astype(BF16)
        wub_ref[...] = wu_ref[0, 0].astype(BF16)
        wdb_ref[...] = wd_ref[0, 0].astype(BF16)

        def block(j, c):
            g = blk0 + j
            slot = g % 2

            @pl.when(g + 1 < n_used)
            def _():
                gather(g + 1, 1 - slot)

            wait_gather(slot)
            x_lo, x_hi = _unpack_bf16_pairs(_load_token_tiles(xbuf.at[slot], MOE_BLOCK, rpt))
            x_lo = x_lo.astype(BF16)
            x_hi = x_hi.astype(BF16)
            gate = _dot(x_lo, wgb_ref[0:dh, :]) + _dot(x_hi, wgb_ref[dh:2 * dh, :])
            up = _dot(x_lo, wub_ref[0:dh, :]) + _dot(x_hi, wub_ref[dh:2 * dh, :])
            mid = (_silu(gate) * up).astype(BF16)
            out = _pack_bf16_pairs(_dot(mid, wdb_ref[...]))

            @pl.when(g > 0)
            def _():
                wait_scatter()

            _store_token_tiles(ybuf, out)
            scatter(g)
            return c

        lax.fori_loop(0, n_blk, block, 0)

    @pl.when(e == pl.num_programs(0) - 1)
    def _():
        wait_scatter()


def _experts(meta, code, bp, w_gate, w_up, w_down, layer, n_tok):
    n_slots = code.shape[0]
    rpt = bp.shape[0] // n_tok
    _, n_exp, d, hid = w_gate.shape

    def wspec(shape):
        return pl.BlockSpec((1, 1) + shape, lambda e, m, c: (layer, e, 0, 0))

    return pl.pallas_call(
        functools.partial(_expert_body, n_tok=n_tok, rpt=rpt),
        grid_spec=pltpu.PrefetchScalarGridSpec(
            num_scalar_prefetch=2,
            grid=(n_exp,),
            in_specs=[pl.BlockSpec(memory_space=pl.ANY),
                      wspec((d, hid)), wspec((d, hid)), wspec((hid, d))],
            out_specs=pl.BlockSpec(memory_space=pl.ANY),
            scratch_shapes=[pltpu.VMEM((d, hid), BF16), pltpu.VMEM((d, hid), BF16),
                            pltpu.VMEM((hid, d), BF16),
                            pltpu.VMEM((2, MOE_BLOCK * rpt, LANES), U32),
                            pltpu.VMEM((MOE_BLOCK * rpt, LANES), U32),
                            pltpu.SemaphoreType.DMA((2,)), pltpu.SemaphoreType.DMA],
        ),
        out_shape=jax.ShapeDtypeStruct(((2 * n_tok + n_slots) * rpt, LANES), U32),
        compiler_params=_params(("arbitrary",)),
        name="moe_experts",
    )(meta, code, bp, w_gate, w_up, w_down)


def _combine_body(h_ref, y_ref, rw_ref, gt_ref, fg_ref, o_ref, *, rpt, final):
    tm = h_ref.shape[1]
    dh = rpt * LANES
    wa = rw_ref[:, 0:1]
    wb = rw_ref[:, 1:2]
    a_lo, a_hi = _unpack_bf16_pairs(_load_token_tiles(y_ref, tm, rpt, 0, 2 * rpt))
    b_lo, b_hi = _unpack_bf16_pairs(_load_token_tiles(y_ref, tm, rpt, rpt, 2 * rpt))
    gt = gt_ref[0]
    lo = h_ref[0, :, 0:dh] + gt[:, 0:dh] * (wa * a_lo + wb * b_lo)
    hi = h_ref[0, :, dh:2 * dh] + gt[:, dh:2 * dh] * (wa * a_hi + wb * b_hi)
    if final:
        ms = (jnp.sum(lo * lo, axis=-1, keepdims=True)
              + jnp.sum(hi * hi, axis=-1, keepdims=True)) / (2 * dh)
        inv = lax.rsqrt(ms + EPS)
        lo = lo * inv * fg_ref[:, 0:dh]
        hi = hi * inv * fg_ref[:, dh:2 * dh]
    o_ref[0, :, 0:dh] = lo
    o_ref[0, :, dh:2 * dh] = hi


def _combine(h, y, rw, mod3, row_fn, sec_gate, final_g, final):
    bsz, t, d = h.shape
    tm = _pick(t, 256)
    nt = t // tm
    rpt = d // 2 // LANES
    return pl.pallas_call(
        functools.partial(_combine_body, rpt=rpt, final=final),
        grid=(bsz, nt),
        in_specs=[
            pl.BlockSpec((1, tm, d), lambda b, i: (b, i, 0)),
            pl.BlockSpec((tm * 2 * rpt, LANES), lambda b, i: (b * nt + i, 0)),
            pl.BlockSpec((tm, LANES), lambda b, i: (b * nt + i, 0)),
            _mod_spec(d, row_fn, sec_gate),
            pl.BlockSpec((1, d), lambda b, i: (0, 0)),
        ],
        out_specs=pl.BlockSpec((1, tm, d), lambda b, i: (b, i, 0)),
        out_shape=jax.ShapeDtypeStruct((bsz, t, d), F32),
        compiler_params=_params(("arbitrary", "arbitrary")),
        name="moe_combine",
    )(h, y, rw, mod3, final_g.reshape(1, d))


def _hier_moe(h, g, mod3, row_fn, w_r1, b_r1, w_r2, b_r2, w_gate, w_up, w_down, layer, final_g, final):
    bsz, t, d = h.shape
    n = bsz * t
    w_r = jnp.concatenate([jnp.transpose(w_r2, (1, 0, 2)).reshape(d, MOE_EXPERTS), w_r1], axis=1)
    w_r = jnp.pad(w_r, ((0, 0), (0, LANES - w_r.shape[1])))
    b_r = jnp.pad(jnp.concatenate([b_r2.reshape(-1), b_r1]), (0, LANES - MOE_EXPERTS - MOE_GROUPS))
    bp, ri, rw, cnt = _router(h, g, mod3, row_fn, 3, 4, w_r, b_r.reshape(1, LANES))

    counts = cnt[0, :MOE_EXPERTS].astype(I32)
    padded = (counts + MOE_BLOCK - 1) // MOE_BLOCK * MOE_BLOCK
    pad_end = jnp.cumsum(padded)
    pad_start = pad_end - padded
    n_slots = (-(-(n * 2) // MOE_BLOCK) + MOE_EXPERTS) * MOE_BLOCK
    blk_off = jnp.concatenate([pad_start, pad_end[-1:]]) // MOE_BLOCK
    row = lambda v: jnp.pad(v.astype(I32), (0, LANES - v.shape[0]))
    meta = jnp.stack([row(blk_off), row(counts), row(blk_off[-1:])])
    hit = ri[:, 0:2, None] == jnp.arange(MOE_EXPERTS, dtype=I32)
    pos = (jnp.sum(jnp.where(hit, pad_start, 0), axis=-1) + ri[:, 2:4]).reshape(-1)
    code = (2 * n + jnp.arange(n_slots, dtype=I32)).at[pos].set(jnp.arange(2 * n, dtype=I32))

    y = _experts(meta, code, bp, w_gate, w_up, w_down, layer, n)
    return _combine(h, y, rw, mod3, row_fn, 5, final_g, final)


def kernel(x, c, ctx, c_ctx, ada_w, ada_b, norm1_g, norm2_g, hgrn_w_in, hgrn_lb, hgrn_onorm_g,
           hgrn_w_out, conv_w_pw1, conv_w_dw, conv_b_dw, conv_ln_g, conv_ln_b, conv_w_pw2,
           moe_w_r1, moe_b_r1, moe_w_r2, moe_b_r2, moe_w_gate, moe_w_up, moe_w_down, final_g):
    bsz, t, d = x.shape
    t_ctx = ctx.shape[1]
    depth = ada_w.shape[0]
    assert depth == 2 and bsz < MOD_ROWS and d % (2 * LANES) == 0 and t % GRID_W == 0

    c_all = jnp.concatenate([c, c_ctx[None, :], jnp.zeros((MOD_ROWS - bsz - 1, d), F32)], axis=0)
    mod = _adaln(c_all, ada_w, ada_b)
    mod3 = mod.reshape(depth * MOD_ROWS, 1, 6 * d)
    lat_row = lambda l: (lambda b: l * MOD_ROWS + b)
    ctx_row = lambda l: (lambda b: l * MOD_ROWS + bsz)
    lb_all = jnp.cumsum(jax.nn.softmax(hgrn_lb.astype(F32), axis=1), axis=1)

    def moe(h, l, final):
        return _hier_moe(h, norm2_g[l], mod3, lat_row(l), moe_w_r1[l], moe_b_r1[l], moe_w_r2[l],
                         moe_b_r2[l], moe_w_gate, moe_w_up, moe_w_down, l, final_g, final)

    a_lat = _norm_mod(x, norm1_g[0], mod3, lat_row(0), 0, 1)
    a_ctx = _norm_mod(ctx, norm1_g[0], mod3, ctx_row(0), 0, 1)
    p_lat = _matmul(a_lat.reshape(bsz * t, d), hgrn_w_in[0]).reshape(bsz, t, 5 * d)
    p_ctx = _matmul(a_ctx.reshape(bsz * t_ctx, d), hgrn_w_in[0]).reshape(bsz, t_ctx, 5 * d)
    o2 = _scan(p_ctx, p_lat, lb_all[:, 0])
    h = _hgrn_readout(o2, p_lat, hgrn_onorm_g[0], hgrn_w_out[0], x, mod3, lat_row(0), 2)
    h = moe(h, 0, False)

    a = _norm_mod(h, norm1_g[1], mod3, lat_row(1), 0, 1)
    u = _matmul_glu(a.reshape(bsz * t, d), conv_w_pw1[0]).reshape(bsz, t, d)
    v = _dwconv(u, conv_w_dw[0], conv_b_dw[0])
    h = _conv_out(v, conv_ln_g[0], conv_ln_b[0], conv_w_pw2[0], h, mod3, lat_row(1), 2)
    return moe(h, 1, True)
```

```python
import functools

import jax
import jax.numpy as jnp
from jax import lax
from jax.experimental import pallas as pl
from jax.experimental.pallas import tpu as pltpu
from jax.experimental.pallas import tpu_sc as plsc

F32 = jnp.float32
BF16 = jnp.bfloat16
U32 = jnp.uint32
I32 = jnp.int32

EPS = 1e-6
LANES = 128
HEAD_DIM = 128
SCAN_CHUNK = 64
GRID_W = 64
CONV_WIDTH = 31
MOE_GROUPS = 8
MOE_PER_GROUP = 8
MOE_EXPERTS = MOE_GROUPS * MOE_PER_GROUP
MOE_BLOCK = 256
SC_GATHER_WINDOW = 256
MOD_ROWS = 16
VMEM_LIMIT = 56 * 1024 * 1024
SAFE_CHUNK_LOG_DECAY = -80.0
HI_MASK = 0xFFFF0000


def _pick(n, pref):
    t = pref
    while n % t:
        t //= 2
    return t


def _params(sem):
    return pltpu.CompilerParams(dimension_semantics=sem, vmem_limit_bytes=VMEM_LIMIT)


def _silu(x):
    return x * jax.nn.sigmoid(x)


def _dot(a, b):
    return jnp.dot(a, b, preferred_element_type=F32)


def _dot_nt(a, b):
    return lax.dot_general(a, b, (((1,), (1,)), ((), ())), preferred_element_type=F32)


def _dot_tn(a, b):
    return lax.dot_general(a, b, (((0,), (0,)), ((), ())), preferred_element_type=F32)


def _pack_bf16_pairs(x):
    n = x.shape[1] // 2
    bits = lax.bitcast_convert_type(x.astype(BF16).astype(F32), U32)
    return (bits[:, :n] >> 16) | (bits[:, n:] & jnp.uint32(HI_MASK))


def _unpack_bf16_pairs(w):
    lo = lax.bitcast_convert_type(w << 16, F32)
    hi = lax.bitcast_convert_type(w & jnp.uint32(HI_MASK), F32)
    return lo, hi


def _store_token_tiles(ref, w, first=0, stride=None):
    m, n = w.shape[0], w.shape[1] // LANES
    stride = n if stride is None else stride
    for s in range(n):
        ref[pl.ds(first + s, m, stride=stride), :] = w[:, s * LANES:(s + 1) * LANES]


def _load_token_tiles(ref, m, n, first=0, stride=None):
    stride = n if stride is None else stride
    return jnp.concatenate([ref[pl.ds(first + s, m, stride=stride), :] for s in range(n)], axis=1)


def _adaln_body(c_ref, w_ref, b_ref, o_ref):
    s = _silu(c_ref[...])
    o_ref[0] = _dot(s.astype(BF16), w_ref[0].astype(BF16)) + b_ref[0]


def _adaln(c_all, ada_w, ada_b):
    depth, d, n6 = ada_w.shape
    tn = _pick(n6, 1024)
    return pl.pallas_call(
        _adaln_body,
        grid=(depth, n6 // tn),
        in_specs=[
            pl.BlockSpec((MOD_ROWS, d), lambda l, j: (0, 0)),
            pl.BlockSpec((1, d, tn), lambda l, j: (l, 0, j)),
            pl.BlockSpec((1, 1, tn), lambda l, j: (l, 0, j)),
        ],
        out_specs=pl.BlockSpec((1, MOD_ROWS, tn), lambda l, j: (l, 0, j)),
        out_shape=jax.ShapeDtypeStruct((depth, MOD_ROWS, n6), F32),
        compiler_params=_params(("arbitrary", "arbitrary")),
        name="adaln",
    )(c_all, ada_w, ada_b.reshape(depth, 1, n6))


def _mod_spec(d, row_fn, sec):
    return pl.BlockSpec((1, 1, d), lambda b, i: (row_fn(b), 0, sec))


def _rms(x, g):
    return x * lax.rsqrt(jnp.mean(x * x, axis=-1, keepdims=True) + EPS) * g


def _norm_mod_body(x_ref, g_ref, sh_ref, sc_ref, o_ref):
    y = _rms(x_ref[0], g_ref[...])
    o_ref[0] = (y * (1.0 + sc_ref[0]) + sh_ref[0]).astype(o_ref.dtype)


def _norm_mod(x, g, mod3, row_fn, sec_shift, sec_scale):
    bsz, t, d = x.shape
    tm = _pick(t, 512)
    return pl.pallas_call(
        _norm_mod_body,
        grid=(bsz, t // tm),
        in_specs=[
            pl.BlockSpec((1, tm, d), lambda b, i: (b, i, 0)),
            pl.BlockSpec((1, d), lambda b, i: (0, 0)),
            _mod_spec(d, row_fn, sec_shift),
            _mod_spec(d, row_fn, sec_scale),
        ],
        out_specs=pl.BlockSpec((1, tm, d), lambda b, i: (b, i, 0)),
        out_shape=jax.ShapeDtypeStruct((bsz, t, d), BF16),
        compiler_params=_params(("arbitrary", "arbitrary")),
        name="norm_mod",
    )(x, g.reshape(1, d), mod3, mod3)


def _mm_body(a_ref, w_ref, o_ref, wb_ref):
    @pl.when(pl.program_id(1) == 0)
    def _():
        wb_ref[...] = w_ref[...].astype(BF16)

    o_ref[...] = _dot(a_ref[...], wb_ref[...]).astype(o_ref.dtype)


def _matmul(a, w):
    m, k = a.shape
    n = w.shape[1]
    tm, tn = _pick(m, 512), _pick(n, 1024)
    return pl.pallas_call(
        _mm_body,
        grid=(n // tn, m // tm),
        in_specs=[
            pl.BlockSpec((tm, k), lambda j, i: (i, 0)),
            pl.BlockSpec((k, tn), lambda j, i: (0, j)),
        ],
        out_specs=pl.BlockSpec((tm, tn), lambda j, i: (i, j)),
        out_shape=jax.ShapeDtypeStruct((m, n), BF16),
        scratch_shapes=[pltpu.VMEM((k, tn), BF16)],
        compiler_params=_params(("arbitrary", "arbitrary")),
        name="matmul",
    )(a, w)


def _mm_glu_body(a_ref, wv_ref, wg_ref, o_ref, wvb_ref, wgb_ref):
    @pl.when(pl.program_id(1) == 0)
    def _():
        wvb_ref[...] = wv_ref[...].astype(BF16)
        wgb_ref[...] = wg_ref[...].astype(BF16)

    a = a_ref[...]
    val = _dot(a, wvb_ref[...])
    gate = _dot(a, wgb_ref[...])
    o_ref[...] = (val * jax.nn.sigmoid(gate)).astype(o_ref.dtype)


def _matmul_glu(a, w):
    m, k = a.shape
    n = w.shape[1] // 2
    tm, tn = _pick(m, 512), _pick(n, 512)
    nj = n // tn
    return pl.pallas_call(
        _mm_glu_body,
        grid=(nj, m // tm),
        in_specs=[
            pl.BlockSpec((tm, k), lambda j, i: (i, 0)),
            pl.BlockSpec((k, tn), lambda j, i: (0, j)),
            pl.BlockSpec((k, tn), lambda j, i: (0, j + nj)),
        ],
        out_specs=pl.BlockSpec((tm, tn), lambda j, i: (i, j)),
        out_shape=jax.ShapeDtypeStruct((m, n), BF16),
        scratch_shapes=[pltpu.VMEM((k, tn), BF16), pltpu.VMEM((k, tn), BF16)],
        compiler_params=_params(("arbitrary", "arbitrary")),
        name="matmul_glu",
    )(a, w, w)


def _scan_body(qc_ref, vc_ref, zc_ref, ql_ref, vl_ref, zl_ref, lb_ref, o_ref,
               st_ref, flag_ref, b_sc, q_sc, k_sc, v_sc, *, nc, n_sub, heads):
    chunk = SCAN_CHUNK
    direction = pl.program_id(1)
    step = pl.program_id(2)
    fwd = direction == 0

    @pl.when(step == 0)
    def _():
        st_ref[0] = jnp.zeros(st_ref.shape[1:], F32)

    @pl.when(step > 0)
    def _():
        st_ref[0] = st_ref[n_sub]

    lb = lb_ref[0]
    row = lax.broadcasted_iota(I32, (chunk, chunk), 0)
    col = lax.broadcasted_iota(I32, (chunk, chunk), 1)
    sign = 1 - 2 * direction
    tri = (row - col) * sign >= 0
    tri_b = jnp.where(tri, 1.0, 0.0).astype(BF16)
    t_idx = lax.broadcasted_iota(I32, (chunk, 1), 0)

    def chunk_rows(ci):
        cidx = jnp.where(fwd, ci, n_sub - 1 - ci)
        return pl.ds(pl.multiple_of(cidx * chunk, chunk), chunk)

    def decay(z_ref, rows, sl):
        z = z_ref[0, rows, sl].astype(F32)
        f = lb[:, sl] + (1.0 - lb[:, sl]) * jax.nn.sigmoid(z)
        lf = jnp.log(f)
        hi = lf.astype(BF16)
        lo = (lf - hi.astype(F32)).astype(BF16)
        b = _dot(tri_b, hi) + _dot(tri_b, lo)
        tot = jnp.where(fwd, b[chunk - 1:chunk, :], b[0:1, :])
        return b, tot, 1.0 - f

    def fast_chunk(q_ref, v_ref, z_ref, ci, emit):
        rows = chunk_rows(ci)
        b, tot, kk = decay(z_ref, rows, slice(None))
        ke = kk * jnp.exp(tot - b)
        ke_b = ke.astype(BF16)
        etot = jnp.exp(tot)
        v = v_ref[0, rows, :]
        head_sl = [slice(h * HEAD_DIM, (h + 1) * HEAD_DIM) for h in range(heads)]
        st = [st_ref[ci, h] for h in range(heads)]
        kv = [_dot_tn(v[:, sl], ke_b[:, sl]) for sl in head_sl]
        if emit:
            qe = _silu(q_ref[0, rows, :].astype(F32)) * jnp.exp(b)
            qe_b = qe.astype(BF16)
            g = jnp.exp(-0.5 * tot)
            qh = (qe * g).astype(BF16)
            kh = (ke * g).astype(BF16)
            flag_ref[ci] = (jnp.min(tot) < SAFE_CHUNK_LOG_DECAY).astype(I32)
            sc = [_dot_nt(qh[:, sl], kh[:, sl]) for sl in head_sl]
            o_inter = [_dot_nt(qe_b[:, sl], st[h].astype(BF16)) for h, sl in enumerate(head_sl)]
            p = [jnp.where(tri, s, 0.0).astype(BF16) for s in sc]
            o_intra = [_dot(p[h], v[:, sl]) for h, sl in enumerate(head_sl)]
            for h, sl in enumerate(head_sl):
                o_ref[0, 0, rows, sl] = (o_inter[h] + o_intra[h]).astype(o_ref.dtype)
        for h, sl in enumerate(head_sl):
            st_ref[ci + 1, h] = st[h] * etot[:, sl] + kv[h]

    def exact_chunk(q_ref, v_ref, z_ref, ci):
        rows = chunk_rows(ci)
        b, _, kk = decay(z_ref, rows, slice(None))
        b_sc[...] = b
        q_sc[...] = _silu(q_ref[0, rows, :].astype(F32))
        k_sc[...] = kk
        v_sc[...] = v_ref[0, rows, :].astype(F32)

        def head_loop(h, c2):
            lanes = pl.ds(pl.multiple_of(h * HEAD_DIM, HEAD_DIM), HEAD_DIM)
            b_h = b_sc[:, lanes]
            q_h = q_sc[:, lanes]
            k_h = k_sc[:, lanes]
            v_h = v_sc[:, lanes]
            o_inter = _dot_nt((q_h * jnp.exp(b_h)).astype(BF16), st_ref[ci, h].astype(BF16))

            def key_loop(s, acc):
                pick = t_idx == s
                b_s = jnp.sum(jnp.where(pick, b_h, 0.0), axis=0, keepdims=True)
                k_s = jnp.sum(jnp.where(pick, k_h, 0.0), axis=0, keepdims=True)
                v_s = jnp.sum(jnp.where(pick, v_h, 0.0), axis=0, keepdims=True)
                valid = (t_idx - s) * sign >= 0
                dec = jnp.where(valid, jnp.exp(jnp.where(valid, b_h - b_s, 0.0)), 0.0)
                w = jnp.sum(q_h * dec * k_s, axis=1, keepdims=True)
                return acc + w * v_s

            o_intra = lax.fori_loop(0, chunk, key_loop, jnp.zeros((chunk, HEAD_DIM), F32))
            o_ref[0, 0, rows, lanes] = (o_inter + o_intra).astype(o_ref.dtype)
            return c2

        lax.fori_loop(0, heads, head_loop, 0)

    @pl.when(step < nc)
    def _():
        for ci in range(n_sub):
            fast_chunk(qc_ref, vc_ref, zc_ref, ci, False)

    @pl.when(step >= nc)
    def _():
        for ci in range(n_sub):
            fast_chunk(ql_ref, vl_ref, zl_ref, ci, True)

        def redo(ci, c):
            @pl.when(flag_ref[ci] == 1)
            def _():
                exact_chunk(ql_ref, vl_ref, zl_ref, ci)
            return c

        lax.fori_loop(0, n_sub, redo, 0)


def _scan(p_ctx, p_lat, lb):
    bsz, t_lat, d5 = p_lat.shape
    t_ctx = p_ctx.shape[1]
    d = d5 // 5
    heads = d // HEAD_DIM
    tb = _pick(t_ctx, 256)
    assert t_lat % tb == 0 and tb % SCAN_CHUNK == 0
    nc, nl = t_ctx // tb, t_lat // tb
    n_sub = tb // SCAN_CHUNK

    def ctx_blk(dr, s):
        sc = jnp.minimum(s, nc - 1)
        return jnp.where(dr == 0, sc, nc - 1 - sc)

    def lat_blk(dr, s):
        sl = jnp.maximum(s - nc, 0)
        return jnp.where(dr == 0, sl, nl - 1 - sl)

    def cspec(sec_fn):
        return pl.BlockSpec((1, tb, d), lambda b, dr, s: (b, ctx_blk(dr, s), sec_fn(dr)))

    def lspec(sec_fn):
        return pl.BlockSpec((1, tb, d), lambda b, dr, s: (b, lat_blk(dr, s), sec_fn(dr)))

    q_sec = lambda dr: 0
    v_sec = lambda dr: 1
    z_sec = lambda dr: 2 + dr
    return pl.pallas_call(
        functools.partial(_scan_body, nc=nc, n_sub=n_sub, heads=heads),
        grid=(bsz, 2, nc + nl),
        in_specs=[cspec(q_sec), cspec(v_sec), cspec(z_sec),
                  lspec(q_sec), lspec(v_sec), lspec(z_sec),
                  pl.BlockSpec((1, 1, d), lambda b, dr, s: (dr, 0, 0))],
        out_specs=pl.BlockSpec((1, 1, tb, d), lambda b, dr, s: (dr, b, lat_blk(dr, s), 0)),
        out_shape=jax.ShapeDtypeStruct((2, bsz, t_lat, d), BF16),
        scratch_shapes=[pltpu.VMEM((n_sub + 1, heads, HEAD_DIM, HEAD_DIM), F32),
                        pltpu.SMEM((n_sub,), I32),
                        pltpu.VMEM((SCAN_CHUNK, d), F32),
                        pltpu.VMEM((SCAN_CHUNK, d), F32),
                        pltpu.VMEM((SCAN_CHUNK, d), F32),
                        pltpu.VMEM((SCAN_CHUNK, d), F32)],
        compiler_params=_params(("arbitrary", "arbitrary", "arbitrary")),
        name="hgrn_scan",
    )(p_ctx, p_ctx, p_ctx, p_lat, p_lat, p_lat, lb.reshape(2, 1, d))


def _hgrn_readout_body(of_ref, ob_ref, g_ref, on_ref, w_ref, h_ref, gt_ref, o_ref,
                       wb_ref, a_ref, *, heads):
    @pl.when((pl.program_id(0) == 0) & (pl.program_id(1) == 0))
    def _():
        wb_ref[...] = w_ref[...].astype(BF16)

    on = on_ref[...]
    for h in range(heads):
        sl = slice(h * HEAD_DIM, (h + 1) * HEAD_DIM)
        o = of_ref[0, 0, :, sl].astype(F32) + ob_ref[0, 0, :, sl].astype(F32)
        o = _rms(o, on)
        a_ref[:, sl] = (o * _silu(g_ref[0, :, sl].astype(F32))).astype(BF16)
    y = _dot(a_ref[...], wb_ref[...])
    o_ref[0] = h_ref[0] + gt_ref[0] * y


def _hgrn_readout(o2, p_lat, onorm_g, w_out, h, mod3, row_fn, sec_gate):
    bsz, t, d = h.shape
    heads = d // HEAD_DIM
    tm = _pick(t, 256)
    return pl.pallas_call(
        functools.partial(_hgrn_readout_body, heads=heads),
        grid=(bsz, t // tm),
        in_specs=[
            pl.BlockSpec((1, 1, tm, d), lambda b, i: (0, b, i, 0)),
            pl.BlockSpec((1, 1, tm, d), lambda b, i: (1, b, i, 0)),
            pl.BlockSpec((1, tm, d), lambda b, i: (b, i, 4)),
            pl.BlockSpec((1, HEAD_DIM), lambda b, i: (0, 0)),
            pl.BlockSpec((d, d), lambda b, i: (0, 0), pipeline_mode=pl.Buffered(1)),
            pl.BlockSpec((1, tm, d), lambda b, i: (b, i, 0)),
            _mod_spec(d, row_fn, sec_gate),
        ],
        out_specs=pl.BlockSpec((1, tm, d), lambda b, i: (b, i, 0)),
        out_shape=jax.ShapeDtypeStruct((bsz, t, d), F32),
        scratch_shapes=[pltpu.VMEM((d, d), BF16), pltpu.VMEM((tm, d), BF16)],
        compiler_params=_params(("arbitrary", "arbitrary")),
        name="hgrn_readout",
    )(o2, o2, p_lat, onorm_g.reshape(1, HEAD_DIM), w_out, h, mod3)


def _conv_out_body(v_ref, lg_ref, lbias_ref, w_ref, h_ref, gt_ref, o_ref, wb_ref):
    @pl.when((pl.program_id(0) == 0) & (pl.program_id(1) == 0))
    def _():
        wb_ref[...] = w_ref[...].astype(BF16)

    x = v_ref[0].astype(F32)
    mu = jnp.mean(x, axis=-1, keepdims=True)
    xc = x - mu
    y = xc * lax.rsqrt(jnp.mean(xc * xc, axis=-1, keepdims=True) + EPS)
    y = _silu(y * lg_ref[...] + lbias_ref[...])
    o_ref[0] = h_ref[0] + gt_ref[0] * _dot(y.astype(BF16), wb_ref[...])


def _conv_out(v, ln_g, ln_b, w, h, mod3, row_fn, sec_gate):
    bsz, t, d = h.shape
    tm = _pick(t, 256)
    return pl.pallas_call(
        _conv_out_body,
        grid=(bsz, t // tm),
        in_specs=[
            pl.BlockSpec((1, tm, d), lambda b, i: (b, i, 0)),
            pl.BlockSpec((1, d), lambda b, i: (0, 0)),
            pl.BlockSpec((1, d), lambda b, i: (0, 0)),
            pl.BlockSpec((d, d), lambda b, i: (0, 0), pipeline_mode=pl.Buffered(1)),
            pl.BlockSpec((1, tm, d), lambda b, i: (b, i, 0)),
            _mod_spec(d, row_fn, sec_gate),
        ],
        out_specs=pl.BlockSpec((1, tm, d), lambda b, i: (b, i, 0)),
        out_shape=jax.ShapeDtypeStruct((bsz, t, d), F32),
        scratch_shapes=[pltpu.VMEM((d, d), BF16)],
        compiler_params=_params(("arbitrary", "arbitrary")),
        name="conv_out",
    )(v, ln_g.reshape(1, d), ln_b.reshape(1, d), w, h, mod3)


def _dwconv_body(u_ref, w_ref, b_ref, o_ref, pad_ref, *, rows, cb, half_blocks):
    half = CONV_WIDTH // 2
    j = pl.program_id(1)
    n_lane = cb // LANES
    bias = b_ref[...]

    def taps(read, r):
        for lb in range(n_lane):
            ln = slice(lb * LANES, (lb + 1) * LANES)
            acc = jnp.zeros((GRID_W, LANES), F32)
            for k in range(CONV_WIDTH):
                acc = acc + w_ref[k:k + 1, ln] * read(r, k, ln)
            o_ref[0, pl.ds(pl.multiple_of(r * GRID_W, GRID_W), GRID_W), ln] = (
                acc + bias[:, ln]).astype(o_ref.dtype)

    @pl.when(j < half_blocks)
    def _():
        zeros = jnp.zeros((rows, 16, cb), F32)
        pad_ref[0:rows, 0:16, :] = zeros
        pad_ref[0:rows, 16 + GRID_W:32 + GRID_W, :] = zeros
        pad_ref[0:rows, 16:16 + GRID_W, :] = u_ref[0].astype(F32).reshape(rows, GRID_W, cb)

        def row_loop(r, c):
            taps(lambda r_, k, ln: pad_ref[r_, 16 - half + k:16 - half + k + GRID_W, ln], r)
            return c

        lax.fori_loop(0, rows, row_loop, 0)

    @pl.when(j >= half_blocks)
    def _():
        zeros = jnp.zeros((half, GRID_W, cb), F32)
        pad_ref[0:half, 0:GRID_W, :] = zeros
        pad_ref[half + rows:2 * half + rows, 0:GRID_W, :] = zeros
        pad_ref[half:half + rows, 0:GRID_W, :] = u_ref[0].astype(F32).reshape(rows, GRID_W, cb)

        def row_loop(r, c):
            taps(lambda r_, k, ln: pad_ref[r_ + k, 0:GRID_W, ln], r)
            return c

        lax.fori_loop(0, rows, row_loop, 0)


def _dwconv(u, w_dw, b_dw):
    bsz, t, d = u.shape
    rows = t // GRID_W
    cb = _pick(d // 2, 256)
    half_blocks = (d // 2) // cb
    return pl.pallas_call(
        functools.partial(_dwconv_body, rows=rows, cb=cb, half_blocks=half_blocks),
        grid=(bsz, d // cb),
        in_specs=[
            pl.BlockSpec((1, t, cb), lambda b, j: (b, 0, j)),
            pl.BlockSpec((CONV_WIDTH, cb), lambda b, j: (0, j)),
            pl.BlockSpec((1, cb), lambda b, j: (0, j)),
        ],
        out_specs=pl.BlockSpec((1, t, cb), lambda b, j: (b, 0, j)),
        out_shape=jax.ShapeDtypeStruct((bsz, t, d), BF16),
        scratch_shapes=[pltpu.VMEM((rows + 2 * (CONV_WIDTH // 2) + 2, GRID_W + 32, cb), F32)],
        compiler_params=_params(("arbitrary", "arbitrary")),
        name="dwconv",
    )(u, w_dw, b_dw.reshape(1, d))


def _router_body(h_ref, g_ref, sh_ref, sc_ref, wr_ref, br_ref,
                 bp_ref, ri_ref, rw_ref, cnt_ref, carry_ref):
    tm = h_ref.shape[1]

    @pl.when((pl.program_id(0) == 0) & (pl.program_id(1) == 0))
    def _():
        carry_ref[...] = jnp.zeros_like(carry_ref)

    bmod = _rms(h_ref[0], g_ref[...]) * (1.0 + sc_ref[0]) + sh_ref[0]
    _store_token_tiles(bp_ref, _pack_bf16_pairs(bmod))

    b_hi = bmod.astype(BF16)
    b_lo = (bmod - b_hi.astype(F32)).astype(BF16)
    w = wr_ref[...]
    w_hi = w.astype(BF16)
    w_lo = (w - w_hi.astype(F32)).astype(BF16)
    logits = _dot(b_hi, w_hi) + _dot(b_hi, w_lo) + _dot(b_lo, w_hi) + br_ref[...]

    lane_i = lax.broadcasted_iota(I32, (tm, LANES), 1)
    lane = lane_i.astype(F32)
    neg = jnp.float32(-jnp.inf)
    big = jnp.float32(1 << 20)
    is_grp = (lane_i >= MOE_EXPERTS) & (lane_i < MOE_EXPERTS + MOE_GROUPS)
    m1 = jnp.max(jnp.where(is_grp, logits, neg), axis=1, keepdims=True)
    grp = jnp.min(jnp.where(is_grp & (logits == m1), lane - MOE_EXPERTS, big), axis=1, keepdims=True)
    pg = 1.0 / jnp.sum(jnp.where(is_grp, jnp.exp(logits - m1), 0.0), axis=1, keepdims=True)

    sel = (lane >= grp * MOE_PER_GROUP) & (lane < (grp + 1) * MOE_PER_GROUP)
    m2 = jnp.max(jnp.where(sel, logits, neg), axis=1, keepdims=True)
    ea = jnp.min(jnp.where(sel & (logits == m2), lane, big), axis=1, keepdims=True)
    sel2 = sel & (lane != ea)
    m3 = jnp.max(jnp.where(sel2, logits, neg), axis=1, keepdims=True)
    eb = jnp.min(jnp.where(sel2 & (logits == m3), lane, big), axis=1, keepdims=True)
    tr = jnp.exp(m3 - m2)
    wa = pg / (1.0 + tr)
    wb = wa * tr

    hit_a = lane == ea
    hit_b = lane == eb
    onehot = jnp.where(hit_a | hit_b, 1.0, 0.0)
    r_i = lax.broadcasted_iota(I32, (tm, tm), 0)
    c_i = lax.broadcasted_iota(I32, (tm, tm), 1)
    lower = jnp.where(r_i > c_i, 1.0, 0.0).astype(BF16)
    before = _dot(lower, onehot.astype(BF16)) + carry_ref[...]
    rank_a = jnp.sum(jnp.where(hit_a, before, 0.0), axis=1, keepdims=True).astype(I32)
    rank_b = jnp.sum(jnp.where(hit_b, before, 0.0), axis=1, keepdims=True).astype(I32)
    carry = carry_ref[...] + jnp.sum(onehot, axis=0, keepdims=True)
    carry_ref[...] = carry
    cnt_ref[...] = carry

    ri_ref[...] = jnp.where(lane_i == 0, ea.astype(I32), jnp.where(lane_i == 1, eb.astype(I32),
                            jnp.where(lane_i == 2, rank_a, jnp.where(lane_i == 3, rank_b, 0))))
    rw_ref[...] = jnp.where(lane_i == 0, wa, jnp.where(lane_i == 1, wb, 0.0))


def _router(h, g, mod3, row_fn, sec_shift, sec_scale, w_r, b_r):
    bsz, t, d = h.shape
    tm = _pick(t, 256)
    nt = t // tm
    n = bsz * t
    rpt = d // 2 // LANES
    return pl.pallas_call(
        _router_body,
        grid=(bsz, nt),
        in_specs=[
            pl.BlockSpec((1, tm, d), lambda b, i: (b, i, 0)),
            pl.BlockSpec((1, d), lambda b, i: (0, 0)),
            _mod_spec(d, row_fn, sec_shift),
            _mod_spec(d, row_fn, sec_scale),
            pl.BlockSpec((d, LANES), lambda b, i: (0, 0)),
            pl.BlockSpec((1, LANES), lambda b, i: (0, 0)),
        ],
        out_specs=[
            pl.BlockSpec((tm * rpt, LANES), lambda b, i: (b * nt + i, 0)),
            pl.BlockSpec((tm, LANES), lambda b, i: (b * nt + i, 0)),
            pl.BlockSpec((tm, LANES), lambda b, i: (b * nt + i, 0)),
            pl.BlockSpec((1, LANES), lambda b, i: (0, 0)),
        ],
        out_shape=[
            jax.ShapeDtypeStruct((n * rpt, LANES), U32),
            jax.ShapeDtypeStruct((n, LANES), I32),
            jax.ShapeDtypeStruct((n, LANES), F32),
            jax.ShapeDtypeStruct((1, LANES), F32),
        ],
        scratch_shapes=[pltpu.VMEM((1, LANES), F32)],
        compiler_params=_params(("arbitrary", "arbitrary")),
        name="router",
    )(h, g.reshape(1, d), mod3, mod3, w_r, b_r)


def _expert_body(meta_ref, code_ref, bp_ref, wg_ref, wu_ref, wd_ref, y_ref,
                 wgb_ref, wub_ref, wdb_ref, xbuf, ybuf, xsem, ysem, *, n_tok, rpt):
    e = pl.program_id(0)
    blk0 = meta_ref[0, e]
    n_blk = meta_ref[0, e + 1] - blk0
    n_used = meta_ref[2, 0]
    dh = rpt * LANES
    tile_rows = MOE_BLOCK * rpt

    def tile(ref, i):
        return ref.at[pl.ds(pl.multiple_of(i * rpt, rpt), rpt), :]

    def gather(g, slot):
        def body(r, c):
            code = code_ref[g * MOE_BLOCK + r]
            tok = jnp.where(code < 2 * n_tok, code >> 1, 0)
            pltpu.make_async_copy(tile(bp_ref, tok), tile(xbuf.at[slot], r), xsem.at[slot]).start()
            return c

        lax.fori_loop(0, MOE_BLOCK, body, 0, unroll=8)

    def write_out(g):
        return pltpu.make_async_copy(
            ybuf, y_ref.at[pl.ds(pl.multiple_of(g * tile_rows, tile_rows), tile_rows), :], ysem)

    def wait_gather(slot):
        pltpu.make_async_copy(bp_ref.at[pl.ds(0, tile_rows), :], xbuf.at[slot], xsem.at[slot]).wait()

    @pl.when(e == 0)
    def _():
        gather(0, 0)

    @pl.when(n_blk > 0)
    def _():
        wgb_ref[...] = wg_ref[0, 0].astype(BF16)
        wub_ref[...] = wu_ref[0, 0].astype(BF16)
        wdb_ref[...] = wd_ref[0, 0].astype(BF16)

        def block(j, c):
            g = blk0 + j
            slot = g % 2

            @pl.when(g + 1 < n_used)
            def _():
                gather(g + 1, 1 - slot)

            wait_gather(slot)
            x_lo, x_hi = _unpack_bf16_pairs(_load_token_tiles(xbuf.at[slot], MOE_BLOCK, rpt))
            x_lo = x_lo.astype(BF16)
            x_hi = x_hi.astype(BF16)
            gate = _dot(x_lo, wgb_ref[0:dh, :]) + _dot(x_hi, wgb_ref[dh:2 * dh, :])
            up = _dot(x_lo, wub_ref[0:dh, :]) + _dot(x_hi, wub_ref[dh:2 * dh, :])
            mid = (_silu(gate) * up).astype(BF16)
            out = _pack_bf16_pairs(_dot(mid, wdb_ref[...]))

            @pl.when(g > 0)
            def _():
                write_out(g - 1).wait()

            _store_token_tiles(ybuf, out)
            write_out(g).start()
            return c

        lax.fori_loop(0, n_blk, block, 0)

    @pl.when(e == pl.num_programs(0) - 1)
    def _():
        write_out(n_used - 1).wait()


def _experts(meta, code, bp, w_gate, w_up, w_down, layer, n_tok):
    n_slots = code.shape[0]
    rpt = bp.shape[0] // n_tok
    _, n_exp, d, hid = w_gate.shape

    def wspec(shape):
        return pl.BlockSpec((1, 1) + shape, lambda e, m, c: (layer, e, 0, 0))

    return pl.pallas_call(
        functools.partial(_expert_body, n_tok=n_tok, rpt=rpt),
        grid_spec=pltpu.PrefetchScalarGridSpec(
            num_scalar_prefetch=2,
            grid=(n_exp,),
            in_specs=[pl.BlockSpec(memory_space=pl.ANY),
                      wspec((d, hid)), wspec((d, hid)), wspec((hid, d))],
            out_specs=pl.BlockSpec(memory_space=pl.ANY),
            scratch_shapes=[pltpu.VMEM((d, hid), BF16), pltpu.VMEM((d, hid), BF16),
                            pltpu.VMEM((hid, d), BF16),
                            pltpu.VMEM((2, MOE_BLOCK * rpt, LANES), U32),
                            pltpu.VMEM((MOE_BLOCK * rpt, LANES), U32),
                            pltpu.SemaphoreType.DMA((2,)), pltpu.SemaphoreType.DMA],
        ),
        out_shape=jax.ShapeDtypeStruct((n_slots * rpt, LANES), U32),
        compiler_params=_params(("arbitrary",)),
        name="moe_experts",
    )(meta, code, bp, w_gate, w_up, w_down)


def _sc_gather_rows(table, idx):
    n_idx = idx.shape[0]
    win = _pick(n_idx, SC_GATHER_WINDOW)
    mesh = plsc.VectorSubcoreMesh(core_axis_name="core", subcore_axis_name="subcore")

    @pl.kernel(out_type=jax.ShapeDtypeStruct((n_idx, LANES), table.dtype), mesh=mesh)
    def gather_kernel(t_hbm, i_hbm, o_hbm):
        def window(i_vmem, o_vmem):
            pltpu.sync_copy(t_hbm.at[i_vmem.at[0]], o_vmem)

        pltpu.emit_pipeline(
            window,
            grid=(n_idx // win,),
            in_specs=[pl.BlockSpec((1, win), index_map=lambda i: (0, i))],
            out_specs=[pl.BlockSpec((win, LANES), index_map=lambda i: (i, 0))],
            core_axis_name=("core", "subcore"),
            dimension_semantics=(pltpu.PARALLEL,),
        )(i_hbm, o_hbm)

    return gather_kernel(table, idx.reshape(1, n_idx))


def _combine_body(h_ref, y_ref, rw_ref, gt_ref, fg_ref, o_ref, *, rpt, final):
    tm = h_ref.shape[1]
    dh = rpt * LANES
    wa = rw_ref[:, 0:1]
    wb = rw_ref[:, 1:2]
    a_lo, a_hi = _unpack_bf16_pairs(_load_token_tiles(y_ref, tm, rpt, 0, 2 * rpt))
    b_lo, b_hi = _unpack_bf16_pairs(_load_token_tiles(y_ref, tm, rpt, rpt, 2 * rpt))
    gt = gt_ref[0]
    lo = h_ref[0, :, 0:dh] + gt[:, 0:dh] * (wa * a_lo + wb * b_lo)
    hi = h_ref[0, :, dh:2 * dh] + gt[:, dh:2 * dh] * (wa * a_hi + wb * b_hi)
    if final:
        ms = (jnp.sum(lo * lo, axis=-1, keepdims=True)
              + jnp.sum(hi * hi, axis=-1, keepdims=True)) / (2 * dh)
        inv = lax.rsqrt(ms + EPS)
        lo = lo * inv * fg_ref[:, 0:dh]
        hi = hi * inv * fg_ref[:, dh:2 * dh]
    o_ref[0, :, 0:dh] = lo
    o_ref[0, :, dh:2 * dh] = hi


def _combine(h, y, rw, mod3, row_fn, sec_gate, final_g, final):
    bsz, t, d = h.shape
    tm = _pick(t, 256)
    nt = t // tm
    rpt = d // 2 // LANES
    return pl.pallas_call(
        functools.partial(_combine_body, rpt=rpt, final=final),
        grid=(bsz, nt),
        in_specs=[
            pl.BlockSpec((1, tm, d), lambda b, i: (b, i, 0)),
            pl.BlockSpec((tm * 2 * rpt, LANES), lambda b, i: (b * nt + i, 0)),
            pl.BlockSpec((tm, LANES), lambda b, i: (b * nt + i, 0)),
            _mod_spec(d, row_fn, sec_gate),
            pl.BlockSpec((1, d), lambda b, i: (0, 0)),
        ],
        out_specs=pl.BlockSpec((1, tm, d), lambda b, i: (b, i, 0)),
        out_shape=jax.ShapeDtypeStruct((bsz, t, d), F32),
        compiler_params=_params(("arbitrary", "arbitrary")),
        name="moe_combine",
    )(h, y, rw, mod3, final_g.reshape(1, d))


def _hier_moe(h, g, mod3, row_fn, w_r1, b_r1, w_r2, b_r2, w_gate, w_up, w_down, layer, final_g, final):
    bsz, t, d = h.shape
    n = bsz * t
    w_r = jnp.concatenate([jnp.transpose(w_r2, (1, 0, 2)).reshape(d, MOE_EXPERTS), w_r1], axis=1)
    w_r = jnp.pad(w_r, ((0, 0), (0, LANES - w_r.shape[1])))
    b_r = jnp.pad(jnp.concatenate([b_r2.reshape(-1), b_r1]), (0, LANES - MOE_EXPERTS - MOE_GROUPS))
    bp, ri, rw, cnt = _router(h, g, mod3, row_fn, 3, 4, w_r, b_r.reshape(1, LANES))

    counts = cnt[0, :MOE_EXPERTS].astype(I32)
    padded = (counts + MOE_BLOCK - 1) // MOE_BLOCK * MOE_BLOCK
    pad_end = jnp.cumsum(padded)
    pad_start = pad_end - padded
    n_slots = (-(-(n * 2) // MOE_BLOCK) + MOE_EXPERTS) * MOE_BLOCK
    blk_off = jnp.concatenate([pad_start, pad_end[-1:]]) // MOE_BLOCK
    row = lambda v: jnp.pad(v.astype(I32), (0, LANES - v.shape[0]))
    meta = jnp.stack([row(blk_off), row(counts), row(blk_off[-1:])])
    hit = ri[:, 0:2, None] == jnp.arange(MOE_EXPERTS, dtype=I32)
    pos = (jnp.sum(jnp.where(hit, pad_start, 0), axis=-1) + ri[:, 2:4]).reshape(-1)
    code = (2 * n + jnp.arange(n_slots, dtype=I32)).at[pos].set(jnp.arange(2 * n, dtype=I32))

    ys = _experts(meta, code, bp, w_gate, w_up, w_down, layer, n)
    rpt = d // 2 // LANES
    y = _sc_gather_rows(ys, (pos[:, None] * rpt + jnp.arange(rpt, dtype=I32)).reshape(-1))
    return _combine(h, y, rw, mod3, row_fn, 5, final_g, final)


def kernel(x, c, ctx, c_ctx, ada_w, ada_b, norm1_g, norm2_g, hgrn_w_in, hgrn_lb, hgrn_onorm_g,
           hgrn_w_out, conv_w_pw1, conv_w_dw, conv_b_dw, conv_ln_g, conv_ln_b, conv_w_pw2,
           moe_w_r1, moe_b_r1, moe_w_r2, moe_b_r2, moe_w_gate, moe_w_up, moe_w_down, final_g):
    bsz, t, d = x.shape
    t_ctx = ctx.shape[1]
    depth = ada_w.shape[0]
    assert depth == 2 and bsz < MOD_ROWS and d % (2 * LANES) == 0 and t % GRID_W == 0

    c_all = jnp.concatenate([c, c_ctx[None, :], jnp.zeros((MOD_ROWS - bsz - 1, d), F32)], axis=0)
    mod = _adaln(c_all, ada_w, ada_b)
    mod3 = mod.reshape(depth * MOD_ROWS, 1, 6 * d)
    lat_row = lambda l: (lambda b: l * MOD_ROWS + b)
    ctx_row = lambda l: (lambda b: l * MOD_ROWS + bsz)
    lb_all = jnp.cumsum(jax.nn.softmax(hgrn_lb.astype(F32), axis=1), axis=1)

    def moe(h, l, final):
        return _hier_moe(h, norm2_g[l], mod3, lat_row(l), moe_w_r1[l], moe_b_r1[l], moe_w_r2[l],
                         moe_b_r2[l], moe_w_gate, moe_w_up, moe_w_down, l, final_g, final)

    a_lat = _norm_mod(x, norm1_g[0], mod3, lat_row(0), 0, 1)
    a_ctx = _norm_mod(ctx, norm1_g[0], mod3, ctx_row(0), 0, 1)
    p_lat = _matmul(a_lat.reshape(bsz * t, d), hgrn_w_in[0]).reshape(bsz, t, 5 * d)
    p_ctx = _matmul(a_ctx.reshape(bsz * t_ctx, d), hgrn_w_in[0]).reshape(bsz, t_ctx, 5 * d)
    o2 = _scan(p_ctx, p_lat, lb_all[:, 0])
    h = _hgrn_readout(o2, p_lat, hgrn_onorm_g[0], hgrn_w_out[0], x, mod3, lat_row(0), 2)
    h = moe(h, 0, False)

    a = _norm_mod(h, norm1_g[1], mod3, lat_row(1), 0, 1)
    u = _matmul_glu(a.reshape(bsz * t, d), conv_w_pw1[0]).reshape(bsz, t, d)
    v = _dwconv(u, conv_w_dw[0], conv_b_dw[0])
    h = _conv_out(v, conv_ln_g[0], conv_ln_b[0], conv_w_pw2[0], h, mod3, lat_row(1), 2)
    return moe(h, 1, True)
```

```python
import functools

import jax
import jax.numpy as jnp
from jax import lax
from jax.experimental import pallas as pl
from jax.experimental.pallas import tpu as pltpu
from jax.experimental.pallas import tpu_sc as plsc

F32 = jnp.float32
BF16 = jnp.bfloat16
U32 = jnp.uint32
I32 = jnp.int32

EPS = 1e-6
LANES = 128
HEAD_DIM = 128
SCAN_CHUNK = 64
GRID_W = 64
CONV_WIDTH = 31
MOE_GROUPS = 8
MOE_PER_GROUP = 8
MOE_EXPERTS = MOE_GROUPS * MOE_PER_GROUP
MOE_BLOCK = 256
SC_GATHER_WINDOW = 256
MOD_ROWS = 16
VMEM_LIMIT = 56 * 1024 * 1024
SAFE_CHUNK_LOG_DECAY = -80.0
HI_MASK = 0xFFFF0000


def _pick(n, pref):
    t = pref
    while n % t:
        t //= 2
    return t


def _params(sem):
    return pltpu.CompilerParams(dimension_semantics=sem, vmem_limit_bytes=VMEM_LIMIT)


def _silu(x):
    return x * jax.nn.sigmoid(x)


def _dot(a, b):
    return jnp.dot(a, b, preferred_element_type=F32)


def _dot_nt(a, b):
    return lax.dot_general(a, b, (((1,), (1,)), ((), ())), preferred_element_type=F32)


def _dot_tn(a, b):
    return lax.dot_general(a, b, (((0,), (0,)), ((), ())), preferred_element_type=F32)


def _pack_bf16_pairs(x):
    n = x.shape[1] // 2
    bits = lax.bitcast_convert_type(x.astype(BF16).astype(F32), U32)
    return (bits[:, :n] >> 16) | (bits[:, n:] & jnp.uint32(HI_MASK))


def _unpack_bf16_pairs(w):
    lo = lax.bitcast_convert_type(w << 16, F32)
    hi = lax.bitcast_convert_type(w & jnp.uint32(HI_MASK), F32)
    return lo, hi


def _store_token_tiles(ref, w, first=0, stride=None):
    m, n = w.shape[0], w.shape[1] // LANES
    stride = n if stride is None else stride
    for s in range(n):
        ref[pl.ds(first + s, m, stride=stride), :] = w[:, s * LANES:(s + 1) * LANES]


def _load_token_tiles(ref, m, n, first=0, stride=None):
    stride = n if stride is None else stride
    return jnp.concatenate([ref[pl.ds(first + s, m, stride=stride), :] for s in range(n)], axis=1)


def _adaln_body(c_ref, w_ref, b_ref, o_ref):
    s = _silu(c_ref[...])
    o_ref[0] = _dot(s.astype(BF16), w_ref[0].astype(BF16)) + b_ref[0]


def _adaln(c_all, ada_w, ada_b):
    depth, d, n6 = ada_w.shape
    tn = _pick(n6, 1024)
    return pl.pallas_call(
        _adaln_body,
        grid=(depth, n6 // tn),
        in_specs=[
            pl.BlockSpec((MOD_ROWS, d), lambda l, j: (0, 0)),
            pl.BlockSpec((1, d, tn), lambda l, j: (l, 0, j)),
            pl.BlockSpec((1, 1, tn), lambda l, j: (l, 0, j)),
        ],
        out_specs=pl.BlockSpec((1, MOD_ROWS, tn), lambda l, j: (l, 0, j)),
        out_shape=jax.ShapeDtypeStruct((depth, MOD_ROWS, n6), F32),
        compiler_params=_params(("arbitrary", "arbitrary")),
        name="adaln",
    )(c_all, ada_w, ada_b.reshape(depth, 1, n6))


def _mod_spec(d, row_fn, sec):
    return pl.BlockSpec((1, 1, d), lambda b, i: (row_fn(b), 0, sec))


def _rms(x, g):
    return x * lax.rsqrt(jnp.mean(x * x, axis=-1, keepdims=True) + EPS) * g


def _norm_mod_body(x_ref, g_ref, sh_ref, sc_ref, o_ref):
    y = _rms(x_ref[0], g_ref[...])
    o_ref[0] = (y * (1.0 + sc_ref[0]) + sh_ref[0]).astype(o_ref.dtype)


def _norm_mod(x, g, mod3, row_fn, sec_shift, sec_scale):
    bsz, t, d = x.shape
    tm = _pick(t, 512)
    return pl.pallas_call(
        _norm_mod_body,
        grid=(bsz, t // tm),
        in_specs=[
            pl.BlockSpec((1, tm, d), lambda b, i: (b, i, 0)),
            pl.BlockSpec((1, d), lambda b, i: (0, 0)),
            _mod_spec(d, row_fn, sec_shift),
            _mod_spec(d, row_fn, sec_scale),
        ],
        out_specs=pl.BlockSpec((1, tm, d), lambda b, i: (b, i, 0)),
        out_shape=jax.ShapeDtypeStruct((bsz, t, d), BF16),
        compiler_params=_params(("arbitrary", "arbitrary")),
        name="norm_mod",
    )(x, g.reshape(1, d), mod3, mod3)


def _mm_body(a_ref, w_ref, o_ref, wb_ref):
    @pl.when(pl.program_id(1) == 0)
    def _():
        wb_ref[...] = w_ref[...].astype(BF16)

    o_ref[...] = _dot(a_ref[...], wb_ref[...]).astype(o_ref.dtype)


def _matmul(a, w):
    m, k = a.shape
    n = w.shape[1]
    tm, tn = _pick(m, 512), _pick(n, 1024)
    return pl.pallas_call(
        _mm_body,
        grid=(n // tn, m // tm),
        in_specs=[
            pl.BlockSpec((tm, k), lambda j, i: (i, 0)),
            pl.BlockSpec((k, tn), lambda j, i: (0, j)),
        ],
        out_specs=pl.BlockSpec((tm, tn), lambda j, i: (i, j)),
        out_shape=jax.ShapeDtypeStruct((m, n), BF16),
        scratch_shapes=[pltpu.VMEM((k, tn), BF16)],
        compiler_params=_params(("arbitrary", "arbitrary")),
        name="matmul",
    )(a, w)


def _mm_glu_body(a_ref, wv_ref, wg_ref, o_ref, wvb_ref, wgb_ref):
    @pl.when(pl.program_id(1) == 0)
    def _():
        wvb_ref[...] = wv_ref[...].astype(BF16)
        wgb_ref[...] = wg_ref[...].astype(BF16)

    a = a_ref[...]
    val = _dot(a, wvb_ref[...])
    gate = _dot(a, wgb_ref[...])
    o_ref[...] = (val * jax.nn.sigmoid(gate)).astype(o_ref.dtype)


def _matmul_glu(a, w):
    m, k = a.shape
    n = w.shape[1] // 2
    tm, tn = _pick(m, 512), _pick(n, 512)
    nj = n // tn
    return pl.pallas_call(
        _mm_glu_body,
        grid=(nj, m // tm),
        in_specs=[
            pl.BlockSpec((tm, k), lambda j, i: (i, 0)),
            pl.BlockSpec((k, tn), lambda j, i: (0, j)),
            pl.BlockSpec((k, tn), lambda j, i: (0, j + nj)),
        ],
        out_specs=pl.BlockSpec((tm, tn), lambda j, i: (i, j)),
        out_shape=jax.ShapeDtypeStruct((m, n), BF16),
        scratch_shapes=[pltpu.VMEM((k, tn), BF16), pltpu.VMEM((k, tn), BF16)],
        compiler_params=_params(("arbitrary", "arbitrary")),
        name="matmul_glu",
    )(a, w, w)


def _scan_body(qc_ref, vc_ref, zc_ref, ql_ref, vl_ref, zl_ref, lb_ref, o_ref,
               st_ref, flag_ref, b_sc, q_sc, k_sc, v_sc, *, nc, n_sub, heads):
    chunk = SCAN_CHUNK
    direction = pl.program_id(1)
    step = pl.program_id(2)
    fwd = direction == 0

    @pl.when(step == 0)
    def _():
        st_ref[0] = jnp.zeros(st_ref.shape[1:], F32)

    @pl.when(step > 0)
    def _():
        st_ref[0] = st_ref[n_sub]

    lb = lb_ref[0]
    row = lax.broadcasted_iota(I32, (chunk, chunk), 0)
    col = lax.broadcasted_iota(I32, (chunk, chunk), 1)
    sign = 1 - 2 * direction
    tri = (row - col) * sign >= 0
    tri_b = jnp.where(tri, 1.0, 0.0).astype(BF16)
    t_idx = lax.broadcasted_iota(I32, (chunk, 1), 0)

    def chunk_rows(ci):
        cidx = jnp.where(fwd, ci, n_sub - 1 - ci)
        return pl.ds(pl.multiple_of(cidx * chunk, chunk), chunk)

    def decay(z_ref, rows, sl):
        z = z_ref[0, rows, sl].astype(F32)
        f = lb[:, sl] + (1.0 - lb[:, sl]) * jax.nn.sigmoid(z)
        lf = jnp.log(f)
        hi = lf.astype(BF16)
        lo = (lf - hi.astype(F32)).astype(BF16)
        b = _dot(tri_b, hi) + _dot(tri_b, lo)
        tot = jnp.where(fwd, b[chunk - 1:chunk, :], b[0:1, :])
        return b, tot, 1.0 - f

    def fast_chunk(q_ref, v_ref, z_ref, ci, emit):
        rows = chunk_rows(ci)
        b, tot, kk = decay(z_ref, rows, slice(None))
        ke = kk * jnp.exp(tot - b)
        ke_b = ke.astype(BF16)
        etot = jnp.exp(tot)
        v = v_ref[0, rows, :]
        head_sl = [slice(h * HEAD_DIM, (h + 1) * HEAD_DIM) for h in range(heads)]
        st = [st_ref[ci, h] for h in range(heads)]
        kv = [_dot_tn(v[:, sl], ke_b[:, sl]) for sl in head_sl]
        if emit:
            qe = _silu(q_ref[0, rows, :].astype(F32)) * jnp.exp(b)
            qe_b = qe.astype(BF16)
            g = jnp.exp(-0.5 * tot)
            qh = (qe * g).astype(BF16)
            kh = (ke * g).astype(BF16)
            flag_ref[ci] = (jnp.min(tot) < SAFE_CHUNK_LOG_DECAY).astype(I32)
            sc = [_dot_nt(qh[:, sl], kh[:, sl]) for sl in head_sl]
            o_inter = [_dot_nt(qe_b[:, sl], st[h].astype(BF16)) for h, sl in enumerate(head_sl)]
            p = [jnp.where(tri, s, 0.0).astype(BF16) for s in sc]
            o_intra = [_dot(p[h], v[:, sl]) for h, sl in enumerate(head_sl)]
            for h, sl in enumerate(head_sl):
                o_ref[0, 0, rows, sl] = (o_inter[h] + o_intra[h]).astype(o_ref.dtype)
        for h, sl in enumerate(head_sl):
            st_ref[ci + 1, h] = st[h] * etot[:, sl] + kv[h]

    def exact_chunk(q_ref, v_ref, z_ref, ci):
        rows = chunk_rows(ci)
        b, _, kk = decay(z_ref, rows, slice(None))
        b_sc[...] = b
        q_sc[...] = _silu(q_ref[0, rows, :].astype(F32))
        k_sc[...] = kk
        v_sc[...] = v_ref[0, rows, :].astype(F32)

        def head_loop(h, c2):
            lanes = pl.ds(pl.multiple_of(h * HEAD_DIM, HEAD_DIM), HEAD_DIM)
            b_h = b_sc[:, lanes]
            q_h = q_sc[:, lanes]
            k_h = k_sc[:, lanes]
            v_h = v_sc[:, lanes]
            o_inter = _dot_nt((q_h * jnp.exp(b_h)).astype(BF16), st_ref[ci, h].astype(BF16))

            def key_loop(s, acc):
                pick = t_idx == s
                b_s = jnp.sum(jnp.where(pick, b_h, 0.0), axis=0, keepdims=True)
                k_s = jnp.sum(jnp.where(pick, k_h, 0.0), axis=0, keepdims=True)
                v_s = jnp.sum(jnp.where(pick, v_h, 0.0), axis=0, keepdims=True)
                valid = (t_idx - s) * sign >= 0
                dec = jnp.where(valid, jnp.exp(jnp.where(valid, b_h - b_s, 0.0)), 0.0)
                w = jnp.sum(q_h * dec * k_s, axis=1, keepdims=True)
                return acc + w * v_s

            o_intra = lax.fori_loop(0, chunk, key_loop, jnp.zeros((chunk, HEAD_DIM), F32))
            o_ref[0, 0, rows, lanes] = (o_inter + o_intra).astype(o_ref.dtype)
            return c2

        lax.fori_loop(0, heads, head_loop, 0)

    @pl.when(step < nc)
    def _():
        for ci in range(n_sub):
            fast_chunk(qc_ref, vc_ref, zc_ref, ci, False)

    @pl.when(step >= nc)
    def _():
        for ci in range(n_sub):
            fast_chunk(ql_ref, vl_ref, zl_ref, ci, True)

        def redo(ci, c):
            @pl.when(flag_ref[ci] == 1)
            def _():
                exact_chunk(ql_ref, vl_ref, zl_ref, ci)
            return c

        lax.fori_loop(0, n_sub, redo, 0)


def _scan(p_ctx, p_lat, lb):
    bsz, t_lat, d5 = p_lat.shape
    t_ctx = p_ctx.shape[1]
    d = d5 // 5
    heads = d // HEAD_DIM
    tb = _pick(t_ctx, 256)
    assert t_lat % tb == 0 and tb % SCAN_CHUNK == 0
    nc, nl = t_ctx // tb, t_lat // tb
    n_sub = tb // SCAN_CHUNK

    def ctx_blk(dr, s):
        sc = jnp.minimum(s, nc - 1)
        return jnp.where(dr == 0, sc, nc - 1 - sc)

    def lat_blk(dr, s):
        sl = jnp.maximum(s - nc, 0)
        return jnp.where(dr == 0, sl, nl - 1 - sl)

    def cspec(sec_fn):
        return pl.BlockSpec((1, tb, d), lambda b, dr, s: (b, ctx_blk(dr, s), sec_fn(dr)))

    def lspec(sec_fn):
        return pl.BlockSpec((1, tb, d), lambda b, dr, s: (b, lat_blk(dr, s), sec_fn(dr)))

    q_sec = lambda dr: 0
    v_sec = lambda dr: 1
    z_sec = lambda dr: 2 + dr
    return pl.pallas_call(
        functools.partial(_scan_body, nc=nc, n_sub=n_sub, heads=heads),
        grid=(bsz, 2, nc + nl),
        in_specs=[cspec(q_sec), cspec(v_sec), cspec(z_sec),
                  lspec(q_sec), lspec(v_sec), lspec(z_sec),
                  pl.BlockSpec((1, 1, d), lambda b, dr, s: (dr, 0, 0))],
        out_specs=pl.BlockSpec((1, 1, tb, d), lambda b, dr, s: (dr, b, lat_blk(dr, s), 0)),
        out_shape=jax.ShapeDtypeStruct((2, bsz, t_lat, d), BF16),
        scratch_shapes=[pltpu.VMEM((n_sub + 1, heads, HEAD_DIM, HEAD_DIM), F32),
                        pltpu.SMEM((n_sub,), I32),
                        pltpu.VMEM((SCAN_CHUNK, d), F32),
                        pltpu.VMEM((SCAN_CHUNK, d), F32),
                        pltpu.VMEM((SCAN_CHUNK, d), F32),
                        pltpu.VMEM((SCAN_CHUNK, d), F32)],
        compiler_params=_params(("arbitrary", "arbitrary", "arbitrary")),
        name="hgrn_scan",
    )(p_ctx, p_ctx, p_ctx, p_lat, p_lat, p_lat, lb.reshape(2, 1, d))


def _hgrn_readout_body(of_ref, ob_ref, g_ref, on_ref, w_ref, h_ref, gt_ref, o_ref,
                       wb_ref, a_ref, *, heads):
    @pl.when((pl.program_id(0) == 0) & (pl.program_id(1) == 0))
    def _():
        wb_ref[...] = w_ref[...].astype(BF16)

    on = on_ref[...]
    for h in range(heads):
        sl = slice(h * HEAD_DIM, (h + 1) * HEAD_DIM)
        o = of_ref[0, 0, :, sl].astype(F32) + ob_ref[0, 0, :, sl].astype(F32)
        o = _rms(o, on)
        a_ref[:, sl] = (o * _silu(g_ref[0, :, sl].astype(F32))).astype(BF16)
    y = _dot(a_ref[...], wb_ref[...])
    o_ref[0] = h_ref[0] + gt_ref[0] * y


def _hgrn_readout(o2, p_lat, onorm_g, w_out, h, mod3, row_fn, sec_gate):
    bsz, t, d = h.shape
    heads = d // HEAD_DIM
    tm = _pick(t, 256)
    return pl.pallas_call(
        functools.partial(_hgrn_readout_body, heads=heads),
        grid=(bsz, t // tm),
        in_specs=[
            pl.BlockSpec((1, 1, tm, d), lambda b, i: (0, b, i, 0)),
            pl.BlockSpec((1, 1, tm, d), lambda b, i: (1, b, i, 0)),
            pl.BlockSpec((1, tm, d), lambda b, i: (b, i, 4)),
            pl.BlockSpec((1, HEAD_DIM), lambda b, i: (0, 0)),
            pl.BlockSpec((d, d), lambda b, i: (0, 0), pipeline_mode=pl.Buffered(1)),
            pl.BlockSpec((1, tm, d), lambda b, i: (b, i, 0)),
            _mod_spec(d, row_fn, sec_gate),
        ],
        out_specs=pl.BlockSpec((1, tm, d), lambda b, i: (b, i, 0)),
        out_shape=jax.ShapeDtypeStruct((bsz, t, d), F32),
        scratch_shapes=[pltpu.VMEM((d, d), BF16), pltpu.VMEM((tm, d), BF16)],
        compiler_params=_params(("arbitrary", "arbitrary")),
        name="hgrn_readout",
    )(o2, o2, p_lat, onorm_g.reshape(1, HEAD_DIM), w_out, h, mod3)


def _conv_out_body(v_ref, lg_ref, lbias_ref, w_ref, h_ref, gt_ref, o_ref, wb_ref):
    @pl.when((pl.program_id(0) == 0) & (pl.program_id(1) == 0))
    def _():
        wb_ref[...] = w_ref[...].astype(BF16)

    x = v_ref[0].astype(F32)
    mu = jnp.mean(x, axis=-1, keepdims=True)
    xc = x - mu
    y = xc * lax.rsqrt(jnp.mean(xc * xc, axis=-1, keepdims=True) + EPS)
    y = _silu(y * lg_ref[...] + lbias_ref[...])
    o_ref[0] = h_ref[0] + gt_ref[0] * _dot(y.astype(BF16), wb_ref[...])


def _conv_out(v, ln_g, ln_b, w, h, mod3, row_fn, sec_gate):
    bsz, t, d = h.shape
    tm = _pick(t, 256)
    return pl.pallas_call(
        _conv_out_body,
        grid=(bsz, t // tm),
        in_specs=[
            pl.BlockSpec((1, tm, d), lambda b, i: (b, i, 0)),
            pl.BlockSpec((1, d), lambda b, i: (0, 0)),
            pl.BlockSpec((1, d), lambda b, i: (0, 0)),
            pl.BlockSpec((d, d), lambda b, i: (0, 0), pipeline_mode=pl.Buffered(1)),
            pl.BlockSpec((1, tm, d), lambda b, i: (b, i, 0)),
            _mod_spec(d, row_fn, sec_gate),
        ],
        out_specs=pl.BlockSpec((1, tm, d), lambda b, i: (b, i, 0)),
        out_shape=jax.ShapeDtypeStruct((bsz, t, d), F32),
        scratch_shapes=[pltpu.VMEM((d, d), BF16)],
        compiler_params=_params(("arbitrary", "arbitrary")),
        name="conv_out",
    )(v, ln_g.reshape(1, d), ln_b.reshape(1, d), w, h, mod3)


def _dwconv_body(u_ref, w_ref, b_ref, o_ref, pad_ref, *, rows, cb, half_blocks):
    half = CONV_WIDTH // 2
    j = pl.program_id(1)
    n_lane = cb // LANES
    bias = b_ref[...]

    def taps(read, r):
        for lb in range(n_lane):
            ln = slice(lb * LANES, (lb + 1) * LANES)
            acc = jnp.zeros((GRID_W, LANES), F32)
            for k in range(CONV_WIDTH):
                acc = acc + w_ref[k:k + 1, ln] * read(r, k, ln)
            o_ref[0, pl.ds(pl.multiple_of(r * GRID_W, GRID_W), GRID_W), ln] = (
                acc + bias[:, ln]).astype(o_ref.dtype)

    @pl.when(j < half_blocks)
    def _():
        zeros = jnp.zeros((rows, 16, cb), F32)
        pad_ref[0:rows, 0:16, :] = zeros
        pad_ref[0:rows, 16 + GRID_W:32 + GRID_W, :] = zeros
        pad_ref[0:rows, 16:16 + GRID_W, :] = u_ref[0].astype(F32).reshape(rows, GRID_W, cb)

        def row_loop(r, c):
            taps(lambda r_, k, ln: pad_ref[r_, 16 - half + k:16 - half + k + GRID_W, ln], r)
            return c

        lax.fori_loop(0, rows, row_loop, 0)

    @pl.when(j >= half_blocks)
    def _():
        zeros = jnp.zeros((half, GRID_W, cb), F32)
        pad_ref[0:half, 0:GRID_W, :] = zeros
        pad_ref[half + rows:2 * half + rows, 0:GRID_W, :] = zeros
        pad_ref[half:half + rows, 0:GRID_W, :] = u_ref[0].astype(F32).reshape(rows, GRID_W, cb)

        def row_loop(r, c):
            taps(lambda r_, k, ln: pad_ref[r_ + k, 0:GRID_W, ln], r)
            return c

        lax.fori_loop(0, rows, row_loop, 0)


def _dwconv(u, w_dw, b_dw):
    bsz, t, d = u.shape
    rows = t // GRID_W
    cb = _pick(d // 2, 256)
    half_blocks = (d // 2) // cb
    return pl.pallas_call(
        functools.partial(_dwconv_body, rows=rows, cb=cb, half_blocks=half_blocks),
        grid=(bsz, d // cb),
        in_specs=[
            pl.BlockSpec((1, t, cb), lambda b, j: (b, 0, j)),
            pl.BlockSpec((CONV_WIDTH, cb), lambda b, j: (0, j)),
            pl.BlockSpec((1, cb), lambda b, j: (0, j)),
        ],
        out_specs=pl.BlockSpec((1, t, cb), lambda b, j: (b, 0, j)),
        out_shape=jax.ShapeDtypeStruct((bsz, t, d), BF16),
        scratch_shapes=[pltpu.VMEM((rows + 2 * (CONV_WIDTH // 2) + 2, GRID_W + 32, cb), F32)],
        compiler_params=_params(("arbitrary", "arbitrary")),
        name="dwconv",
    )(u, w_dw, b_dw.reshape(1, d))


def _router_body(h_ref, g_ref, sh_ref, sc_ref, wr_ref, br_ref,
                 bp_ref, ri_ref, rw_ref, cnt_ref, carry_ref):
    tm = h_ref.shape[1]

    @pl.when((pl.program_id(0) == 0) & (pl.program_id(1) == 0))
    def _():
        carry_ref[...] = jnp.zeros_like(carry_ref)

    bmod = _rms(h_ref[0], g_ref[...]) * (1.0 + sc_ref[0]) + sh_ref[0]
    _store_token_tiles(bp_ref, _pack_bf16_pairs(bmod))

    b_hi = bmod.astype(BF16)
    b_lo = (bmod - b_hi.astype(F32)).astype(BF16)
    w = wr_ref[...]
    w_hi = w.astype(BF16)
    w_lo = (w - w_hi.astype(F32)).astype(BF16)
    logits = _dot(b_hi, w_hi) + _dot(b_hi, w_lo) + _dot(b_lo, w_hi) + br_ref[...]

    lane_i = lax.broadcasted_iota(I32, (tm, LANES), 1)
    lane = lane_i.astype(F32)
    neg = jnp.float32(-jnp.inf)
    big = jnp.float32(1 << 20)
    is_grp = (lane_i >= MOE_EXPERTS) & (lane_i < MOE_EXPERTS + MOE_GROUPS)
    m1 = jnp.max(jnp.where(is_grp, logits, neg), axis=1, keepdims=True)
    grp = jnp.min(jnp.where(is_grp & (logits == m1), lane - MOE_EXPERTS, big), axis=1, keepdims=True)
    pg = 1.0 / jnp.sum(jnp.where(is_grp, jnp.exp(logits - m1), 0.0), axis=1, keepdims=True)

    sel = (lane >= grp * MOE_PER_GROUP) & (lane < (grp + 1) * MOE_PER_GROUP)
    m2 = jnp.max(jnp.where(sel, logits, neg), axis=1, keepdims=True)
    ea = jnp.min(jnp.where(sel & (logits == m2), lane, big), axis=1, keepdims=True)
    sel2 = sel & (lane != ea)
    m3 = jnp.max(jnp.where(sel2, logits, neg), axis=1, keepdims=True)
    eb = jnp.min(jnp.where(sel2 & (logits == m3), lane, big), axis=1, keepdims=True)
    tr = jnp.exp(m3 - m2)
    wa = pg / (1.0 + tr)
    wb = wa * tr

    hit_a = lane == ea
    hit_b = lane == eb
    onehot = jnp.where(hit_a | hit_b, 1.0, 0.0)
    r_i = lax.broadcasted_iota(I32, (tm, tm), 0)
    c_i = lax.broadcasted_iota(I32, (tm, tm), 1)
    lower = jnp.where(r_i > c_i, 1.0, 0.0).astype(BF16)
    before = _dot(lower, onehot.astype(BF16)) + carry_ref[...]
    rank_a = jnp.sum(jnp.where(hit_a, before, 0.0), axis=1, keepdims=True).astype(I32)
    rank_b = jnp.sum(jnp.where(hit_b, before, 0.0), axis=1, keepdims=True).astype(I32)
    carry = carry_ref[...] + jnp.sum(onehot, axis=0, keepdims=True)
    carry_ref[...] = carry
    cnt_ref[...] = carry

    ri_ref[...] = jnp.where(lane_i == 0, ea.astype(I32), jnp.where(lane_i == 1, eb.astype(I32),
                            jnp.where(lane_i == 2, rank_a, jnp.where(lane_i == 3, rank_b, 0))))
    rw_ref[...] = jnp.where(lane_i == 0, wa, jnp.where(lane_i == 1, wb, 0.0))


def _router(h, g, mod3, row_fn, sec_shift, sec_scale, w_r, b_r):
    bsz, t, d = h.shape
    tm = _pick(t, 256)
    nt = t // tm
    n = bsz * t
    rpt = d // 2 // LANES
    return pl.pallas_call(
        _router_body,
        grid=(bsz, nt),
        in_specs=[
            pl.BlockSpec((1, tm, d), lambda b, i: (b, i, 0)),
            pl.BlockSpec((1, d), lambda b, i: (0, 0)),
            _mod_spec(d, row_fn, sec_shift),
            _mod_spec(d, row_fn, sec_scale),
            pl.BlockSpec((d, LANES), lambda b, i: (0, 0)),
            pl.BlockSpec((1, LANES), lambda b, i: (0, 0)),
        ],
        out_specs=[
            pl.BlockSpec((tm * rpt, LANES), lambda b, i: (b * nt + i, 0)),
            pl.BlockSpec((tm, LANES), lambda b, i: (b * nt + i, 0)),
            pl.BlockSpec((tm, LANES), lambda b, i: (b * nt + i, 0)),
            pl.BlockSpec((1, LANES), lambda b, i: (0, 0)),
        ],
        out_shape=[
            jax.ShapeDtypeStruct((n * rpt, LANES), U32),
            jax.ShapeDtypeStruct((n, LANES), I32),
            jax.ShapeDtypeStruct((n, LANES), F32),
            jax.ShapeDtypeStruct((1, LANES), F32),
        ],
        scratch_shapes=[pltpu.VMEM((1, LANES), F32)],
        compiler_params=_params(("arbitrary", "arbitrary")),
        name="router",
    )(h, g.reshape(1, d), mod3, mod3, w_r, b_r)


def _expert_body(meta_ref, code_ref, bp_ref, wg_ref, wu_ref, wd_ref, y_ref,
                 wgb_ref, wub_ref, wdb_ref, xbuf, ybuf, xsem, ysem, *, n_tok, rpt):
    e = pl.program_id(0)
    blk0 = meta_ref[0, e]
    n_blk = meta_ref[0, e + 1] - blk0
    n_used = meta_ref[2, 0]
    dh = rpt * LANES
    tile_rows = MOE_BLOCK * rpt

    def tile(ref, i):
        return ref.at[pl.ds(pl.multiple_of(i * rpt, rpt), rpt), :]

    def gather(g, slot):
        def body(r, c):
            code = code_ref[g * MOE_BLOCK + r]
            tok = jnp.where(code < 2 * n_tok, code >> 1, 0)
            pltpu.make_async_copy(tile(bp_ref, tok), tile(xbuf.at[slot], r),
                                  xsem.at[slot]).start(priority=1)
            return c

        lax.fori_loop(0, MOE_BLOCK, body, 0, unroll=8)

    def write_out(g):
        return pltpu.make_async_copy(
            ybuf, y_ref.at[pl.ds(pl.multiple_of(g * tile_rows, tile_rows), tile_rows), :], ysem)

    def wait_gather(slot):
        pltpu.make_async_copy(bp_ref.at[pl.ds(0, tile_rows), :], xbuf.at[slot], xsem.at[slot]).wait()

    @pl.when(e == 0)
    def _():
        gather(0, 0)

    @pl.when(n_blk > 0)
    def _():
        wgb_ref[...] = wg_ref[0, 0].astype(BF16)
        wub_ref[...] = wu_ref[0, 0].astype(BF16)
        wdb_ref[...] = wd_ref[0, 0].astype(BF16)

        def block(j, c):
            g = blk0 + j
            slot = g % 2

            @pl.when(g + 1 < n_used)
            def _():
                gather(g + 1, 1 - slot)

            wait_gather(slot)
            x_lo, x_hi = _unpack_bf16_pairs(_load_token_tiles(xbuf.at[slot], MOE_BLOCK, rpt))
            x_lo = x_lo.astype(BF16)
            x_hi = x_hi.astype(BF16)
            gate = _dot(x_lo, wgb_ref[0:dh, :]) + _dot(x_hi, wgb_ref[dh:2 * dh, :])
            up = _dot(x_lo, wub_ref[0:dh, :]) + _dot(x_hi, wub_ref[dh:2 * dh, :])
            mid = (_silu(gate) * up).astype(BF16)
            out = _pack_bf16_pairs(_dot(mid, wdb_ref[...]))

            @pl.when(g > 0)
            def _():
                write_out(g - 1).wait()

            _store_token_tiles(ybuf, out)
            write_out(g).start()
            return c

        lax.fori_loop(0, n_blk, block, 0)

    @pl.when(e == pl.num_programs(0) - 1)
    def _():
        write_out(n_used - 1).wait()


def _experts(meta, code, bp, w_gate, w_up, w_down, layer, n_tok):
    n_slots = code.shape[0]
    rpt = bp.shape[0] // n_tok
    _, n_exp, d, hid = w_gate.shape

    def wspec(shape):
        return pl.BlockSpec((1, 1) + shape, lambda e, m, c: (layer, e, 0, 0))

    return pl.pallas_call(
        functools.partial(_expert_body, n_tok=n_tok, rpt=rpt),
        grid_spec=pltpu.PrefetchScalarGridSpec(
            num_scalar_prefetch=2,
            grid=(n_exp,),
            in_specs=[pl.BlockSpec(memory_space=pl.ANY),
                      wspec((d, hid)), wspec((d, hid)), wspec((hid, d))],
            out_specs=pl.BlockSpec(memory_space=pl.ANY),
            scratch_shapes=[pltpu.VMEM((d, hid), BF16), pltpu.VMEM((d, hid), BF16),
                            pltpu.VMEM((hid, d), BF16),
                            pltpu.VMEM((2, MOE_BLOCK * rpt, LANES), U32),
                            pltpu.VMEM((MOE_BLOCK * rpt, LANES), U32),
                            pltpu.SemaphoreType.DMA((2,)), pltpu.SemaphoreType.DMA],
        ),
        out_shape=jax.ShapeDtypeStruct((n_slots * rpt, LANES), U32),
        compiler_params=_params(("arbitrary",)),
        name="moe_experts",
    )(meta, code, bp, w_gate, w_up, w_down)


def _sc_gather_rows(table, idx):
    n_idx = idx.shape[0]
    win = _pick(n_idx, SC_GATHER_WINDOW)
    mesh = plsc.VectorSubcoreMesh(core_axis_name="core", subcore_axis_name="subcore")

    @pl.kernel(out_type=jax.ShapeDtypeStruct((n_idx, LANES), table.dtype), mesh=mesh)
    def gather_kernel(t_hbm, i_hbm, o_hbm):
        def window(i_vmem, o_vmem):
            pltpu.sync_copy(t_hbm.at[i_vmem.at[0]], o_vmem)

        pltpu.emit_pipeline(
            window,
            grid=(n_idx // win,),
            in_specs=[pl.BlockSpec((1, win), index_map=lambda i: (0, i))],
            out_specs=[pl.BlockSpec((win, LANES), index_map=lambda i: (i, 0))],
            core_axis_name=("core", "subcore"),
            dimension_semantics=(pltpu.PARALLEL,),
        )(i_hbm, o_hbm)

    return gather_kernel(table, idx.reshape(1, n_idx))


def _combine_body(h_ref, y_ref, rw_ref, gt_ref, fg_ref, o_ref, *, rpt, final):
    tm = h_ref.shape[1]
    dh = rpt * LANES
    wa = rw_ref[:, 0:1]
    wb = rw_ref[:, 1:2]
    a_lo, a_hi = _unpack_bf16_pairs(_load_token_tiles(y_ref, tm, rpt, 0, 2 * rpt))
    b_lo, b_hi = _unpack_bf16_pairs(_load_token_tiles(y_ref, tm, rpt, rpt, 2 * rpt))
    gt = gt_ref[0]
    lo = h_ref[0, :, 0:dh] + gt[:, 0:dh] * (wa * a_lo + wb * b_lo)
    hi = h_ref[0, :, dh:2 * dh] + gt[:, dh:2 * dh] * (wa * a_hi + wb * b_hi)
    if final:
        ms = (jnp.sum(lo * lo, axis=-1, keepdims=True)
              + jnp.sum(hi * hi, axis=-1, keepdims=True)) / (2 * dh)
        inv = lax.rsqrt(ms + EPS)
        lo = lo * inv * fg_ref[:, 0:dh]
        hi = hi * inv * fg_ref[:, dh:2 * dh]
    o_ref[0, :, 0:dh] = lo
    o_ref[0, :, dh:2 * dh] = hi


def _combine(h, y, rw, mod3, row_fn, sec_gate, final_g, final):
    bsz, t, d = h.shape
    tm = _pick(t, 256)
    nt = t // tm
    rpt = d // 2 // LANES
    return pl.pallas_call(
        functools.partial(_combine_body, rpt=rpt, final=final),
        grid=(bsz, nt),
        in_specs=[
            pl.BlockSpec((1, tm, d), lambda b, i: (b, i, 0)),
            pl.BlockSpec((tm * 2 * rpt, LANES), lambda b, i: (b * nt + i, 0)),
            pl.BlockSpec((tm, LANES), lambda b, i: (b * nt + i, 0)),
            _mod_spec(d, row_fn, sec_gate),
            pl.BlockSpec((1, d), lambda b, i: (0, 0)),
        ],
        out_specs=pl.BlockSpec((1, tm, d), lambda b, i: (b, i, 0)),
        out_shape=jax.ShapeDtypeStruct((bsz, t, d), F32),
        compiler_params=_params(("arbitrary", "arbitrary")),
        name="moe_combine",
    )(h, y, rw, mod3, final_g.reshape(1, d))


def _hier_moe(h, g, mod3, row_fn, w_r1, b_r1, w_r2, b_r2, w_gate, w_up, w_down, layer, final_g, final):
    bsz, t, d = h.shape
    n = bsz * t
    w_r = jnp.concatenate([jnp.transpose(w_r2, (1, 0, 2)).reshape(d, MOE_EXPERTS), w_r1], axis=1)
    w_r = jnp.pad(w_r, ((0, 0), (0, LANES - w_r.shape[1])))
    b_r = jnp.pad(jnp.concatenate([b_r2.reshape(-1), b_r1]), (0, LANES - MOE_EXPERTS - MOE_GROUPS))
    bp, ri, rw, cnt = _router(h, g, mod3, row_fn, 3, 4, w_r, b_r.reshape(1, LANES))

    counts = cnt[0, :MOE_EXPERTS].astype(I32)
    padded = (counts + MOE_BLOCK - 1) // MOE_BLOCK * MOE_BLOCK
    pad_end = jnp.cumsum(padded)
    pad_start = pad_end - padded
    n_slots = (-(-(n * 2) // MOE_BLOCK) + MOE_EXPERTS) * MOE_BLOCK
    blk_off = jnp.concatenate([pad_start, pad_end[-1:]]) // MOE_BLOCK
    row = lambda v: jnp.pad(v.astype(I32), (0, LANES - v.shape[0]))
    meta = jnp.stack([row(blk_off), row(counts), row(blk_off[-1:])])
    hit = ri[:, 0:2, None] == jnp.arange(MOE_EXPERTS, dtype=I32)
    pos = (jnp.sum(jnp.where(hit, pad_start, 0), axis=-1) + ri[:, 2:4]).reshape(-1)
    code = (2 * n + jnp.arange(n_slots, dtype=I32)).at[pos].set(jnp.arange(2 * n, dtype=I32))

    ys = _experts(meta, code, bp, w_gate, w_up, w_down, layer, n)
    rpt = d // 2 // LANES
    y = _sc_gather_rows(ys, (pos[:, None] * rpt + jnp.arange(rpt, dtype=I32)).reshape(-1))
    return _combine(h, y, rw, mod3, row_fn, 5, final_g, final)


def kernel(x, c, ctx, c_ctx, ada_w, ada_b, norm1_g, norm2_g, hgrn_w_in, hgrn_lb, hgrn_onorm_g,
           hgrn_w_out, conv_w_pw1, conv_w_dw, conv_b_dw, conv_ln_g, conv_ln_b, conv_w_pw2,
           moe_w_r1, moe_b_r1, moe_w_r2, moe_b_r2, moe_w_gate, moe_w_up, moe_w_down, final_g):
    bsz, t, d = x.shape
    t_ctx = ctx.shape[1]
    depth = ada_w.shape[0]
    assert depth == 2 and bsz < MOD_ROWS and d % (2 * LANES) == 0 and t % GRID_W == 0

    c_all = jnp.concatenate([c, c_ctx[None, :], jnp.zeros((MOD_ROWS - bsz - 1, d), F32)], axis=0)
    mod = _adaln(c_all, ada_w, ada_b)
    mod3 = mod.reshape(depth * MOD_ROWS, 1, 6 * d)
    lat_row = lambda l: (lambda b: l * MOD_ROWS + b)
    ctx_row = lambda l: (lambda b: l * MOD_ROWS + bsz)
    lb_all = jnp.cumsum(jax.nn.softmax(hgrn_lb.astype(F32), axis=1), axis=1)

    def moe(h, l, final):
        return _hier_moe(h, norm2_g[l], mod3, lat_row(l), moe_w_r1[l], moe_b_r1[l], moe_w_r2[l],
                         moe_b_r2[l], moe_w_gate, moe_w_up, moe_w_down, l, final_g, final)

    a_lat = _norm_mod(x, norm1_g[0], mod3, lat_row(0), 0, 1)
    a_ctx = _norm_mod(ctx, norm1_g[0], mod3, ctx_row(0), 0, 1)
    p_lat = _matmul(a_lat.reshape(bsz * t, d), hgrn_w_in[0]).reshape(bsz, t, 5 * d)
    p_ctx = _matmul(a_ctx.reshape(bsz * t_ctx, d), hgrn_w_in[0]).reshape(bsz, t_ctx, 5 * d)
    o2 = _scan(p_ctx, p_lat, lb_all[:, 0])
    h = _hgrn_readout(o2, p_lat, hgrn_onorm_g[0], hgrn_w_out[0], x, mod3, lat_row(0), 2)
    h = moe(h, 0, False)

    a = _norm_mod(h, norm1_g[1], mod3, lat_row(1), 0, 1)
    u = _matmul_glu(a.reshape(bsz * t, d), conv_w_pw1[0]).reshape(bsz, t, d)
    v = _dwconv(u, conv_w_dw[0], conv_b_dw[0])
    h = _conv_out(v, conv_ln_g[0], conv_ln_b[0], conv_w_pw2[0], h, mod3, lat_row(1), 2)
    return moe(h, 1, True)
```

```python
import functools

import jax
import jax.numpy as jnp
from jax import lax
from jax.experimental import pallas as pl
from jax.experimental.pallas import tpu as pltpu
from jax.experimental.pallas import tpu_sc as plsc

F32 = jnp.float32
BF16 = jnp.bfloat16
U32 = jnp.uint32
I32 = jnp.int32

EPS = 1e-6
LANES = 128
HEAD_DIM = 128
SCAN_CHUNK = 64
GRID_W = 64
CONV_WIDTH = 31
MOE_GROUPS = 8
MOE_PER_GROUP = 8
MOE_EXPERTS = MOE_GROUPS * MOE_PER_GROUP
MOE_BLOCK = 256
SC_GATHER_WINDOW = 256
MOD_ROWS = 16
VMEM_LIMIT = 56 * 1024 * 1024
SAFE_CHUNK_LOG_DECAY = -80.0
HI_MASK = 0xFFFF0000


def _pick(n, pref):
    t = pref
    while n % t:
        t //= 2
    return t


def _params(sem):
    return pltpu.CompilerParams(dimension_semantics=sem, vmem_limit_bytes=VMEM_LIMIT)


def _silu(x):
    return x * jax.nn.sigmoid(x)


def _dot(a, b):
    return jnp.dot(a, b, preferred_element_type=F32)


def _dot_nt(a, b):
    return lax.dot_general(a, b, (((1,), (1,)), ((), ())), preferred_element_type=F32)


def _dot_tn(a, b):
    return lax.dot_general(a, b, (((0,), (0,)), ((), ())), preferred_element_type=F32)


def _pack_bf16_pairs(x):
    n = x.shape[1] // 2
    bits = lax.bitcast_convert_type(x.astype(BF16).astype(F32), U32)
    return (bits[:, :n] >> 16) | (bits[:, n:] & jnp.uint32(HI_MASK))


def _unpack_bf16_pairs(w):
    lo = lax.bitcast_convert_type(w << 16, F32)
    hi = lax.bitcast_convert_type(w & jnp.uint32(HI_MASK), F32)
    return lo, hi


def _store_token_tiles(ref, w, first=0, stride=None):
    m, n = w.shape[0], w.shape[1] // LANES
    stride = n if stride is None else stride
    for s in range(n):
        ref[pl.ds(first + s, m, stride=stride), :] = w[:, s * LANES:(s + 1) * LANES]


def _load_token_tiles(ref, m, n, first=0, stride=None):
    stride = n if stride is None else stride
    return jnp.concatenate([ref[pl.ds(first + s, m, stride=stride), :] for s in range(n)], axis=1)


def _adaln_body(c_ref, w_ref, b_ref, o_ref):
    s = _silu(c_ref[...])
    o_ref[0] = _dot(s.astype(BF16), w_ref[0].astype(BF16)) + b_ref[0]


def _adaln(c_all, ada_w, ada_b):
    depth, d, n6 = ada_w.shape
    tn = _pick(n6, 1024)
    return pl.pallas_call(
        _adaln_body,
        grid=(depth, n6 // tn),
        in_specs=[
            pl.BlockSpec((MOD_ROWS, d), lambda l, j: (0, 0)),
            pl.BlockSpec((1, d, tn), lambda l, j: (l, 0, j)),
            pl.BlockSpec((1, 1, tn), lambda l, j: (l, 0, j)),
        ],
        out_specs=pl.BlockSpec((1, MOD_ROWS, tn), lambda l, j: (l, 0, j)),
        out_shape=jax.ShapeDtypeStruct((depth, MOD_ROWS, n6), F32),
        compiler_params=_params(("arbitrary", "arbitrary")),
        name="adaln",
    )(c_all, ada_w, ada_b.reshape(depth, 1, n6))


def _mod_spec(d, row_fn, sec):
    return pl.BlockSpec((1, 1, d), lambda b, i: (row_fn(b), 0, sec))


def _rms(x, g):
    return x * lax.rsqrt(jnp.mean(x * x, axis=-1, keepdims=True) + EPS) * g


def _norm_mod_body(x_ref, g_ref, sh_ref, sc_ref, o_ref):
    y = _rms(x_ref[0], g_ref[...])
    o_ref[0] = (y * (1.0 + sc_ref[0]) + sh_ref[0]).astype(o_ref.dtype)


def _norm_mod(x, g, mod3, row_fn, sec_shift, sec_scale):
    bsz, t, d = x.shape
    tm = _pick(t, 512)
    return pl.pallas_call(
        _norm_mod_body,
        grid=(bsz, t // tm),
        in_specs=[
            pl.BlockSpec((1, tm, d), lambda b, i: (b, i, 0)),
            pl.BlockSpec((1, d), lambda b, i: (0, 0)),
            _mod_spec(d, row_fn, sec_shift),
            _mod_spec(d, row_fn, sec_scale),
        ],
        out_specs=pl.BlockSpec((1, tm, d), lambda b, i: (b, i, 0)),
        out_shape=jax.ShapeDtypeStruct((bsz, t, d), BF16),
        compiler_params=_params(("arbitrary", "arbitrary")),
        name="norm_mod",
    )(x, g.reshape(1, d), mod3, mod3)


def _mm_body(a_ref, w_ref, o_ref, wb_ref):
    @pl.when(pl.program_id(1) == 0)
    def _():
        wb_ref[...] = w_ref[...].astype(BF16)

    o_ref[...] = _dot(a_ref[...], wb_ref[...]).astype(o_ref.dtype)


def _matmul(a, w):
    m, k = a.shape
    n = w.shape[1]
    tm, tn = _pick(m, 512), _pick(n, 1024)
    return pl.pallas_call(
        _mm_body,
        grid=(n // tn, m // tm),
        in_specs=[
            pl.BlockSpec((tm, k), lambda j, i: (i, 0)),
            pl.BlockSpec((k, tn), lambda j, i: (0, j)),
        ],
        out_specs=pl.BlockSpec((tm, tn), lambda j, i: (i, j)),
        out_shape=jax.ShapeDtypeStruct((m, n), BF16),
        scratch_shapes=[pltpu.VMEM((k, tn), BF16)],
        compiler_params=_params(("arbitrary", "arbitrary")),
        name="matmul",
    )(a, w)


def _mm_glu_body(a_ref, wv_ref, wg_ref, o_ref, wvb_ref, wgb_ref):
    @pl.when(pl.program_id(1) == 0)
    def _():
        wvb_ref[...] = wv_ref[...].astype(BF16)
        wgb_ref[...] = wg_ref[...].astype(BF16)

    a = a_ref[...]
    val = _dot(a, wvb_ref[...])
    gate = _dot(a, wgb_ref[...])
    o_ref[...] = (val * jax.nn.sigmoid(gate)).astype(o_ref.dtype)


def _matmul_glu(a, w):
    m, k = a.shape
    n = w.shape[1] // 2
    tm, tn = _pick(m, 512), _pick(n, 512)
    nj = n // tn
    return pl.pallas_call(
        _mm_glu_body,
        grid=(nj, m // tm),
        in_specs=[
            pl.BlockSpec((tm, k), lambda j, i: (i, 0)),
            pl.BlockSpec((k, tn), lambda j, i: (0, j)),
            pl.BlockSpec((k, tn), lambda j, i: (0, j + nj)),
        ],
        out_specs=pl.BlockSpec((tm, tn), lambda j, i: (i, j)),
        out_shape=jax.ShapeDtypeStruct((m, n), BF16),
        scratch_shapes=[pltpu.VMEM((k, tn), BF16), pltpu.VMEM((k, tn), BF16)],
        compiler_params=_params(("arbitrary", "arbitrary")),
        name="matmul_glu",
    )(a, w, w)


def _scan_body(qc_ref, vc_ref, zc_ref, ql_ref, vl_ref, zl_ref, lb_ref, o_ref,
               st_ref, flag_ref, b_sc, q_sc, k_sc, v_sc, *, nc, n_sub, heads):
    chunk = SCAN_CHUNK
    direction = pl.program_id(1)
    step = pl.program_id(2)
    fwd = direction == 0

    @pl.when(step == 0)
    def _():
        st_ref[0] = jnp.zeros(st_ref.shape[1:], F32)

    @pl.when(step > 0)
    def _():
        st_ref[0] = st_ref[n_sub]

    lb = lb_ref[0]
    row = lax.broadcasted_iota(I32, (chunk, chunk), 0)
    col = lax.broadcasted_iota(I32, (chunk, chunk), 1)
    sign = 1 - 2 * direction
    tri = (row - col) * sign >= 0
    tri_b = jnp.where(tri, 1.0, 0.0).astype(BF16)
    t_idx = lax.broadcasted_iota(I32, (chunk, 1), 0)

    def chunk_rows(ci):
        cidx = jnp.where(fwd, ci, n_sub - 1 - ci)
        return pl.ds(pl.multiple_of(cidx * chunk, chunk), chunk)

    def decay(z_ref, rows, sl):
        z = z_ref[0, rows, sl].astype(F32)
        f = lb[:, sl] + (1.0 - lb[:, sl]) * jax.nn.sigmoid(z)
        lf = jnp.log(f)
        hi = lf.astype(BF16)
        lo = (lf - hi.astype(F32)).astype(BF16)
        b = _dot(tri_b, hi) + _dot(tri_b, lo)
        tot = jnp.where(fwd, b[chunk - 1:chunk, :], b[0:1, :])
        return b, tot, 1.0 - f

    def fast_chunk(q_ref, v_ref, z_ref, ci, emit):
        rows = chunk_rows(ci)
        b, tot, kk = decay(z_ref, rows, slice(None))
        ke = kk * jnp.exp(tot - b)
        ke_b = ke.astype(BF16)
        etot = jnp.exp(tot)
        v = v_ref[0, rows, :]
        head_sl = [slice(h * HEAD_DIM, (h + 1) * HEAD_DIM) for h in range(heads)]
        st = [st_ref[ci, h] for h in range(heads)]
        kv = [_dot_tn(v[:, sl], ke_b[:, sl]) for sl in head_sl]
        if emit:
            qe = _silu(q_ref[0, rows, :].astype(F32)) * jnp.exp(b)
            qe_b = qe.astype(BF16)
            g = jnp.exp(-0.5 * tot)
            qh = (qe * g).astype(BF16)
            kh = (ke * g).astype(BF16)
            flag_ref[ci] = (jnp.min(tot) < SAFE_CHUNK_LOG_DECAY).astype(I32)
            sc = [_dot_nt(qh[:, sl], kh[:, sl]) for sl in head_sl]
            o_inter = [_dot_nt(qe_b[:, sl], st[h].astype(BF16)) for h, sl in enumerate(head_sl)]
            p = [jnp.where(tri, s, 0.0).astype(BF16) for s in sc]
            o_intra = [_dot(p[h], v[:, sl]) for h, sl in enumerate(head_sl)]
            for h, sl in enumerate(head_sl):
                o_ref[0, 0, rows, sl] = (o_inter[h] + o_intra[h]).astype(o_ref.dtype)
        for h, sl in enumerate(head_sl):
            st_ref[ci + 1, h] = st[h] * etot[:, sl] + kv[h]

    def exact_chunk(q_ref, v_ref, z_ref, ci):
        rows = chunk_rows(ci)
        b, _, kk = decay(z_ref, rows, slice(None))
        b_sc[...] = b
        q_sc[...] = _silu(q_ref[0, rows, :].astype(F32))
        k_sc[...] = kk
        v_sc[...] = v_ref[0, rows, :].astype(F32)

        def head_loop(h, c2):
            lanes = pl.ds(pl.multiple_of(h * HEAD_DIM, HEAD_DIM), HEAD_DIM)
            b_h = b_sc[:, lanes]
            q_h = q_sc[:, lanes]
            k_h = k_sc[:, lanes]
            v_h = v_sc[:, lanes]
            o_inter = _dot_nt((q_h * jnp.exp(b_h)).astype(BF16), st_ref[ci, h].astype(BF16))

            def key_loop(s, acc):
                pick = t_idx == s
                b_s = jnp.sum(jnp.where(pick, b_h, 0.0), axis=0, keepdims=True)
                k_s = jnp.sum(jnp.where(pick, k_h, 0.0), axis=0, keepdims=True)
                v_s = jnp.sum(jnp.where(pick, v_h, 0.0), axis=0, keepdims=True)
                valid = (t_idx - s) * sign >= 0
                dec = jnp.where(valid, jnp.exp(jnp.where(valid, b_h - b_s, 0.0)), 0.0)
                w = jnp.sum(q_h * dec * k_s, axis=1, keepdims=True)
                return acc + w * v_s

            o_intra = lax.fori_loop(0, chunk, key_loop, jnp.zeros((chunk, HEAD_DIM), F32))
            o_ref[0, 0, rows, lanes] = (o_inter + o_intra).astype(o_ref.dtype)
            return c2

        lax.fori_loop(0, heads, head_loop, 0)

    @pl.when(step < nc)
    def _():
        for ci in range(n_sub):
            fast_chunk(qc_ref, vc_ref, zc_ref, ci, False)

    @pl.when(step >= nc)
    def _():
        for ci in range(n_sub):
            fast_chunk(ql_ref, vl_ref, zl_ref, ci, True)

        def redo(ci, c):
            @pl.when(flag_ref[ci] == 1)
            def _():
                exact_chunk(ql_ref, vl_ref, zl_ref, ci)
            return c

        lax.fori_loop(0, n_sub, redo, 0)


def _scan(p_ctx, p_lat, lb):
    bsz, t_lat, d5 = p_lat.shape
    t_ctx = p_ctx.shape[1]
    d = d5 // 5
    heads = d // HEAD_DIM
    tb = _pick(t_ctx, 256)
    assert t_lat % tb == 0 and tb % SCAN_CHUNK == 0
    nc, nl = t_ctx // tb, t_lat // tb
    n_sub = tb // SCAN_CHUNK

    def ctx_blk(dr, s):
        sc = jnp.minimum(s, nc - 1)
        return jnp.where(dr == 0, sc, nc - 1 - sc)

    def lat_blk(dr, s):
        sl = jnp.maximum(s - nc, 0)
        return jnp.where(dr == 0, sl, nl - 1 - sl)

    def cspec(sec_fn):
        return pl.BlockSpec((1, tb, d), lambda b, dr, s: (b, ctx_blk(dr, s), sec_fn(dr)))

    def lspec(sec_fn):
        return pl.BlockSpec((1, tb, d), lambda b, dr, s: (b, lat_blk(dr, s), sec_fn(dr)))

    q_sec = lambda dr: 0
    v_sec = lambda dr: 1
    z_sec = lambda dr: 2 + dr
    return pl.pallas_call(
        functools.partial(_scan_body, nc=nc, n_sub=n_sub, heads=heads),
        grid=(bsz, 2, nc + nl),
        in_specs=[cspec(q_sec), cspec(v_sec), cspec(z_sec),
                  lspec(q_sec), lspec(v_sec), lspec(z_sec),
                  pl.BlockSpec((1, 1, d), lambda b, dr, s: (dr, 0, 0))],
        out_specs=pl.BlockSpec((1, 1, tb, d), lambda b, dr, s: (dr, b, lat_blk(dr, s), 0)),
        out_shape=jax.ShapeDtypeStruct((2, bsz, t_lat, d), BF16),
        scratch_shapes=[pltpu.VMEM((n_sub + 1, heads, HEAD_DIM, HEAD_DIM), F32),
                        pltpu.SMEM((n_sub,), I32),
                        pltpu.VMEM((SCAN_CHUNK, d), F32),
                        pltpu.VMEM((SCAN_CHUNK, d), F32),
                        pltpu.VMEM((SCAN_CHUNK, d), F32),
                        pltpu.VMEM((SCAN_CHUNK, d), F32)],
        compiler_params=_params(("arbitrary", "arbitrary", "arbitrary")),
        name="hgrn_scan",
    )(p_ctx, p_ctx, p_ctx, p_lat, p_lat, p_lat, lb.reshape(2, 1, d))


def _hgrn_readout_body(of_ref, ob_ref, g_ref, on_ref, w_ref, h_ref, gt_ref, o_ref,
                       wb_ref, a_ref, *, heads):
    @pl.when((pl.program_id(0) == 0) & (pl.program_id(1) == 0))
    def _():
        wb_ref[...] = w_ref[...].astype(BF16)

    on = on_ref[...]
    for h in range(heads):
        sl = slice(h * HEAD_DIM, (h + 1) * HEAD_DIM)
        o = of_ref[0, 0, :, sl].astype(F32) + ob_ref[0, 0, :, sl].astype(F32)
        o = _rms(o, on)
        a_ref[:, sl] = (o * _silu(g_ref[0, :, sl].astype(F32))).astype(BF16)
    y = _dot(a_ref[...], wb_ref[...])
    o_ref[0] = h_ref[0] + gt_ref[0] * y


def _hgrn_readout(o2, p_lat, onorm_g, w_out, h, mod3, row_fn, sec_gate):
    bsz, t, d = h.shape
    heads = d // HEAD_DIM
    tm = _pick(t, 256)
    return pl.pallas_call(
        functools.partial(_hgrn_readout_body, heads=heads),
        grid=(bsz, t // tm),
        in_specs=[
            pl.BlockSpec((1, 1, tm, d), lambda b, i: (0, b, i, 0)),
            pl.BlockSpec((1, 1, tm, d), lambda b, i: (1, b, i, 0)),
            pl.BlockSpec((1, tm, d), lambda b, i: (b, i, 4)),
            pl.BlockSpec((1, HEAD_DIM), lambda b, i: (0, 0)),
            pl.BlockSpec((d, d), lambda b, i: (0, 0), pipeline_mode=pl.Buffered(1)),
            pl.BlockSpec((1, tm, d), lambda b, i: (b, i, 0)),
            _mod_spec(d, row_fn, sec_gate),
        ],
        out_specs=pl.BlockSpec((1, tm, d), lambda b, i: (b, i, 0)),
        out_shape=jax.ShapeDtypeStruct((bsz, t, d), F32),
        scratch_shapes=[pltpu.VMEM((d, d), BF16), pltpu.VMEM((tm, d), BF16)],
        compiler_params=_params(("arbitrary", "arbitrary")),
        name="hgrn_readout",
    )(o2, o2, p_lat, onorm_g.reshape(1, HEAD_DIM), w_out, h, mod3)


def _conv_out_body(v_ref, lg_ref, lbias_ref, w_ref, h_ref, gt_ref, o_ref, wb_ref):
    @pl.when((pl.program_id(0) == 0) & (pl.program_id(1) == 0))
    def _():
        wb_ref[...] = w_ref[...].astype(BF16)

    x = v_ref[0].astype(F32)
    mu = jnp.mean(x, axis=-1, keepdims=True)
    xc = x - mu
    y = xc * lax.rsqrt(jnp.mean(xc * xc, axis=-1, keepdims=True) + EPS)
    y = _silu(y * lg_ref[...] + lbias_ref[...])
    o_ref[0] = h_ref[0] + gt_ref[0] * _dot(y.astype(BF16), wb_ref[...])


def _conv_out(v, ln_g, ln_b, w, h, mod3, row_fn, sec_gate):
    bsz, t, d = h.shape
    tm = _pick(t, 256)
    return pl.pallas_call(
        _conv_out_body,
        grid=(bsz, t // tm),
        in_specs=[
            pl.BlockSpec((1, tm, d), lambda b, i: (b, i, 0)),
            pl.BlockSpec((1, d), lambda b, i: (0, 0)),
            pl.BlockSpec((1, d), lambda b, i: (0, 0)),
            pl.BlockSpec((d, d), lambda b, i: (0, 0), pipeline_mode=pl.Buffered(1)),
            pl.BlockSpec((1, tm, d), lambda b, i: (b, i, 0)),
            _mod_spec(d, row_fn, sec_gate),
        ],
        out_specs=pl.BlockSpec((1, tm, d), lambda b, i: (b, i, 0)),
        out_shape=jax.ShapeDtypeStruct((bsz, t, d), F32),
        scratch_shapes=[pltpu.VMEM((d, d), BF16)],
        compiler_params=_params(("arbitrary", "arbitrary")),
        name="conv_out",
    )(v, ln_g.reshape(1, d), ln_b.reshape(1, d), w, h, mod3)


def _dwconv_body(u_ref, w_ref, b_ref, o_ref, pad_ref, *, rows, cb, half_blocks):
    half = CONV_WIDTH // 2
    j = pl.program_id(1)
    n_lane = cb // LANES
    bias = b_ref[...]

    def taps(read, r):
        for lb in range(n_lane):
            ln = slice(lb * LANES, (lb + 1) * LANES)
            acc = jnp.zeros((GRID_W, LANES), F32)
            for k in range(CONV_WIDTH):
                acc = acc + w_ref[k:k + 1, ln] * read(r, k, ln)
            o_ref[0, pl.ds(pl.multiple_of(r * GRID_W, GRID_W), GRID_W), ln] = (
                acc + bias[:, ln]).astype(o_ref.dtype)

    @pl.when(j < half_blocks)
    def _():
        zeros = jnp.zeros((rows, 16, cb), F32)
        pad_ref[0:rows, 0:16, :] = zeros
        pad_ref[0:rows, 16 + GRID_W:32 + GRID_W, :] = zeros
        pad_ref[0:rows, 16:16 + GRID_W, :] = u_ref[0].astype(F32).reshape(rows, GRID_W, cb)

        def row_loop(r, c):
            taps(lambda r_, k, ln: pad_ref[r_, 16 - half + k:16 - half + k + GRID_W, ln], r)
            return c

        lax.fori_loop(0, rows, row_loop, 0)

    @pl.when(j >= half_blocks)
    def _():
        zeros = jnp.zeros((half, GRID_W, cb), F32)
        pad_ref[0:half, 0:GRID_W, :] = zeros
        pad_ref[half + rows:2 * half + rows, 0:GRID_W, :] = zeros
        pad_ref[half:half + rows, 0:GRID_W, :] = u_ref[0].astype(F32).reshape(rows, GRID_W, cb)

        def row_loop(r, c):
            taps(lambda r_, k, ln: pad_ref[r_ + k, 0:GRID_W, ln], r)
            return c

        lax.fori_loop(0, rows, row_loop, 0)


def _dwconv(u, w_dw, b_dw):
    bsz, t, d = u.shape
    rows = t // GRID_W
    cb = _pick(d // 2, 256)
    half_blocks = (d // 2) // cb
    return pl.pallas_call(
        functools.partial(_dwconv_body, rows=rows, cb=cb, half_blocks=half_blocks),
        grid=(bsz, d // cb),
        in_specs=[
            pl.BlockSpec((1, t, cb), lambda b, j: (b, 0, j)),
            pl.BlockSpec((CONV_WIDTH, cb), lambda b, j: (0, j)),
            pl.BlockSpec((1, cb), lambda b, j: (0, j)),
        ],
        out_specs=pl.BlockSpec((1, t, cb), lambda b, j: (b, 0, j)),
        out_shape=jax.ShapeDtypeStruct((bsz, t, d), BF16),
        scratch_shapes=[pltpu.VMEM((rows + 2 * (CONV_WIDTH // 2) + 2, GRID_W + 32, cb), F32)],
        compiler_params=_params(("arbitrary", "arbitrary")),
        name="dwconv",
    )(u, w_dw, b_dw.reshape(1, d))


def _router_body(h_ref, g_ref, sh_ref, sc_ref, wr_ref, br_ref,
                 bp_ref, ri_ref, rw_ref, cnt_ref, carry_ref):
    tm = h_ref.shape[1]

    @pl.when((pl.program_id(0) == 0) & (pl.program_id(1) == 0))
    def _():
        carry_ref[...] = jnp.zeros_like(carry_ref)

    bmod = _rms(h_ref[0], g_ref[...]) * (1.0 + sc_ref[0]) + sh_ref[0]
    _store_token_tiles(bp_ref, _pack_bf16_pairs(bmod))

    b_hi = bmod.astype(BF16)
    b_lo = (bmod - b_hi.astype(F32)).astype(BF16)
    w = wr_ref[...]
    w_hi = w.astype(BF16)
    w_lo = (w - w_hi.astype(F32)).astype(BF16)
    logits = _dot(b_hi, w_hi) + _dot(b_hi, w_lo) + _dot(b_lo, w_hi) + br_ref[...]

    lane_i = lax.broadcasted_iota(I32, (tm, LANES), 1)
    lane = lane_i.astype(F32)
    neg = jnp.float32(-jnp.inf)
    big = jnp.float32(1 << 20)
    is_grp = (lane_i >= MOE_EXPERTS) & (lane_i < MOE_EXPERTS + MOE_GROUPS)
    m1 = jnp.max(jnp.where(is_grp, logits, neg), axis=1, keepdims=True)
    grp = jnp.min(jnp.where(is_grp & (logits == m1), lane - MOE_EXPERTS, big), axis=1, keepdims=True)
    pg = 1.0 / jnp.sum(jnp.where(is_grp, jnp.exp(logits - m1), 0.0), axis=1, keepdims=True)

    sel = (lane >= grp * MOE_PER_GROUP) & (lane < (grp + 1) * MOE_PER_GROUP)
    m2 = jnp.max(jnp.where(sel, logits, neg), axis=1, keepdims=True)
    ea = jnp.min(jnp.where(sel & (logits == m2), lane, big), axis=1, keepdims=True)
    sel2 = sel & (lane != ea)
    m3 = jnp.max(jnp.where(sel2, logits, neg), axis=1, keepdims=True)
    eb = jnp.min(jnp.where(sel2 & (logits == m3), lane, big), axis=1, keepdims=True)
    tr = jnp.exp(m3 - m2)
    wa = pg / (1.0 + tr)
    wb = wa * tr

    hit_a = lane == ea
    hit_b = lane == eb
    onehot = jnp.where(hit_a | hit_b, 1.0, 0.0)
    r_i = lax.broadcasted_iota(I32, (tm, tm), 0)
    c_i = lax.broadcasted_iota(I32, (tm, tm), 1)
    lower = jnp.where(r_i > c_i, 1.0, 0.0).astype(BF16)
    before = _dot(lower, onehot.astype(BF16)) + carry_ref[...]
    rank_a = jnp.sum(jnp.where(hit_a, before, 0.0), axis=1, keepdims=True).astype(I32)
    rank_b = jnp.sum(jnp.where(hit_b, before, 0.0), axis=1, keepdims=True).astype(I32)
    carry = carry_ref[...] + jnp.sum(onehot, axis=0, keepdims=True)
    carry_ref[...] = carry
    cnt_ref[...] = carry

    ri_ref[...] = jnp.where(lane_i == 0, ea.astype(I32), jnp.where(lane_i == 1, eb.astype(I32),
                            jnp.where(lane_i == 2, rank_a, jnp.where(lane_i == 3, rank_b, 0))))
    rw_ref[...] = jnp.where(lane_i == 0, wa, jnp.where(lane_i == 1, wb, 0.0))


def _router(h, g, mod3, row_fn, sec_shift, sec_scale, w_r, b_r):
    bsz, t, d = h.shape
    tm = _pick(t, 256)
    nt = t // tm
    n = bsz * t
    rpt = d // 2 // LANES
    return pl.pallas_call(
        _router_body,
        grid=(bsz, nt),
        in_specs=[
            pl.BlockSpec((1, tm, d), lambda b, i: (b, i, 0)),
            pl.BlockSpec((1, d), lambda b, i: (0, 0)),
            _mod_spec(d, row_fn, sec_shift),
            _mod_spec(d, row_fn, sec_scale),
            pl.BlockSpec((d, LANES), lambda b, i: (0, 0)),
            pl.BlockSpec((1, LANES), lambda b, i: (0, 0)),
        ],
        out_specs=[
            pl.BlockSpec((tm * rpt, LANES), lambda b, i: (b * nt + i, 0)),
            pl.BlockSpec((tm, LANES), lambda b, i: (b * nt + i, 0)),
            pl.BlockSpec((tm, LANES), lambda b, i: (b * nt + i, 0)),
            pl.BlockSpec((1, LANES), lambda b, i: (0, 0)),
        ],
        out_shape=[
            jax.ShapeDtypeStruct((n * rpt, LANES), U32),
            jax.ShapeDtypeStruct((n, LANES), I32),
            jax.ShapeDtypeStruct((n, LANES), F32),
            jax.ShapeDtypeStruct((1, LANES), F32),
        ],
        scratch_shapes=[pltpu.VMEM((1, LANES), F32)],
        compiler_params=_params(("arbitrary", "arbitrary")),
        name="router",
    )(h, g.reshape(1, d), mod3, mod3, w_r, b_r)


def _expert_body(meta_ref, xs_ref, wg_ref, wu_ref, wd_ref, y_ref,
                 wgb_ref, wub_ref, wdb_ref, xbuf, ybuf, xsem, ysem, *, rpt):
    e = pl.program_id(0)
    blk0 = meta_ref[0, e]
    n_blk = meta_ref[0, e + 1] - blk0
    n_used = meta_ref[2, 0]
    dh = rpt * LANES
    tile_rows = MOE_BLOCK * rpt

    def block_rows(ref, g):
        return ref.at[pl.ds(pl.multiple_of(g * tile_rows, tile_rows), tile_rows), :]

    def read_in(g, slot):
        return pltpu.make_async_copy(block_rows(xs_ref, g), xbuf.at[slot], xsem.at[slot])

    def write_out(g):
        return pltpu.make_async_copy(ybuf, block_rows(y_ref, g), ysem)

    @pl.when(e == 0)
    def _():
        read_in(0, 0).start()

    @pl.when(n_blk > 0)
    def _():
        wgb_ref[...] = wg_ref[0, 0].astype(BF16)
        wub_ref[...] = wu_ref[0, 0].astype(BF16)
        wdb_ref[...] = wd_ref[0, 0].astype(BF16)

        def block(j, c):
            g = blk0 + j
            slot = g % 2

            @pl.when(g + 1 < n_used)
            def _():
                read_in(g + 1, 1 - slot).start()

            read_in(g, slot).wait()
            x_lo, x_hi = _unpack_bf16_pairs(_load_token_tiles(xbuf.at[slot], MOE_BLOCK, rpt))
            x_lo = x_lo.astype(BF16)
            x_hi = x_hi.astype(BF16)
            gate = _dot(x_lo, wgb_ref[0:dh, :]) + _dot(x_hi, wgb_ref[dh:2 * dh, :])
            up = _dot(x_lo, wub_ref[0:dh, :]) + _dot(x_hi, wub_ref[dh:2 * dh, :])
            mid = (_silu(gate) * up).astype(BF16)
            out = _pack_bf16_pairs(_dot(mid, wdb_ref[...]))

            @pl.when(g > 0)
            def _():
                write_out(g - 1).wait()

            _store_token_tiles(ybuf, out)
            write_out(g).start()
            return c

        lax.fori_loop(0, n_blk, block, 0)

    @pl.when(e == pl.num_programs(0) - 1)
    def _():
        write_out(n_used - 1).wait()


def _experts(meta, xs, w_gate, w_up, w_down, layer):
    _, n_exp, d, hid = w_gate.shape
    rpt = d // 2 // LANES
    n_slots = xs.shape[0] // rpt

    def wspec(shape):
        return pl.BlockSpec((1, 1) + shape, lambda e, m: (layer, e, 0, 0))

    return pl.pallas_call(
        functools.partial(_expert_body, rpt=rpt),
        grid_spec=pltpu.PrefetchScalarGridSpec(
            num_scalar_prefetch=1,
            grid=(n_exp,),
            in_specs=[pl.BlockSpec(memory_space=pl.ANY),
                      wspec((d, hid)), wspec((d, hid)), wspec((hid, d))],
            out_specs=pl.BlockSpec(memory_space=pl.ANY),
            scratch_shapes=[pltpu.VMEM((d, hid), BF16), pltpu.VMEM((d, hid), BF16),
                            pltpu.VMEM((hid, d), BF16),
                            pltpu.VMEM((2, MOE_BLOCK * rpt, LANES), U32),
                            pltpu.VMEM((MOE_BLOCK * rpt, LANES), U32),
                            pltpu.SemaphoreType.DMA((2,)), pltpu.SemaphoreType.DMA],
        ),
        out_shape=jax.ShapeDtypeStruct((n_slots * rpt, LANES), U32),
        compiler_params=_params(("arbitrary",)),
        name="moe_experts",
    )(meta, xs, w_gate, w_up, w_down)


def _sc_mesh():
    return plsc.VectorSubcoreMesh(core_axis_name="core", subcore_axis_name="subcore")


def _sc_scatter_rows_twice(src, idx_a, idx_b, n_out):
    n = src.shape[0]
    win = _pick(n, SC_GATHER_WINDOW)

    @pl.kernel(out_type=jax.ShapeDtypeStruct((n_out, LANES), src.dtype), mesh=_sc_mesh(),
               scratch_types=[])
    def scatter_kernel(s_hbm, ia_hbm, ib_hbm, o_hbm):
        def window(s_vmem, ia_vmem, ib_vmem):
            pltpu.sync_copy(s_vmem, o_hbm.at[ia_vmem.at[0]])
            pltpu.sync_copy(s_vmem, o_hbm.at[ib_vmem.at[0]])

        pltpu.emit_pipeline(
            window,
            grid=(n // win,),
            in_specs=[pl.BlockSpec((win, LANES), index_map=lambda i: (i, 0)),
                      pl.BlockSpec((1, win), index_map=lambda i: (0, i)),
                      pl.BlockSpec((1, win), index_map=lambda i: (0, i))],
            out_specs=[],
            core_axis_name=("core", "subcore"),
            dimension_semantics=(pltpu.PARALLEL,),
        )(s_hbm, ia_hbm, ib_hbm)

    return scatter_kernel(src, idx_a.reshape(1, n), idx_b.reshape(1, n))


def _sc_gather_rows(table, idx):
    n_idx = idx.shape[0]
    win = _pick(n_idx, SC_GATHER_WINDOW)
    mesh = _sc_mesh()

    @pl.kernel(out_type=jax.ShapeDtypeStruct((n_idx, LANES), table.dtype), mesh=mesh)
    def gather_kernel(t_hbm, i_hbm, o_hbm):
        def window(i_vmem, o_vmem):
            pltpu.sync_copy(t_hbm.at[i_vmem.at[0]], o_vmem)

        pltpu.emit_pipeline(
            window,
            grid=(n_idx // win,),
            in_specs=[pl.BlockSpec((1, win), index_map=lambda i: (0, i))],
            out_specs=[pl.BlockSpec((win, LANES), index_map=lambda i: (i, 0))],
            core_axis_name=("core", "subcore"),
            dimension_semantics=(pltpu.PARALLEL,),
        )(i_hbm, o_hbm)

    return gather_kernel(table, idx.reshape(1, n_idx))


def _combine_body(h_ref, y_ref, rw_ref, gt_ref, fg_ref, o_ref, *, rpt, final):
    tm = h_ref.shape[1]
    dh = rpt * LANES
    wa = rw_ref[:, 0:1]
    wb = rw_ref[:, 1:2]
    a_lo, a_hi = _unpack_bf16_pairs(_load_token_tiles(y_ref, tm, rpt, 0, 2 * rpt))
    b_lo, b_hi = _unpack_bf16_pairs(_load_token_tiles(y_ref, tm, rpt, rpt, 2 * rpt))
    gt = gt_ref[0]
    lo = h_ref[0, :, 0:dh] + gt[:, 0:dh] * (wa * a_lo + wb * b_lo)
    hi = h_ref[0, :, dh:2 * dh] + gt[:, dh:2 * dh] * (wa * a_hi + wb * b_hi)
    if final:
        ms = (jnp.sum(lo * lo, axis=-1, keepdims=True)
              + jnp.sum(hi * hi, axis=-1, keepdims=True)) / (2 * dh)
        inv = lax.rsqrt(ms + EPS)
        lo = lo * inv * fg_ref[:, 0:dh]
        hi = hi * inv * fg_ref[:, dh:2 * dh]
    o_ref[0, :, 0:dh] = lo
    o_ref[0, :, dh:2 * dh] = hi


def _combine(h, y, rw, mod3, row_fn, sec_gate, final_g, final):
    bsz, t, d = h.shape
    tm = _pick(t, 256)
    nt = t // tm
    rpt = d // 2 // LANES
    return pl.pallas_call(
        functools.partial(_combine_body, rpt=rpt, final=final),
        grid=(bsz, nt),
        in_specs=[
            pl.BlockSpec((1, tm, d), lambda b, i: (b, i, 0)),
            pl.BlockSpec((tm * 2 * rpt, LANES), lambda b, i: (b * nt + i, 0)),
            pl.BlockSpec((tm, LANES), lambda b, i: (b * nt + i, 0)),
            _mod_spec(d, row_fn, sec_gate),
            pl.BlockSpec((1, d), lambda b, i: (0, 0)),
        ],
        out_specs=pl.BlockSpec((1, tm, d), lambda b, i: (b, i, 0)),
        out_shape=jax.ShapeDtypeStruct((bsz, t, d), F32),
        compiler_params=_params(("arbitrary", "arbitrary")),
        name="moe_combine",
    )(h, y, rw, mod3, final_g.reshape(1, d))


def _hier_moe(h, g, mod3, row_fn, w_r1, b_r1, w_r2, b_r2, w_gate, w_up, w_down, layer, final_g, final):
    bsz, t, d = h.shape
    n = bsz * t
    w_r = jnp.concatenate([jnp.transpose(w_r2, (1, 0, 2)).reshape(d, MOE_EXPERTS), w_r1], axis=1)
    w_r = jnp.pad(w_r, ((0, 0), (0, LANES - w_r.shape[1])))
    b_r = jnp.pad(jnp.concatenate([b_r2.reshape(-1), b_r1]), (0, LANES - MOE_EXPERTS - MOE_GROUPS))
    bp, ri, rw, cnt = _router(h, g, mod3, row_fn, 3, 4, w_r, b_r.reshape(1, LANES))

    counts = cnt[0, :MOE_EXPERTS].astype(I32)
    padded = (counts + MOE_BLOCK - 1) // MOE_BLOCK * MOE_BLOCK
    pad_end = jnp.cumsum(padded)
    pad_start = pad_end - padded
    n_slots = (-(-(n * 2) // MOE_BLOCK) + MOE_EXPERTS) * MOE_BLOCK
    blk_off = jnp.concatenate([pad_start, pad_end[-1:]]) // MOE_BLOCK
    row = lambda v: jnp.pad(v.astype(I32), (0, LANES - v.shape[0]))
    meta = jnp.stack([row(blk_off), row(counts), row(blk_off[-1:])])
    hit = ri[:, 0:2, None] == jnp.arange(MOE_EXPERTS, dtype=I32)
    pos = jnp.sum(jnp.where(hit, pad_start, 0), axis=-1) + ri[:, 2:4]
    rpt = d // 2 // LANES
    rows = pos[:, :, None] * rpt + jnp.arange(rpt, dtype=I32)

    xs = _sc_scatter_rows_twice(bp, rows[:, 0].reshape(-1), rows[:, 1].reshape(-1), n_slots * rpt)
    ys = _experts(meta, xs, w_gate, w_up, w_down, layer)
    y = _sc_gather_rows(ys, rows.reshape(-1))
    return _combine(h, y, rw, mod3, row_fn, 5, final_g, final)


def kernel(x, c, ctx, c_ctx, ada_w, ada_b, norm1_g, norm2_g, hgrn_w_in, hgrn_lb, hgrn_onorm_g,
           hgrn_w_out, conv_w_pw1, conv_w_dw, conv_b_dw, conv_ln_g, conv_ln_b, conv_w_pw2,
           moe_w_r1, moe_b_r1, moe_w_r2, moe_b_r2, moe_w_gate, moe_w_up, moe_w_down, final_g):
    bsz, t, d = x.shape
    t_ctx = ctx.shape[1]
    depth = ada_w.shape[0]
    assert depth == 2 and bsz < MOD_ROWS and d % (2 * LANES) == 0 and t % GRID_W == 0

    c_all = jnp.concatenate([c, c_ctx[None, :], jnp.zeros((MOD_ROWS - bsz - 1, d), F32)], axis=0)
    mod = _adaln(c_all, ada_w, ada_b)
    mod3 = mod.reshape(depth * MOD_ROWS, 1, 6 * d)
    lat_row = lambda l: (lambda b: l * MOD_ROWS + b)
    ctx_row = lambda l: (lambda b: l * MOD_ROWS + bsz)
    lb_all = jnp.cumsum(jax.nn.softmax(hgrn_lb.astype(F32), axis=1), axis=1)

    def moe(h, l, final):
        return _hier_moe(h, norm2_g[l], mod3, lat_row(l), moe_w_r1[l], moe_b_r1[l], moe_w_r2[l],
                         moe_b_r2[l], moe_w_gate, moe_w_up, moe_w_down, l, final_g, final)

    a_lat = _norm_mod(x, norm1_g[0], mod3, lat_row(0), 0, 1)
    a_ctx = _norm_mod(ctx, norm1_g[0], mod3, ctx_row(0), 0, 1)
    p_lat = _matmul(a_lat.reshape(bsz * t, d), hgrn_w_in[0]).reshape(bsz, t, 5 * d)
    p_ctx = _matmul(a_ctx.reshape(bsz * t_ctx, d), hgrn_w_in[0]).reshape(bsz, t_ctx, 5 * d)
    o2 = _scan(p_ctx, p_lat, lb_all[:, 0])
    h = _hgrn_readout(o2, p_lat, hgrn_onorm_g[0], hgrn_w_out[0], x, mod3, lat_row(0), 2)
    h = moe(h, 0, False)

    a = _norm_mod(h, norm1_g[1], mod3, lat_row(1), 0, 1)
    u = _matmul_glu(a.reshape(bsz * t, d), conv_w_pw1[0]).reshape(bsz, t, d)
    v = _dwconv(u, conv_w_dw[0], conv_b_dw[0])
    h = _conv_out(v, conv_ln_g[0], conv_ln_b[0], conv_w_pw2[0], h, mod3, lat_row(1), 2)
    return moe(h, 1, True)
```

```python
import functools

import jax
import jax.numpy as jnp
from jax import lax
from jax.experimental import pallas as pl
from jax.experimental.pallas import tpu as pltpu
from jax.experimental.pallas import tpu_sc as plsc

F32 = jnp.float32
BF16 = jnp.bfloat16
U32 = jnp.uint32
I32 = jnp.int32

EPS = 1e-6
LANES = 128
SUBLANES = 8
HEAD_DIM = 128
SCAN_CHUNK = 64
GRID_W = 64
CONV_WIDTH = 31
CONV_REACH = (16 - CONV_WIDTH // 2 + CONV_WIDTH - 1) // 8 * 8
MOE_GROUPS = 8
MOE_PER_GROUP = 8
MOE_EXPERTS = MOE_GROUPS * MOE_PER_GROUP
MOE_BLOCK = 256
SC_GATHER_WINDOW = 256
MOD_ROWS = 16
VMEM_LIMIT = 56 * 1024 * 1024
SAFE_CHUNK_LOG_DECAY = -80.0
HI_MASK = 0xFFFF0000


def _pick(n, pref):
    t = pref
    while n % t:
        t //= 2
    return t


def _params(sem):
    return pltpu.CompilerParams(dimension_semantics=sem, vmem_limit_bytes=VMEM_LIMIT)


def _silu(x):
    return x * jax.nn.sigmoid(x)


def _dot(a, b):
    return jnp.dot(a, b, preferred_element_type=F32)


def _dot_nt(a, b):
    return lax.dot_general(a, b, (((1,), (1,)), ((), ())), preferred_element_type=F32)


def _dot_tn(a, b):
    return lax.dot_general(a, b, (((0,), (0,)), ((), ())), preferred_element_type=F32)


def _pack_bf16_pairs(x):
    n = x.shape[1] // 2
    bits = lax.bitcast_convert_type(x.astype(BF16).astype(F32), U32)
    return (bits[:, :n] >> 16) | (bits[:, n:] & jnp.uint32(HI_MASK))


def _unpack_bf16_pairs(w):
    lo = lax.bitcast_convert_type(w << 16, F32)
    hi = lax.bitcast_convert_type(w & jnp.uint32(HI_MASK), F32)
    return lo, hi


def _store_token_tiles(ref, w, first=0, stride=None):
    m, n = w.shape[0], w.shape[1] // LANES
    stride = n if stride is None else stride
    for s in range(n):
        ref[pl.ds(first + s, m, stride=stride), :] = w[:, s * LANES:(s + 1) * LANES]


def _load_token_tiles(ref, m, n, first=0, stride=None):
    stride = n if stride is None else stride
    return jnp.concatenate([ref[pl.ds(first + s, m, stride=stride), :] for s in range(n)], axis=1)


def _adaln_body(c_ref, w_ref, b_ref, o_ref):
    s = _silu(c_ref[...])
    o_ref[0] = _dot(s.astype(BF16), w_ref[0].astype(BF16)) + b_ref[0]


def _adaln(c_all, ada_w, ada_b):
    depth, d, n6 = ada_w.shape
    tn = _pick(n6, 1024)
    return pl.pallas_call(
        _adaln_body,
        grid=(depth, n6 // tn),
        in_specs=[
            pl.BlockSpec((MOD_ROWS, d), lambda l, j: (0, 0)),
            pl.BlockSpec((1, d, tn), lambda l, j: (l, 0, j)),
            pl.BlockSpec((1, 1, tn), lambda l, j: (l, 0, j)),
        ],
        out_specs=pl.BlockSpec((1, MOD_ROWS, tn), lambda l, j: (l, 0, j)),
        out_shape=jax.ShapeDtypeStruct((depth, MOD_ROWS, n6), F32),
        compiler_params=_params(("arbitrary", "arbitrary")),
        name="adaln",
    )(c_all, ada_w, ada_b.reshape(depth, 1, n6))


def _mod_spec(d, row_fn, sec):
    return pl.BlockSpec((1, 1, d), lambda b, i: (row_fn(b), 0, sec))


def _rms(x, g):
    return x * lax.rsqrt(jnp.mean(x * x, axis=-1, keepdims=True) + EPS) * g


def _norm_mod_body(x_ref, g_ref, sh_ref, sc_ref, o_ref):
    y = _rms(x_ref[0], g_ref[...])
    o_ref[0] = (y * (1.0 + sc_ref[0]) + sh_ref[0]).astype(o_ref.dtype)


def _norm_mod(x, g, mod3, row_fn, sec_shift, sec_scale):
    bsz, t, d = x.shape
    tm = _pick(t, 512)
    return pl.pallas_call(
        _norm_mod_body,
        grid=(bsz, t // tm),
        in_specs=[
            pl.BlockSpec((1, tm, d), lambda b, i: (b, i, 0)),
            pl.BlockSpec((1, d), lambda b, i: (0, 0)),
            _mod_spec(d, row_fn, sec_shift),
            _mod_spec(d, row_fn, sec_scale),
        ],
        out_specs=pl.BlockSpec((1, tm, d), lambda b, i: (b, i, 0)),
        out_shape=jax.ShapeDtypeStruct((bsz, t, d), BF16),
        compiler_params=_params(("arbitrary", "arbitrary")),
        name="norm_mod",
    )(x, g.reshape(1, d), mod3, mod3)


def _mm_body(a_ref, w_ref, o_ref, wb_ref):
    @pl.when(pl.program_id(1) == 0)
    def _():
        wb_ref[...] = w_ref[...].astype(BF16)

    o_ref[...] = _dot(a_ref[...], wb_ref[...]).astype(o_ref.dtype)


def _matmul(a, w):
    m, k = a.shape
    n = w.shape[1]
    tm, tn = _pick(m, 512), _pick(n, 1024)
    return pl.pallas_call(
        _mm_body,
        grid=(n // tn, m // tm),
        in_specs=[
            pl.BlockSpec((tm, k), lambda j, i: (i, 0)),
            pl.BlockSpec((k, tn), lambda j, i: (0, j)),
        ],
        out_specs=pl.BlockSpec((tm, tn), lambda j, i: (i, j)),
        out_shape=jax.ShapeDtypeStruct((m, n), BF16),
        scratch_shapes=[pltpu.VMEM((k, tn), BF16)],
        compiler_params=_params(("arbitrary", "arbitrary")),
        name="matmul",
    )(a, w)


def _mm_glu_body(a_ref, wv_ref, wg_ref, o_ref, wvb_ref, wgb_ref):
    @pl.when(pl.program_id(1) == 0)
    def _():
        wvb_ref[...] = wv_ref[...].astype(BF16)
        wgb_ref[...] = wg_ref[...].astype(BF16)

    a = a_ref[...]
    val = _dot(a, wvb_ref[...])
    gate = _dot(a, wgb_ref[...])
    o_ref[...] = (val * jax.nn.sigmoid(gate)).astype(o_ref.dtype)


def _matmul_glu(a, w):
    m, k = a.shape
    n = w.shape[1] // 2
    tm, tn = _pick(m, 512), _pick(n, 512)
    nj = n // tn
    return pl.pallas_call(
        _mm_glu_body,
        grid=(nj, m // tm),
        in_specs=[
            pl.BlockSpec((tm, k), lambda j, i: (i, 0)),
            pl.BlockSpec((k, tn), lambda j, i: (0, j)),
            pl.BlockSpec((k, tn), lambda j, i: (0, j + nj)),
        ],
        out_specs=pl.BlockSpec((tm, tn), lambda j, i: (i, j)),
        out_shape=jax.ShapeDtypeStruct((m, n), BF16),
        scratch_shapes=[pltpu.VMEM((k, tn), BF16), pltpu.VMEM((k, tn), BF16)],
        compiler_params=_params(("arbitrary", "arbitrary")),
        name="matmul_glu",
    )(a, w, w)


def _scan_body(qc_ref, vc_ref, zc_ref, ql_ref, vl_ref, zl_ref, lb_ref, o_ref,
               st_ref, flag_ref, b_sc, q_sc, k_sc, v_sc, *, nc, n_sub, heads):
    chunk = SCAN_CHUNK
    direction = pl.program_id(1)
    step = pl.program_id(2)
    fwd = direction == 0

    @pl.when(step == 0)
    def _():
        st_ref[0] = jnp.zeros(st_ref.shape[1:], F32)

    @pl.when(step > 0)
    def _():
        st_ref[0] = st_ref[n_sub]

    lb = lb_ref[0]
    row = lax.broadcasted_iota(I32, (chunk, chunk), 0)
    col = lax.broadcasted_iota(I32, (chunk, chunk), 1)
    sign = 1 - 2 * direction
    tri = (row - col) * sign >= 0
    tri_b = jnp.where(tri, 1.0, 0.0).astype(BF16)
    t_idx = lax.broadcasted_iota(I32, (chunk, 1), 0)

    def chunk_rows(ci):
        cidx = jnp.where(fwd, ci, n_sub - 1 - ci)
        return pl.ds(pl.multiple_of(cidx * chunk, chunk), chunk)

    def decay(z_ref, rows, sl):
        z = z_ref[0, rows, sl].astype(F32)
        f = lb[:, sl] + (1.0 - lb[:, sl]) * jax.nn.sigmoid(z)
        lf = jnp.log(f)
        hi = lf.astype(BF16)
        lo = (lf - hi.astype(F32)).astype(BF16)
        b = _dot(tri_b, hi) + _dot(tri_b, lo)
        tot = jnp.where(fwd, b[chunk - 1:chunk, :], b[0:1, :])
        return b, tot, 1.0 - f

    def fast_chunk(q_ref, v_ref, z_ref, ci, emit):
        rows = chunk_rows(ci)
        b, tot, kk = decay(z_ref, rows, slice(None))
        ke = kk * jnp.exp(tot - b)
        ke_b = ke.astype(BF16)
        etot = jnp.exp(tot)
        v = v_ref[0, rows, :]
        head_sl = [slice(h * HEAD_DIM, (h + 1) * HEAD_DIM) for h in range(heads)]
        st = [st_ref[ci, h] for h in range(heads)]
        kv = [_dot_tn(v[:, sl], ke_b[:, sl]) for sl in head_sl]
        if emit:
            qe = _silu(q_ref[0, rows, :].astype(F32)) * jnp.exp(b)
            qe_b = qe.astype(BF16)
            g = jnp.exp(-0.5 * tot)
            qh = (qe * g).astype(BF16)
            kh = (ke * g).astype(BF16)
            flag_ref[ci] = (jnp.min(tot) < SAFE_CHUNK_LOG_DECAY).astype(I32)
            sc = [_dot_nt(qh[:, sl], kh[:, sl]) for sl in head_sl]
            o_inter = [_dot_nt(qe_b[:, sl], st[h].astype(BF16)) for h, sl in enumerate(head_sl)]
            p = [jnp.where(tri, s, 0.0).astype(BF16) for s in sc]
            o_intra = [_dot(p[h], v[:, sl]) for h, sl in enumerate(head_sl)]
            for h, sl in enumerate(head_sl):
                o_ref[0, 0, rows, sl] = (o_inter[h] + o_intra[h]).astype(o_ref.dtype)
        for h, sl in enumerate(head_sl):
            st_ref[ci + 1, h] = st[h] * etot[:, sl] + kv[h]

    def exact_chunk(q_ref, v_ref, z_ref, ci):
        rows = chunk_rows(ci)
        b, _, kk = decay(z_ref, rows, slice(None))
        b_sc[...] = b
        q_sc[...] = _silu(q_ref[0, rows, :].astype(F32))
        k_sc[...] = kk
        v_sc[...] = v_ref[0, rows, :].astype(F32)

        def head_loop(h, c2):
            lanes = pl.ds(pl.multiple_of(h * HEAD_DIM, HEAD_DIM), HEAD_DIM)
            b_h = b_sc[:, lanes]
            q_h = q_sc[:, lanes]
            k_h = k_sc[:, lanes]
            v_h = v_sc[:, lanes]
            o_inter = _dot_nt((q_h * jnp.exp(b_h)).astype(BF16), st_ref[ci, h].astype(BF16))

            def key_loop(s, acc):
                pick = t_idx == s
                b_s = jnp.sum(jnp.where(pick, b_h, 0.0), axis=0, keepdims=True)
                k_s = jnp.sum(jnp.where(pick, k_h, 0.0), axis=0, keepdims=True)
                v_s = jnp.sum(jnp.where(pick, v_h, 0.0), axis=0, keepdims=True)
                valid = (t_idx - s) * sign >= 0
                dec = jnp.where(valid, jnp.exp(jnp.where(valid, b_h - b_s, 0.0)), 0.0)
                w = jnp.sum(q_h * dec * k_s, axis=1, keepdims=True)
                return acc + w * v_s

            o_intra = lax.fori_loop(0, chunk, key_loop, jnp.zeros((chunk, HEAD_DIM), F32))
            o_ref[0, 0, rows, lanes] = (o_inter + o_intra).astype(o_ref.dtype)
            return c2

        lax.fori_loop(0, heads, head_loop, 0)

    @pl.when(step < nc)
    def _():
        for ci in range(n_sub):
            fast_chunk(qc_ref, vc_ref, zc_ref, ci, False)

    @pl.when(step >= nc)
    def _():
        for ci in range(n_sub):
            fast_chunk(ql_ref, vl_ref, zl_ref, ci, True)

        def redo(ci, c):
            @pl.when(flag_ref[ci] == 1)
            def _():
                exact_chunk(ql_ref, vl_ref, zl_ref, ci)
            return c

        lax.fori_loop(0, n_sub, redo, 0)


def _scan(p_ctx, p_lat, lb):
    bsz, t_lat, d5 = p_lat.shape
    t_ctx = p_ctx.shape[1]
    d = d5 // 5
    heads = d // HEAD_DIM
    tb = _pick(t_ctx, 256)
    assert t_lat % tb == 0 and tb % SCAN_CHUNK == 0
    nc, nl = t_ctx // tb, t_lat // tb
    n_sub = tb // SCAN_CHUNK

    def ctx_blk(dr, s):
        sc = jnp.minimum(s, nc - 1)
        return jnp.where(dr == 0, sc, nc - 1 - sc)

    def lat_blk(dr, s):
        sl = jnp.maximum(s - nc, 0)
        return jnp.where(dr == 0, sl, nl - 1 - sl)

    def cspec(sec_fn):
        return pl.BlockSpec((1, tb, d), lambda b, dr, s: (b, ctx_blk(dr, s), sec_fn(dr)))

    def lspec(sec_fn):
        return pl.BlockSpec((1, tb, d), lambda b, dr, s: (b, lat_blk(dr, s), sec_fn(dr)))

    q_sec = lambda dr: 0
    v_sec = lambda dr: 1
    z_sec = lambda dr: 2 + dr
    return pl.pallas_call(
        functools.partial(_scan_body, nc=nc, n_sub=n_sub, heads=heads),
        grid=(bsz, 2, nc + nl),
        in_specs=[cspec(q_sec), cspec(v_sec), cspec(z_sec),
                  lspec(q_sec), lspec(v_sec), lspec(z_sec),
                  pl.BlockSpec((1, 1, d), lambda b, dr, s: (dr, 0, 0))],
        out_specs=pl.BlockSpec((1, 1, tb, d), lambda b, dr, s: (dr, b, lat_blk(dr, s), 0)),
        out_shape=jax.ShapeDtypeStruct((2, bsz, t_lat, d), BF16),
        scratch_shapes=[pltpu.VMEM((n_sub + 1, heads, HEAD_DIM, HEAD_DIM), F32),
                        pltpu.SMEM((n_sub,), I32),
                        pltpu.VMEM((SCAN_CHUNK, d), F32),
                        pltpu.VMEM((SCAN_CHUNK, d), F32),
                        pltpu.VMEM((SCAN_CHUNK, d), F32),
                        pltpu.VMEM((SCAN_CHUNK, d), F32)],
        compiler_params=_params(("arbitrary", "arbitrary", "arbitrary")),
        name="hgrn_scan",
    )(p_ctx, p_ctx, p_ctx, p_lat, p_lat, p_lat, lb.reshape(2, 1, d))


def _hgrn_readout_body(of_ref, ob_ref, g_ref, on_ref, w_ref, h_ref, gt_ref, o_ref,
                       wb_ref, a_ref, *, heads):
    @pl.when((pl.program_id(0) == 0) & (pl.program_id(1) == 0))
    def _():
        wb_ref[...] = w_ref[...].astype(BF16)

    on = on_ref[...]
    for h in range(heads):
        sl = slice(h * HEAD_DIM, (h + 1) * HEAD_DIM)
        o = of_ref[0, 0, :, sl].astype(F32) + ob_ref[0, 0, :, sl].astype(F32)
        o = _rms(o, on)
        a_ref[:, sl] = (o * _silu(g_ref[0, :, sl].astype(F32))).astype(BF16)
    y = _dot(a_ref[...], wb_ref[...])
    o_ref[0] = h_ref[0] + gt_ref[0] * y


def _hgrn_readout(o2, p_lat, onorm_g, w_out, h, mod3, row_fn, sec_gate):
    bsz, t, d = h.shape
    heads = d // HEAD_DIM
    tm = _pick(t, 256)
    return pl.pallas_call(
        functools.partial(_hgrn_readout_body, heads=heads),
        grid=(bsz, t // tm),
        in_specs=[
            pl.BlockSpec((1, 1, tm, d), lambda b, i: (0, b, i, 0)),
            pl.BlockSpec((1, 1, tm, d), lambda b, i: (1, b, i, 0)),
            pl.BlockSpec((1, tm, d), lambda b, i: (b, i, 4)),
            pl.BlockSpec((1, HEAD_DIM), lambda b, i: (0, 0)),
            pl.BlockSpec((d, d), lambda b, i: (0, 0), pipeline_mode=pl.Buffered(1)),
            pl.BlockSpec((1, tm, d), lambda b, i: (b, i, 0)),
            _mod_spec(d, row_fn, sec_gate),
        ],
        out_specs=pl.BlockSpec((1, tm, d), lambda b, i: (b, i, 0)),
        out_shape=jax.ShapeDtypeStruct((bsz, t, d), F32),
        scratch_shapes=[pltpu.VMEM((d, d), BF16), pltpu.VMEM((tm, d), BF16)],
        compiler_params=_params(("arbitrary", "arbitrary")),
        name="hgrn_readout",
    )(o2, o2, p_lat, onorm_g.reshape(1, HEAD_DIM), w_out, h, mod3)


def _conv_out_body(v_ref, lg_ref, lbias_ref, w_ref, h_ref, gt_ref, o_ref, wb_ref):
    @pl.when((pl.program_id(0) == 0) & (pl.program_id(1) == 0))
    def _():
        wb_ref[...] = w_ref[...].astype(BF16)

    x = v_ref[0].astype(F32)
    mu = jnp.mean(x, axis=-1, keepdims=True)
    xc = x - mu
    y = xc * lax.rsqrt(jnp.mean(xc * xc, axis=-1, keepdims=True) + EPS)
    y = _silu(y * lg_ref[...] + lbias_ref[...])
    o_ref[0] = h_ref[0] + gt_ref[0] * _dot(y.astype(BF16), wb_ref[...])


def _conv_out(v, ln_g, ln_b, w, h, mod3, row_fn, sec_gate):
    bsz, t, d = h.shape
    tm = _pick(t, 256)
    return pl.pallas_call(
        _conv_out_body,
        grid=(bsz, t // tm),
        in_specs=[
            pl.BlockSpec((1, tm, d), lambda b, i: (b, i, 0)),
            pl.BlockSpec((1, d), lambda b, i: (0, 0)),
            pl.BlockSpec((1, d), lambda b, i: (0, 0)),
            pl.BlockSpec((d, d), lambda b, i: (0, 0), pipeline_mode=pl.Buffered(1)),
            pl.BlockSpec((1, tm, d), lambda b, i: (b, i, 0)),
            _mod_spec(d, row_fn, sec_gate),
        ],
        out_specs=pl.BlockSpec((1, tm, d), lambda b, i: (b, i, 0)),
        out_shape=jax.ShapeDtypeStruct((bsz, t, d), F32),
        scratch_shapes=[pltpu.VMEM((d, d), BF16)],
        compiler_params=_params(("arbitrary", "arbitrary")),
        name="conv_out",
    )(v, ln_g.reshape(1, d), ln_b.reshape(1, d), w, h, mod3)


def _dwconv_body(u_ref, w_ref, b_ref, o_ref, pad_ref, shift_ref, *, rows, cb, half_blocks):
    half = CONV_WIDTH // 2
    j = pl.program_id(1)
    n_lane = cb // LANES
    bias = b_ref[...]

    def taps(read, r):
        for lb in range(n_lane):
            ln = slice(lb * LANES, (lb + 1) * LANES)
            acc = jnp.zeros((GRID_W, LANES), F32)
            for k in range(CONV_WIDTH):
                acc = acc + w_ref[k:k + 1, ln] * read(r, k, ln)
            o_ref[0, pl.ds(pl.multiple_of(r * GRID_W, GRID_W), GRID_W), ln] = (
                acc + bias[:, ln]).astype(o_ref.dtype)

    @pl.when(j < half_blocks)
    def _():
        zeros = jnp.zeros((rows, 16, cb), F32)
        pad_ref[0:rows, 0:16, :] = zeros
        pad_ref[0:rows, 16 + GRID_W:32 + GRID_W, :] = zeros
        pad_ref[0:rows, 16:16 + GRID_W, :] = u_ref[0].astype(F32).reshape(rows, GRID_W, cb)

        span = GRID_W + CONV_REACH

        def row_loop(r, c):
            for s in range(1, SUBLANES):
                shift_ref[s - 1] = pad_ref[r, s:s + span, :]

            def read(r_, k, ln):
                a, s = divmod(16 - half + k, SUBLANES)
                if s == 0:
                    return pad_ref[r_, a * SUBLANES:a * SUBLANES + GRID_W, ln]
                return shift_ref[s - 1, a * SUBLANES:a * SUBLANES + GRID_W, ln]

            taps(read, r)
            return c

        lax.fori_loop(0, rows, row_loop, 0)

    @pl.when(j >= half_blocks)
    def _():
        zeros = jnp.zeros((half, GRID_W, cb), F32)
        pad_ref[0:half, 0:GRID_W, :] = zeros
        pad_ref[half + rows:2 * half + rows, 0:GRID_W, :] = zeros
        pad_ref[half:half + rows, 0:GRID_W, :] = u_ref[0].astype(F32).reshape(rows, GRID_W, cb)

        def row_loop(r, c):
            taps(lambda r_, k, ln: pad_ref[r_ + k, 0:GRID_W, ln], r)
            return c

        lax.fori_loop(0, rows, row_loop, 0)


def _dwconv(u, w_dw, b_dw):
    bsz, t, d = u.shape
    rows = t // GRID_W
    cb = _pick(d // 2, 256)
    half_blocks = (d // 2) // cb
    return pl.pallas_call(
        functools.partial(_dwconv_body, rows=rows, cb=cb, half_blocks=half_blocks),
        grid=(bsz, d // cb),
        in_specs=[
            pl.BlockSpec((1, t, cb), lambda b, j: (b, 0, j)),
            pl.BlockSpec((CONV_WIDTH, cb), lambda b, j: (0, j)),
            pl.BlockSpec((1, cb), lambda b, j: (0, j)),
        ],
        out_specs=pl.BlockSpec((1, t, cb), lambda b, j: (b, 0, j)),
        out_shape=jax.ShapeDtypeStruct((bsz, t, d), BF16),
        scratch_shapes=[pltpu.VMEM((rows + 2 * (CONV_WIDTH // 2) + 2, GRID_W + 32, cb), F32),
                        pltpu.VMEM((SUBLANES - 1, GRID_W + CONV_REACH, cb), F32)],
        compiler_params=_params(("arbitrary", "arbitrary")),
        name="dwconv",
    )(u, w_dw, b_dw.reshape(1, d))


def _router_body(h_ref, g_ref, sh_ref, sc_ref, wr_ref, br_ref,
                 bp_ref, ri_ref, rw_ref, cnt_ref, carry_ref):
    tm = h_ref.shape[1]

    @pl.when((pl.program_id(0) == 0) & (pl.program_id(1) == 0))
    def _():
        carry_ref[...] = jnp.zeros_like(carry_ref)

    bmod = _rms(h_ref[0], g_ref[...]) * (1.0 + sc_ref[0]) + sh_ref[0]
    _store_token_tiles(bp_ref, _pack_bf16_pairs(bmod))

    b_hi = bmod.astype(BF16)
    b_lo = (bmod - b_hi.astype(F32)).astype(BF16)
    w = wr_ref[...]
    w_hi = w.astype(BF16)
    w_lo = (w - w_hi.astype(F32)).astype(BF16)
    logits = _dot(b_hi, w_hi) + _dot(b_hi, w_lo) + _dot(b_lo, w_hi) + br_ref[...]

    lane_i = lax.broadcasted_iota(I32, (tm, LANES), 1)
    lane = lane_i.astype(F32)
    neg = jnp.float32(-jnp.inf)
    big = jnp.float32(1 << 20)
    is_grp = (lane_i >= MOE_EXPERTS) & (lane_i < MOE_EXPERTS + MOE_GROUPS)
    m1 = jnp.max(jnp.where(is_grp, logits, neg), axis=1, keepdims=True)
    grp = jnp.min(jnp.where(is_grp & (logits == m1), lane - MOE_EXPERTS, big), axis=1, keepdims=True)
    pg = 1.0 / jnp.sum(jnp.where(is_grp, jnp.exp(logits - m1), 0.0), axis=1, keepdims=True)

    sel = (lane >= grp * MOE_PER_GROUP) & (lane < (grp + 1) * MOE_PER_GROUP)
    m2 = jnp.max(jnp.where(sel, logits, neg), axis=1, keepdims=True)
    ea = jnp.min(jnp.where(sel & (logits == m2), lane, big), axis=1, keepdims=True)
    sel2 = sel & (lane != ea)
    m3 = jnp.max(jnp.where(sel2, logits, neg), axis=1, keepdims=True)
    eb = jnp.min(jnp.where(sel2 & (logits == m3), lane, big), axis=1, keepdims=True)
    tr = jnp.exp(m3 - m2)
    wa = pg / (1.0 + tr)
    wb = wa * tr

    hit_a = lane == ea
    hit_b = lane == eb
    onehot = jnp.where(hit_a | hit_b, 1.0, 0.0)
    r_i = lax.broadcasted_iota(I32, (tm, tm), 0)
    c_i = lax.broadcasted_iota(I32, (tm, tm), 1)
    lower = jnp.where(r_i > c_i, 1.0, 0.0).astype(BF16)
    before = _dot(lower, onehot.astype(BF16)) + carry_ref[...]
    rank_a = jnp.sum(jnp.where(hit_a, before, 0.0), axis=1, keepdims=True).astype(I32)
    rank_b = jnp.sum(jnp.where(hit_b, before, 0.0), axis=1, keepdims=True).astype(I32)
    carry = carry_ref[...] + jnp.sum(onehot, axis=0, keepdims=True)
    carry_ref[...] = carry
    cnt_ref[...] = carry

    ri_ref[...] = jnp.where(lane_i == 0, ea.astype(I32), jnp.where(lane_i == 1, eb.astype(I32),
                            jnp.where(lane_i == 2, rank_a, jnp.where(lane_i == 3, rank_b, 0))))
    rw_ref[...] = jnp.where(lane_i == 0, wa, jnp.where(lane_i == 1, wb, 0.0))


def _router(h, g, mod3, row_fn, sec_shift, sec_scale, w_r, b_r):
    bsz, t, d = h.shape
    tm = _pick(t, 256)
    nt = t // tm
    n = bsz * t
    rpt = d // 2 // LANES
    return pl.pallas_call(
        _router_body,
        grid=(bsz, nt),
        in_specs=[
            pl.BlockSpec((1, tm, d), lambda b, i: (b, i, 0)),
            pl.BlockSpec((1, d), lambda b, i: (0, 0)),
            _mod_spec(d, row_fn, sec_shift),
            _mod_spec(d, row_fn, sec_scale),
            pl.BlockSpec((d, LANES), lambda b, i: (0, 0)),
            pl.BlockSpec((1, LANES), lambda b, i: (0, 0)),
        ],
        out_specs=[
            pl.BlockSpec((tm * rpt, LANES), lambda b, i: (b * nt + i, 0)),
            pl.BlockSpec((tm, LANES), lambda b, i: (b * nt + i, 0)),
            pl.BlockSpec((tm, LANES), lambda b, i: (b * nt + i, 0)),
            pl.BlockSpec((1, LANES), lambda b, i: (0, 0)),
        ],
        out_shape=[
            jax.ShapeDtypeStruct((n * rpt, LANES), U32),
            jax.ShapeDtypeStruct((n, LANES), I32),
            jax.ShapeDtypeStruct((n, LANES), F32),
            jax.ShapeDtypeStruct((1, LANES), F32),
        ],
        scratch_shapes=[pltpu.VMEM((1, LANES), F32)],
        compiler_params=_params(("arbitrary", "arbitrary")),
        name="router",
    )(h, g.reshape(1, d), mod3, mod3, w_r, b_r)


def _expert_body(meta_ref, xs_ref, wg_ref, wu_ref, wd_ref, y_ref,
                 wgb_ref, wub_ref, wdb_ref, xbuf, ybuf, xsem, ysem, *, rpt):
    e = pl.program_id(0)
    blk0 = meta_ref[0, e]
    n_blk = meta_ref[0, e + 1] - blk0
    n_used = meta_ref[2, 0]
    dh = rpt * LANES
    tile_rows = MOE_BLOCK * rpt

    def block_rows(ref, g):
        return ref.at[pl.ds(pl.multiple_of(g * tile_rows, tile_rows), tile_rows), :]

    def read_in(g, slot):
        return pltpu.make_async_copy(block_rows(xs_ref, g), xbuf.at[slot], xsem.at[slot])

    def write_out(g):
        return pltpu.make_async_copy(ybuf, block_rows(y_ref, g), ysem)

    @pl.when(e == 0)
    def _():
        read_in(0, 0).start()

    @pl.when(n_blk > 0)
    def _():
        wgb_ref[...] = wg_ref[0, 0].astype(BF16)
        wub_ref[...] = wu_ref[0, 0].astype(BF16)
        wdb_ref[...] = wd_ref[0, 0].astype(BF16)

        def block(j, c):
            g = blk0 + j
            slot = g % 2

            @pl.when(g + 1 < n_used)
            def _():
                read_in(g + 1, 1 - slot).start(priority=1)

            read_in(g, slot).wait()
            x_lo, x_hi = _unpack_bf16_pairs(_load_token_tiles(xbuf.at[slot], MOE_BLOCK, rpt))
            x_lo = x_lo.astype(BF16)
            x_hi = x_hi.astype(BF16)
            gate = _dot(x_lo, wgb_ref[0:dh, :]) + _dot(x_hi, wgb_ref[dh:2 * dh, :])
            up = _dot(x_lo, wub_ref[0:dh, :]) + _dot(x_hi, wub_ref[dh:2 * dh, :])
            mid = (_silu(gate) * up).astype(BF16)
            out = _pack_bf16_pairs(_dot(mid, wdb_ref[...]))

            @pl.when(g > 0)
            def _():
                write_out(g - 1).wait()

            _store_token_tiles(ybuf, out)
            write_out(g).start(priority=1)
            return c

        lax.fori_loop(0, n_blk, block, 0)

    @pl.when(e == pl.num_programs(0) - 1)
    def _():
        write_out(n_used - 1).wait()


def _experts(meta, xs, w_gate, w_up, w_down, layer):
    _, n_exp, d, hid = w_gate.shape
    rpt = d // 2 // LANES
    n_slots = xs.shape[0] // rpt

    def wspec(shape):
        return pl.BlockSpec((1, 1) + shape, lambda e, m: (layer, e, 0, 0))

    return pl.pallas_call(
        functools.partial(_expert_body, rpt=rpt),
        grid_spec=pltpu.PrefetchScalarGridSpec(
            num_scalar_prefetch=1,
            grid=(n_exp,),
            in_specs=[pl.BlockSpec(memory_space=pl.ANY),
                      wspec((d, hid)), wspec((d, hid)), wspec((hid, d))],
            out_specs=pl.BlockSpec(memory_space=pl.ANY),
            scratch_shapes=[pltpu.VMEM((d, hid), BF16), pltpu.VMEM((d, hid), BF16),
                            pltpu.VMEM((hid, d), BF16),
                            pltpu.VMEM((2, MOE_BLOCK * rpt, LANES), U32),
                            pltpu.VMEM((MOE_BLOCK * rpt, LANES), U32),
                            pltpu.SemaphoreType.DMA((2,)), pltpu.SemaphoreType.DMA],
        ),
        out_shape=jax.ShapeDtypeStruct((n_slots * rpt, LANES), U32),
        compiler_params=_params(("arbitrary",)),
        name="moe_experts",
    )(meta, xs, w_gate, w_up, w_down)


def _sc_mesh():
    return plsc.VectorSubcoreMesh(core_axis_name="core", subcore_axis_name="subcore")


def _sc_scatter_rows_twice(src, idx_a, idx_b, n_out):
    n = src.shape[0]
    win = _pick(n, SC_GATHER_WINDOW)

    @pl.kernel(out_type=jax.ShapeDtypeStruct((n_out, LANES), src.dtype), mesh=_sc_mesh(),
               scratch_types=[])
    def scatter_kernel(s_hbm, ia_hbm, ib_hbm, o_hbm):
        def window(s_vmem, ia_vmem, ib_vmem):
            pltpu.sync_copy(s_vmem, o_hbm.at[ia_vmem.at[0]])
            pltpu.sync_copy(s_vmem, o_hbm.at[ib_vmem.at[0]])

        pltpu.emit_pipeline(
            window,
            grid=(n // win,),
            in_specs=[pl.BlockSpec((win, LANES), index_map=lambda i: (i, 0)),
                      pl.BlockSpec((1, win), index_map=lambda i: (0, i)),
                      pl.BlockSpec((1, win), index_map=lambda i: (0, i))],
            out_specs=[],
            core_axis_name=("core", "subcore"),
            dimension_semantics=(pltpu.PARALLEL,),
        )(s_hbm, ia_hbm, ib_hbm)

    return scatter_kernel(src, idx_a.reshape(1, n), idx_b.reshape(1, n))


def _sc_gather_rows_twice(table, idx_a, idx_b):
    n_idx = idx_a.shape[0]
    win = _pick(n_idx, SC_GATHER_WINDOW // 2)
    out = jax.ShapeDtypeStruct((n_idx, LANES), table.dtype)

    @pl.kernel(out_type=(out, out), mesh=_sc_mesh())
    def gather_kernel(t_hbm, ia_hbm, ib_hbm, oa_hbm, ob_hbm):
        def window(ia_vmem, ib_vmem, oa_vmem, ob_vmem):
            pltpu.sync_copy(t_hbm.at[ia_vmem.at[0]], oa_vmem)
            pltpu.sync_copy(t_hbm.at[ib_vmem.at[0]], ob_vmem)

        idx_spec = pl.BlockSpec((1, win), index_map=lambda i: (0, i))
        out_spec = pl.BlockSpec((win, LANES), index_map=lambda i: (i, 0))
        pltpu.emit_pipeline(
            window,
            grid=(n_idx // win,),
            in_specs=[idx_spec, idx_spec],
            out_specs=[out_spec, out_spec],
            core_axis_name=("core", "subcore"),
            dimension_semantics=(pltpu.PARALLEL,),
        )(ia_hbm, ib_hbm, oa_hbm, ob_hbm)

    return gather_kernel(table, idx_a.reshape(1, n_idx), idx_b.reshape(1, n_idx))


def _combine_body(h_ref, ya_ref, yb_ref, rw_ref, gt_ref, fg_ref, o_ref, *, rpt, final):
    tm = h_ref.shape[1]
    dh = rpt * LANES
    wa = rw_ref[:, 0:1]
    wb = rw_ref[:, 1:2]
    a_lo, a_hi = _unpack_bf16_pairs(_load_token_tiles(ya_ref, tm, rpt))
    b_lo, b_hi = _unpack_bf16_pairs(_load_token_tiles(yb_ref, tm, rpt))
    gt = gt_ref[0]
    lo = h_ref[0, :, 0:dh] + gt[:, 0:dh] * (wa * a_lo + wb * b_lo)
    hi = h_ref[0, :, dh:2 * dh] + gt[:, dh:2 * dh] * (wa * a_hi + wb * b_hi)
    if final:
        ms = (jnp.sum(lo * lo, axis=-1, keepdims=True)
              + jnp.sum(hi * hi, axis=-1, keepdims=True)) / (2 * dh)
        inv = lax.rsqrt(ms + EPS)
        lo = lo * inv * fg_ref[:, 0:dh]
        hi = hi * inv * fg_ref[:, dh:2 * dh]
    o_ref[0, :, 0:dh] = lo
    o_ref[0, :, dh:2 * dh] = hi


def _combine(h, ya, yb, rw, mod3, row_fn, sec_gate, final_g, final):
    bsz, t, d = h.shape
    tm = _pick(t, 256)
    nt = t // tm
    rpt = d // 2 // LANES
    return pl.pallas_call(
        functools.partial(_combine_body, rpt=rpt, final=final),
        grid=(bsz, nt),
        in_specs=[
            pl.BlockSpec((1, tm, d), lambda b, i: (b, i, 0)),
            pl.BlockSpec((tm * rpt, LANES), lambda b, i: (b * nt + i, 0)),
            pl.BlockSpec((tm * rpt, LANES), lambda b, i: (b * nt + i, 0)),
            pl.BlockSpec((tm, LANES), lambda b, i: (b * nt + i, 0)),
            _mod_spec(d, row_fn, sec_gate),
            pl.BlockSpec((1, d), lambda b, i: (0, 0)),
        ],
        out_specs=pl.BlockSpec((1, tm, d), lambda b, i: (b, i, 0)),
        out_shape=jax.ShapeDtypeStruct((bsz, t, d), F32),
        compiler_params=_params(("arbitrary", "arbitrary")),
        name="moe_combine",
    )(h, ya, yb, rw, mod3, final_g.reshape(1, d))


def _hier_moe(h, g, mod3, row_fn, w_r1, b_r1, w_r2, b_r2, w_gate, w_up, w_down, layer, final_g, final):
    bsz, t, d = h.shape
    n = bsz * t
    w_r = jnp.concatenate([jnp.transpose(w_r2, (1, 0, 2)).reshape(d, MOE_EXPERTS), w_r1], axis=1)
    w_r = jnp.pad(w_r, ((0, 0), (0, LANES - w_r.shape[1])))
    b_r = jnp.pad(jnp.concatenate([b_r2.reshape(-1), b_r1]), (0, LANES - MOE_EXPERTS - MOE_GROUPS))
    bp, ri, rw, cnt = _router(h, g, mod3, row_fn, 3, 4, w_r, b_r.reshape(1, LANES))

    counts = cnt[0, :MOE_EXPERTS].astype(I32)
    padded = (counts + MOE_BLOCK - 1) // MOE_BLOCK * MOE_BLOCK
    pad_end = jnp.cumsum(padded)
    pad_start = pad_end - padded
    n_slots = (-(-(n * 2) // MOE_BLOCK) + MOE_EXPERTS) * MOE_BLOCK
    blk_off = jnp.concatenate([pad_start, pad_end[-1:]]) // MOE_BLOCK
    row = lambda v: jnp.pad(v.astype(I32), (0, LANES - v.shape[0]))
    meta = jnp.stack([row(blk_off), row(counts), row(blk_off[-1:])])
    rpt = d // 2 // LANES
    tile_rows = jnp.arange(rpt, dtype=I32)
    rows_a = ((pad_start[ri[:, 0]] + ri[:, 2])[:, None] * rpt + tile_rows).reshape(-1)
    rows_b = ((pad_start[ri[:, 1]] + ri[:, 3])[:, None] * rpt + tile_rows).reshape(-1)

    xs = _sc_scatter_rows_twice(bp, rows_a, rows_b, n_slots * rpt)
    ys = _experts(meta, xs, w_gate, w_up, w_down, layer)
    ya, yb = _sc_gather_rows_twice(ys, rows_a, rows_b)
    return _combine(h, ya, yb, rw, mod3, row_fn, 5, final_g, final)


def kernel(x, c, ctx, c_ctx, ada_w, ada_b, norm1_g, norm2_g, hgrn_w_in, hgrn_lb, hgrn_onorm_g,
           hgrn_w_out, conv_w_pw1, conv_w_dw, conv_b_dw, conv_ln_g, conv_ln_b, conv_w_pw2,
           moe_w_r1, moe_b_r1, moe_w_r2, moe_b_r2, moe_w_gate, moe_w_up, moe_w_down, final_g):
    bsz, t, d = x.shape
    t_ctx = ctx.shape[1]
    depth = ada_w.shape[0]
    assert depth == 2 and bsz < MOD_ROWS and d % (2 * LANES) == 0 and t % GRID_W == 0

    c_all = jnp.concatenate([c, c_ctx[None, :], jnp.zeros((MOD_ROWS - bsz - 1, d), F32)], axis=0)
    mod = _adaln(c_all, ada_w, ada_b)
    mod3 = mod.reshape(depth * MOD_ROWS, 1, 6 * d)
    lat_row = lambda l: (lambda b: l * MOD_ROWS + b)
    ctx_row = lambda l: (lambda b: l * MOD_ROWS + bsz)
    lb_all = jnp.cumsum(jax.nn.softmax(hgrn_lb.astype(F32), axis=1), axis=1)

    def moe(h, l, final):
        return _hier_moe(h, norm2_g[l], mod3, lat_row(l), moe_w_r1[l], moe_b_r1[l], moe_w_r2[l],
                         moe_b_r2[l], moe_w_gate, moe_w_up, moe_w_down, l, final_g, final)

    a_lat = _norm_mod(x, norm1_g[0], mod3, lat_row(0), 0, 1)
    a_ctx = _norm_mod(ctx, norm1_g[0], mod3, ctx_row(0), 0, 1)
    p_lat = _matmul(a_lat.reshape(bsz * t, d), hgrn_w_in[0]).reshape(bsz, t, 5 * d)
    p_ctx = _matmul(a_ctx.reshape(bsz * t_ctx, d), hgrn_w_in[0]).reshape(bsz, t_ctx, 5 * d)
    o2 = _scan(p_ctx, p_lat, lb_all[:, 0])
    h = _hgrn_readout(o2, p_lat, hgrn_onorm_g[0], hgrn_w_out[0], x, mod3, lat_row(0), 2)
    h = moe(h, 0, False)

    a = _norm_mod(h, norm1_g[1], mod3, lat_row(1), 0, 1)
    u = _matmul_glu(a.reshape(bsz * t, d), conv_w_pw1[0]).reshape(bsz, t, d)
    v = _dwconv(u, conv_w_dw[0], conv_b_dw[0])
    h = _conv_out(v, conv_ln_g[0], conv_ln_b[0], conv_w_pw2[0], h, mod3, lat_row(1), 2)
    return moe(h, 1, True)
```

```python
import functools

import jax
import jax.numpy as jnp
from jax import lax
from jax.experimental import pallas as pl
from jax.experimental.pallas import tpu as pltpu
from jax.experimental.pallas import tpu_sc as plsc

F32 = jnp.float32
BF16 = jnp.bfloat16
U32 = jnp.uint32
I32 = jnp.int32

EPS = 1e-6
LANES = 128
SUBLANES = 8
HEAD_DIM = 128
SCAN_CHUNK = 64
GRID_W = 64
CONV_WIDTH = 31
CONV_REACH = (16 - CONV_WIDTH // 2 + CONV_WIDTH - 1) // 8 * 8
MOE_GROUPS = 8
MOE_PER_GROUP = 8
MOE_EXPERTS = MOE_GROUPS * MOE_PER_GROUP
MOE_BLOCK = 256
SC_GATHER_WINDOW = 256
MOD_ROWS = 16
VMEM_LIMIT = 56 * 1024 * 1024
SAFE_CHUNK_LOG_DECAY = -80.0
HI_MASK = 0xFFFF0000


def _pick(n, pref):
    t = pref
    while n % t:
        t //= 2
    return t


def _params(sem):
    return pltpu.CompilerParams(dimension_semantics=sem, vmem_limit_bytes=VMEM_LIMIT)


def _silu(x):
    return x * jax.nn.sigmoid(x)


def _dot(a, b):
    return jnp.dot(a, b, preferred_element_type=F32)


def _dot_nt(a, b):
    return lax.dot_general(a, b, (((1,), (1,)), ((), ())), preferred_element_type=F32)


def _dot_tn(a, b):
    return lax.dot_general(a, b, (((0,), (0,)), ((), ())), preferred_element_type=F32)


def _pack_bf16_pairs(x):
    n = x.shape[1] // 2
    bits = lax.bitcast_convert_type(x.astype(BF16).astype(F32), U32)
    return (bits[:, :n] >> 16) | (bits[:, n:] & jnp.uint32(HI_MASK))


def _unpack_bf16_pairs(w):
    lo = lax.bitcast_convert_type(w << 16, F32)
    hi = lax.bitcast_convert_type(w & jnp.uint32(HI_MASK), F32)
    return lo, hi


def _store_token_tiles(ref, w, first=0, stride=None):
    m, n = w.shape[0], w.shape[1] // LANES
    stride = n if stride is None else stride
    for s in range(n):
        ref[pl.ds(first + s, m, stride=stride), :] = w[:, s * LANES:(s + 1) * LANES]


def _load_token_tiles(ref, m, n, first=0, stride=None):
    stride = n if stride is None else stride
    return jnp.concatenate([ref[pl.ds(first + s, m, stride=stride), :] for s in range(n)], axis=1)


def _adaln_body(c_ref, w_ref, b_ref, o_ref):
    s = _silu(c_ref[...])
    o_ref[0] = _dot(s.astype(BF16), w_ref[0].astype(BF16)) + b_ref[0]


def _adaln(c_all, ada_w, ada_b):
    depth, d, n6 = ada_w.shape
    tn = _pick(n6, 1024)
    return pl.pallas_call(
        _adaln_body,
        grid=(depth, n6 // tn),
        in_specs=[
            pl.BlockSpec((MOD_ROWS, d), lambda l, j: (0, 0)),
            pl.BlockSpec((1, d, tn), lambda l, j: (l, 0, j)),
            pl.BlockSpec((1, 1, tn), lambda l, j: (l, 0, j)),
        ],
        out_specs=pl.BlockSpec((1, MOD_ROWS, tn), lambda l, j: (l, 0, j)),
        out_shape=jax.ShapeDtypeStruct((depth, MOD_ROWS, n6), F32),
        compiler_params=_params(("arbitrary", "arbitrary")),
        name="adaln",
    )(c_all, ada_w, ada_b.reshape(depth, 1, n6))


def _mod_spec(d, row_fn, sec):
    return pl.BlockSpec((1, 1, d), lambda b, i: (row_fn(b), 0, sec))


def _rms(x, g):
    return x * lax.rsqrt(jnp.mean(x * x, axis=-1, keepdims=True) + EPS) * g


def _norm_mod_body(x_ref, g_ref, sh_ref, sc_ref, o_ref):
    y = _rms(x_ref[0], g_ref[...])
    o_ref[0] = (y * (1.0 + sc_ref[0]) + sh_ref[0]).astype(o_ref.dtype)


def _norm_mod(x, g, mod3, row_fn, sec_shift, sec_scale):
    bsz, t, d = x.shape
    tm = _pick(t, 512)
    return pl.pallas_call(
        _norm_mod_body,
        grid=(bsz, t // tm),
        in_specs=[
            pl.BlockSpec((1, tm, d), lambda b, i: (b, i, 0)),
            pl.BlockSpec((1, d), lambda b, i: (0, 0)),
            _mod_spec(d, row_fn, sec_shift),
            _mod_spec(d, row_fn, sec_scale),
        ],
        out_specs=pl.BlockSpec((1, tm, d), lambda b, i: (b, i, 0)),
        out_shape=jax.ShapeDtypeStruct((bsz, t, d), BF16),
        compiler_params=_params(("arbitrary", "arbitrary")),
        name="norm_mod",
    )(x, g.reshape(1, d), mod3, mod3)


def _mm_body(a_ref, w_ref, o_ref, wb_ref):
    @pl.when(pl.program_id(1) == 0)
    def _():
        wb_ref[...] = w_ref[...].astype(BF16)

    o_ref[...] = _dot(a_ref[...], wb_ref[...]).astype(o_ref.dtype)


def _matmul(a, w):
    m, k = a.shape
    n = w.shape[1]
    tm, tn = _pick(m, 512), _pick(n, 1024)
    return pl.pallas_call(
        _mm_body,
        grid=(n // tn, m // tm),
        in_specs=[
            pl.BlockSpec((tm, k), lambda j, i: (i, 0)),
            pl.BlockSpec((k, tn), lambda j, i: (0, j)),
        ],
        out_specs=pl.BlockSpec((tm, tn), lambda j, i: (i, j)),
        out_shape=jax.ShapeDtypeStruct((m, n), BF16),
        scratch_shapes=[pltpu.VMEM((k, tn), BF16)],
        compiler_params=_params(("arbitrary", "arbitrary")),
        name="matmul",
    )(a, w)


def _mm_glu_body(a_ref, wv_ref, wg_ref, o_ref, wvb_ref, wgb_ref):
    @pl.when(pl.program_id(1) == 0)
    def _():
        wvb_ref[...] = wv_ref[...].astype(BF16)
        wgb_ref[...] = wg_ref[...].astype(BF16)

    a = a_ref[...]
    val = _dot(a, wvb_ref[...])
    gate = _dot(a, wgb_ref[...])
    o_ref[...] = (val * jax.nn.sigmoid(gate)).astype(o_ref.dtype)


def _matmul_glu(a, w):
    m, k = a.shape
    n = w.shape[1] // 2
    tm, tn = _pick(m, 512), _pick(n, 512)
    nj = n // tn
    return pl.pallas_call(
        _mm_glu_body,
        grid=(nj, m // tm),
        in_specs=[
            pl.BlockSpec((tm, k), lambda j, i: (i, 0)),
            pl.BlockSpec((k, tn), lambda j, i: (0, j)),
            pl.BlockSpec((k, tn), lambda j, i: (0, j + nj)),
        ],
        out_specs=pl.BlockSpec((tm, tn), lambda j, i: (i, j)),
        out_shape=jax.ShapeDtypeStruct((m, n), BF16),
        scratch_shapes=[pltpu.VMEM((k, tn), BF16), pltpu.VMEM((k, tn), BF16)],
        compiler_params=_params(("arbitrary", "arbitrary")),
        name="matmul_glu",
    )(a, w, w)


def _scan_body(qc_ref, vc_ref, zc_ref, ql_ref, vl_ref, zl_ref, lb_ref, o_ref,
               st_ref, flag_ref, b_sc, q_sc, k_sc, v_sc, *, nc, n_sub, heads):
    chunk = SCAN_CHUNK
    direction = pl.program_id(1)
    step = pl.program_id(2)
    fwd = direction == 0

    @pl.when(step == 0)
    def _():
        st_ref[0] = jnp.zeros(st_ref.shape[1:], F32)

    @pl.when(step > 0)
    def _():
        st_ref[0] = st_ref[n_sub]

    lb = lb_ref[0]
    row = lax.broadcasted_iota(I32, (chunk, chunk), 0)
    col = lax.broadcasted_iota(I32, (chunk, chunk), 1)
    sign = 1 - 2 * direction
    tri = (row - col) * sign >= 0
    tri_b = jnp.where(tri, 1.0, 0.0).astype(BF16)
    t_idx = lax.broadcasted_iota(I32, (chunk, 1), 0)

    def chunk_rows(ci):
        cidx = jnp.where(fwd, ci, n_sub - 1 - ci)
        return pl.ds(pl.multiple_of(cidx * chunk, chunk), chunk)

    def decay(z_ref, rows, sl):
        z = z_ref[0, rows, sl].astype(F32)
        f = lb[:, sl] + (1.0 - lb[:, sl]) * jax.nn.sigmoid(z)
        lf = jnp.log(f)
        hi = lf.astype(BF16)
        lo = (lf - hi.astype(F32)).astype(BF16)
        b = _dot(tri_b, hi) + _dot(tri_b, lo)
        tot = jnp.where(fwd, b[chunk - 1:chunk, :], b[0:1, :])
        return b, tot, 1.0 - f

    def fast_chunk(q_ref, v_ref, z_ref, ci, emit):
        rows = chunk_rows(ci)
        b, tot, kk = decay(z_ref, rows, slice(None))
        ke = kk * jnp.exp(tot - b)
        ke_b = ke.astype(BF16)
        etot = jnp.exp(tot)
        v = v_ref[0, rows, :]
        head_sl = [slice(h * HEAD_DIM, (h + 1) * HEAD_DIM) for h in range(heads)]
        st = [st_ref[ci, h] for h in range(heads)]
        kv = [_dot_tn(v[:, sl], ke_b[:, sl]) for sl in head_sl]
        if emit:
            qe = _silu(q_ref[0, rows, :].astype(F32)) * jnp.exp(b)
            qe_b = qe.astype(BF16)
            g = jnp.exp(-0.5 * tot)
            qh = (qe * g).astype(BF16)
            kh = (ke * g).astype(BF16)
            flag_ref[ci] = (jnp.min(tot) < SAFE_CHUNK_LOG_DECAY).astype(I32)
            sc = [_dot_nt(qh[:, sl], kh[:, sl]) for sl in head_sl]
            o_inter = [_dot_nt(qe_b[:, sl], st[h].astype(BF16)) for h, sl in enumerate(head_sl)]
            p = [jnp.where(tri, s, 0.0).astype(BF16) for s in sc]
            o_intra = [_dot(p[h], v[:, sl]) for h, sl in enumerate(head_sl)]
            for h, sl in enumerate(head_sl):
                o_ref[0, 0, rows, sl] = (o_inter[h] + o_intra[h]).astype(o_ref.dtype)
        for h, sl in enumerate(head_sl):
            st_ref[ci + 1, h] = st[h] * etot[:, sl] + kv[h]

    def exact_chunk(q_ref, v_ref, z_ref, ci):
        rows = chunk_rows(ci)
        b, _, kk = decay(z_ref, rows, slice(None))
        b_sc[...] = b
        q_sc[...] = _silu(q_ref[0, rows, :].astype(F32))
        k_sc[...] = kk
        v_sc[...] = v_ref[0, rows, :].astype(F32)

        def head_loop(h, c2):
            lanes = pl.ds(pl.multiple_of(h * HEAD_DIM, HEAD_DIM), HEAD_DIM)
            b_h = b_sc[:, lanes]
            q_h = q_sc[:, lanes]
            k_h = k_sc[:, lanes]
            v_h = v_sc[:, lanes]
            o_inter = _dot_nt((q_h * jnp.exp(b_h)).astype(BF16), st_ref[ci, h].astype(BF16))

            def key_loop(s, acc):
                pick = t_idx == s
                b_s = jnp.sum(jnp.where(pick, b_h, 0.0), axis=0, keepdims=True)
                k_s = jnp.sum(jnp.where(pick, k_h, 0.0), axis=0, keepdims=True)
                v_s = jnp.sum(jnp.where(pick, v_h, 0.0), axis=0, keepdims=True)
                valid = (t_idx - s) * sign >= 0
                dec = jnp.where(valid, jnp.exp(jnp.where(valid, b_h - b_s, 0.0)), 0.0)
                w = jnp.sum(q_h * dec * k_s, axis=1, keepdims=True)
                return acc + w * v_s

            o_intra = lax.fori_loop(0, chunk, key_loop, jnp.zeros((chunk, HEAD_DIM), F32))
            o_ref[0, 0, rows, lanes] = (o_inter + o_intra).astype(o_ref.dtype)
            return c2

        lax.fori_loop(0, heads, head_loop, 0)

    @pl.when(step < nc)
    def _():
        for ci in range(n_sub):
            fast_chunk(qc_ref, vc_ref, zc_ref, ci, False)

    @pl.when(step >= nc)
    def _():
        for ci in range(n_sub):
            fast_chunk(ql_ref, vl_ref, zl_ref, ci, True)

        def redo(ci, c):
            @pl.when(flag_ref[ci] == 1)
            def _():
                exact_chunk(ql_ref, vl_ref, zl_ref, ci)
            return c

        lax.fori_loop(0, n_sub, redo, 0)


def _scan(p_ctx, p_lat, lb):
    bsz, t_lat, d5 = p_lat.shape
    t_ctx = p_ctx.shape[1]
    d = d5 // 5
    heads = d // HEAD_DIM
    tb = _pick(t_ctx, 256)
    assert t_lat % tb == 0 and tb % SCAN_CHUNK == 0
    nc, nl = t_ctx // tb, t_lat // tb
    n_sub = tb // SCAN_CHUNK

    def ctx_blk(dr, s):
        sc = jnp.minimum(s, nc - 1)
        return jnp.where(dr == 0, sc, nc - 1 - sc)

    def lat_blk(dr, s):
        sl = jnp.maximum(s - nc, 0)
        return jnp.where(dr == 0, sl, nl - 1 - sl)

    def cspec(sec_fn):
        return pl.BlockSpec((1, tb, d), lambda b, dr, s: (b, ctx_blk(dr, s), sec_fn(dr)))

    def lspec(sec_fn):
        return pl.BlockSpec((1, tb, d), lambda b, dr, s: (b, lat_blk(dr, s), sec_fn(dr)))

    q_sec = lambda dr: 0
    v_sec = lambda dr: 1
    z_sec = lambda dr: 2 + dr
    return pl.pallas_call(
        functools.partial(_scan_body, nc=nc, n_sub=n_sub, heads=heads),
        grid=(bsz, 2, nc + nl),
        in_specs=[cspec(q_sec), cspec(v_sec), cspec(z_sec),
                  lspec(q_sec), lspec(v_sec), lspec(z_sec),
                  pl.BlockSpec((1, 1, d), lambda b, dr, s: (dr, 0, 0))],
        out_specs=pl.BlockSpec((1, 1, tb, d), lambda b, dr, s: (dr, b, lat_blk(dr, s), 0)),
        out_shape=jax.ShapeDtypeStruct((2, bsz, t_lat, d), BF16),
        scratch_shapes=[pltpu.VMEM((n_sub + 1, heads, HEAD_DIM, HEAD_DIM), F32),
                        pltpu.SMEM((n_sub,), I32),
                        pltpu.VMEM((SCAN_CHUNK, d), F32),
                        pltpu.VMEM((SCAN_CHUNK, d), F32),
                        pltpu.VMEM((SCAN_CHUNK, d), F32),
                        pltpu.VMEM((SCAN_CHUNK, d), F32)],
        compiler_params=_params(("arbitrary", "arbitrary", "arbitrary")),
        name="hgrn_scan",
    )(p_ctx, p_ctx, p_ctx, p_lat, p_lat, p_lat, lb.reshape(2, 1, d))


def _hgrn_readout_body(of_ref, ob_ref, g_ref, on_ref, w_ref, h_ref, gt_ref, o_ref,
                       wb_ref, a_ref, *, heads):
    @pl.when((pl.program_id(0) == 0) & (pl.program_id(1) == 0))
    def _():
        wb_ref[...] = w_ref[...].astype(BF16)

    on = on_ref[...]
    for h in range(heads):
        sl = slice(h * HEAD_DIM, (h + 1) * HEAD_DIM)
        o = of_ref[0, 0, :, sl].astype(F32) + ob_ref[0, 0, :, sl].astype(F32)
        o = _rms(o, on)
        a_ref[:, sl] = (o * _silu(g_ref[0, :, sl].astype(F32))).astype(BF16)
    y = _dot(a_ref[...], wb_ref[...])
    o_ref[0] = h_ref[0] + gt_ref[0] * y


def _hgrn_readout(o2, p_lat, onorm_g, w_out, h, mod3, row_fn, sec_gate):
    bsz, t, d = h.shape
    heads = d // HEAD_DIM
    tm = _pick(t, 256)
    return pl.pallas_call(
        functools.partial(_hgrn_readout_body, heads=heads),
        grid=(bsz, t // tm),
        in_specs=[
            pl.BlockSpec((1, 1, tm, d), lambda b, i: (0, b, i, 0)),
            pl.BlockSpec((1, 1, tm, d), lambda b, i: (1, b, i, 0)),
            pl.BlockSpec((1, tm, d), lambda b, i: (b, i, 4)),
            pl.BlockSpec((1, HEAD_DIM), lambda b, i: (0, 0)),
            pl.BlockSpec((d, d), lambda b, i: (0, 0), pipeline_mode=pl.Buffered(1)),
            pl.BlockSpec((1, tm, d), lambda b, i: (b, i, 0)),
            _mod_spec(d, row_fn, sec_gate),
        ],
        out_specs=pl.BlockSpec((1, tm, d), lambda b, i: (b, i, 0)),
        out_shape=jax.ShapeDtypeStruct((bsz, t, d), F32),
        scratch_shapes=[pltpu.VMEM((d, d), BF16), pltpu.VMEM((tm, d), BF16)],
        compiler_params=_params(("arbitrary", "arbitrary")),
        name="hgrn_readout",
    )(o2, o2, p_lat, onorm_g.reshape(1, HEAD_DIM), w_out, h, mod3)


def _conv_out_body(v_ref, lg_ref, lbias_ref, w_ref, h_ref, gt_ref, o_ref, wb_ref):
    @pl.when((pl.program_id(0) == 0) & (pl.program_id(1) == 0))
    def _():
        wb_ref[...] = w_ref[...].astype(BF16)

    x = v_ref[0].astype(F32)
    mu = jnp.mean(x, axis=-1, keepdims=True)
    xc = x - mu
    y = xc * lax.rsqrt(jnp.mean(xc * xc, axis=-1, keepdims=True) + EPS)
    y = _silu(y * lg_ref[...] + lbias_ref[...])
    o_ref[0] = h_ref[0] + gt_ref[0] * _dot(y.astype(BF16), wb_ref[...])


def _conv_out(v, ln_g, ln_b, w, h, mod3, row_fn, sec_gate):
    bsz, t, d = h.shape
    tm = _pick(t, 256)
    return pl.pallas_call(
        _conv_out_body,
        grid=(bsz, t // tm),
        in_specs=[
            pl.BlockSpec((1, tm, d), lambda b, i: (b, i, 0)),
            pl.BlockSpec((1, d), lambda b, i: (0, 0)),
            pl.BlockSpec((1, d), lambda b, i: (0, 0)),
            pl.BlockSpec((d, d), lambda b, i: (0, 0), pipeline_mode=pl.Buffered(1)),
            pl.BlockSpec((1, tm, d), lambda b, i: (b, i, 0)),
            _mod_spec(d, row_fn, sec_gate),
        ],
        out_specs=pl.BlockSpec((1, tm, d), lambda b, i: (b, i, 0)),
        out_shape=jax.ShapeDtypeStruct((bsz, t, d), F32),
        scratch_shapes=[pltpu.VMEM((d, d), BF16)],
        compiler_params=_params(("arbitrary", "arbitrary")),
        name="conv_out",
    )(v, ln_g.reshape(1, d), ln_b.reshape(1, d), w, h, mod3)


def _dwconv_body(u_ref, w_ref, b_ref, o_ref, pad_ref, shift_ref, *, rows, cb, half_blocks):
    half = CONV_WIDTH // 2
    j = pl.program_id(1)
    n_lane = cb // LANES
    bias = b_ref[...]

    def taps(read, r):
        for lb in range(n_lane):
            ln = slice(lb * LANES, (lb + 1) * LANES)
            acc = jnp.zeros((GRID_W, LANES), F32)
            for k in range(CONV_WIDTH):
                acc = acc + w_ref[k:k + 1, ln] * read(r, k, ln)
            o_ref[0, pl.ds(pl.multiple_of(r * GRID_W, GRID_W), GRID_W), ln] = (
                acc + bias[:, ln]).astype(o_ref.dtype)

    @pl.when(j < half_blocks)
    def _():
        zeros = jnp.zeros((rows, 16, cb), F32)
        pad_ref[0:rows, 0:16, :] = zeros
        pad_ref[0:rows, 16 + GRID_W:32 + GRID_W, :] = zeros
        pad_ref[0:rows, 16:16 + GRID_W, :] = u_ref[0].astype(F32).reshape(rows, GRID_W, cb)

        span = GRID_W + CONV_REACH

        def row_loop(r, c):
            for s in range(1, SUBLANES):
                shift_ref[s - 1] = pad_ref[r, s:s + span, :]

            def read(r_, k, ln):
                a, s = divmod(16 - half + k, SUBLANES)
                if s == 0:
                    return pad_ref[r_, a * SUBLANES:a * SUBLANES + GRID_W, ln]
                return shift_ref[s - 1, a * SUBLANES:a * SUBLANES + GRID_W, ln]

            taps(read, r)
            return c

        lax.fori_loop(0, rows, row_loop, 0)

    @pl.when(j >= half_blocks)
    def _():
        zeros = jnp.zeros((half, GRID_W, cb), F32)
        pad_ref[0:half, 0:GRID_W, :] = zeros
        pad_ref[half + rows:2 * half + rows, 0:GRID_W, :] = zeros
        pad_ref[half:half + rows, 0:GRID_W, :] = u_ref[0].astype(F32).reshape(rows, GRID_W, cb)

        def row_loop(r, c):
            taps(lambda r_, k, ln: pad_ref[r_ + k, 0:GRID_W, ln], r)
            return c

        lax.fori_loop(0, rows, row_loop, 0)


def _dwconv(u, w_dw, b_dw):
    bsz, t, d = u.shape
    rows = t // GRID_W
    cb = _pick(d // 2, 256)
    half_blocks = (d // 2) // cb
    return pl.pallas_call(
        functools.partial(_dwconv_body, rows=rows, cb=cb, half_blocks=half_blocks),
        grid=(bsz, d // cb),
        in_specs=[
            pl.BlockSpec((1, t, cb), lambda b, j: (b, 0, j)),
            pl.BlockSpec((CONV_WIDTH, cb), lambda b, j: (0, j)),
            pl.BlockSpec((1, cb), lambda b, j: (0, j)),
        ],
        out_specs=pl.BlockSpec((1, t, cb), lambda b, j: (b, 0, j)),
        out_shape=jax.ShapeDtypeStruct((bsz, t, d), BF16),
        scratch_shapes=[pltpu.VMEM((rows + 2 * (CONV_WIDTH // 2) + 2, GRID_W + 32, cb), F32),
                        pltpu.VMEM((SUBLANES - 1, GRID_W + CONV_REACH, cb), F32)],
        compiler_params=_params(("arbitrary", "arbitrary")),
        name="dwconv",
    )(u, w_dw, b_dw.reshape(1, d))


def _router_body(h_ref, g_ref, sh_ref, sc_ref, wr_ref, br_ref,
                 bp_ref, ri_ref, rw_ref, cnt_ref, carry_ref):
    tm = h_ref.shape[1]

    @pl.when((pl.program_id(0) == 0) & (pl.program_id(1) == 0))
    def _():
        carry_ref[...] = jnp.zeros_like(carry_ref)

    bmod = _rms(h_ref[0], g_ref[...]) * (1.0 + sc_ref[0]) + sh_ref[0]
    _store_token_tiles(bp_ref, _pack_bf16_pairs(bmod))

    b_hi = bmod.astype(BF16)
    b_lo = (bmod - b_hi.astype(F32)).astype(BF16)
    w = wr_ref[...]
    w_hi = w.astype(BF16)
    w_lo = (w - w_hi.astype(F32)).astype(BF16)
    logits = _dot(b_hi, w_hi) + _dot(b_hi, w_lo) + _dot(b_lo, w_hi) + br_ref[...]

    lane_i = lax.broadcasted_iota(I32, (tm, LANES), 1)
    lane = lane_i.astype(F32)
    neg = jnp.float32(-jnp.inf)
    big = jnp.float32(1 << 20)
    is_grp = (lane_i >= MOE_EXPERTS) & (lane_i < MOE_EXPERTS + MOE_GROUPS)
    m1 = jnp.max(jnp.where(is_grp, logits, neg), axis=1, keepdims=True)
    grp = jnp.min(jnp.where(is_grp & (logits == m1), lane - MOE_EXPERTS, big), axis=1, keepdims=True)
    pg = 1.0 / jnp.sum(jnp.where(is_grp, jnp.exp(logits - m1), 0.0), axis=1, keepdims=True)

    sel = (lane >= grp * MOE_PER_GROUP) & (lane < (grp + 1) * MOE_PER_GROUP)
    m2 = jnp.max(jnp.where(sel, logits, neg), axis=1, keepdims=True)
    ea = jnp.min(jnp.where(sel & (logits == m2), lane, big), axis=1, keepdims=True)
    sel2 = sel & (lane != ea)
    m3 = jnp.max(jnp.where(sel2, logits, neg), axis=1, keepdims=True)
    eb = jnp.min(jnp.where(sel2 & (logits == m3), lane, big), axis=1, keepdims=True)
    tr = jnp.exp(m3 - m2)
    wa = pg / (1.0 + tr)
    wb = wa * tr

    hit_a = lane == ea
    hit_b = lane == eb
    onehot = jnp.where(hit_a | hit_b, 1.0, 0.0)
    r_i = lax.broadcasted_iota(I32, (tm, tm), 0)
    c_i = lax.broadcasted_iota(I32, (tm, tm), 1)
    lower = jnp.where(r_i > c_i, 1.0, 0.0).astype(BF16)
    before = _dot(lower, onehot.astype(BF16)) + carry_ref[...]
    rank_a = jnp.sum(jnp.where(hit_a, before, 0.0), axis=1, keepdims=True).astype(I32)
    rank_b = jnp.sum(jnp.where(hit_b, before, 0.0), axis=1, keepdims=True).astype(I32)
    carry = carry_ref[...] + jnp.sum(onehot, axis=0, keepdims=True)
    carry_ref[...] = carry
    cnt_ref[...] = carry

    ri_ref[...] = jnp.where(lane_i == 0, ea.astype(I32), jnp.where(lane_i == 1, eb.astype(I32),
                            jnp.where(lane_i == 2, rank_a, jnp.where(lane_i == 3, rank_b, 0))))
    rw_ref[...] = jnp.where(lane_i == 0, wa, jnp.where(lane_i == 1, wb, 0.0))


def _router(h, g, mod3, row_fn, sec_shift, sec_scale, w_r, b_r):
    bsz, t, d = h.shape
    tm = _pick(t, 256)
    nt = t // tm
    n = bsz * t
    rpt = d // 2 // LANES
    return pl.pallas_call(
        _router_body,
        grid=(bsz, nt),
        in_specs=[
            pl.BlockSpec((1, tm, d), lambda b, i: (b, i, 0)),
            pl.BlockSpec((1, d), lambda b, i: (0, 0)),
            _mod_spec(d, row_fn, sec_shift),
            _mod_spec(d, row_fn, sec_scale),
            pl.BlockSpec((d, LANES), lambda b, i: (0, 0)),
            pl.BlockSpec((1, LANES), lambda b, i: (0, 0)),
        ],
        out_specs=[
            pl.BlockSpec((tm * rpt, LANES), lambda b, i: (b * nt + i, 0)),
            pl.BlockSpec((tm, LANES), lambda b, i: (b * nt + i, 0)),
            pl.BlockSpec((tm, LANES), lambda b, i: (b * nt + i, 0)),
            pl.BlockSpec((1, LANES), lambda b, i: (0, 0)),
        ],
        out_shape=[
            jax.ShapeDtypeStruct((n * rpt, LANES), U32),
            jax.ShapeDtypeStruct((n, LANES), I32),
            jax.ShapeDtypeStruct((n, LANES), F32),
            jax.ShapeDtypeStruct((1, LANES), F32),
        ],
        scratch_shapes=[pltpu.VMEM((1, LANES), F32)],
        compiler_params=_params(("arbitrary", "arbitrary")),
        name="router",
    )(h, g.reshape(1, d), mod3, mod3, w_r, b_r)


def _expert_body(meta_ref, xs_ref, wg_ref, wu_ref, wd_ref, y_ref,
                 wgb_ref, wub_ref, wdb_ref, xbuf, ybuf, xsem, ysem, *, rpt):
    e = pl.program_id(0)
    blk0 = meta_ref[0, e]
    n_blk = meta_ref[0, e + 1] - blk0
    n_used = meta_ref[2, 0]
    dh = rpt * LANES
    tile_rows = MOE_BLOCK * rpt

    def block_rows(ref, g):
        return ref.at[pl.ds(pl.multiple_of(g * tile_rows, tile_rows), tile_rows), :]

    def read_in(g, slot):
        return pltpu.make_async_copy(block_rows(xs_ref, g), xbuf.at[slot], xsem.at[slot])

    def write_out(g):
        return pltpu.make_async_copy(ybuf, block_rows(y_ref, g), ysem)

    @pl.when(e == 0)
    def _():
        read_in(0, 0).start()

    @pl.when(n_blk > 0)
    def _():
        wgb_ref[...] = wg_ref[0, 0].astype(BF16)
        wub_ref[...] = wu_ref[0, 0].astype(BF16)
        wdb_ref[...] = wd_ref[0, 0].astype(BF16)

        def block(j, c):
            g = blk0 + j
            slot = g % 2

            @pl.when(g + 1 < n_used)
            def _():
                read_in(g + 1, 1 - slot).start(priority=1)

            read_in(g, slot).wait()
            x_lo, x_hi = _unpack_bf16_pairs(_load_token_tiles(xbuf.at[slot], MOE_BLOCK, rpt))
            x_lo = x_lo.astype(BF16)
            x_hi = x_hi.astype(BF16)
            gate = _dot(x_lo, wgb_ref[0:dh, :]) + _dot(x_hi, wgb_ref[dh:2 * dh, :])
            up = _dot(x_lo, wub_ref[0:dh, :]) + _dot(x_hi, wub_ref[dh:2 * dh, :])
            mid = (_silu(gate) * up).astype(BF16)
            out = _pack_bf16_pairs(_dot(mid, wdb_ref[...]))

            @pl.when(g > 0)
            def _():
                write_out(g - 1).wait()

            _store_token_tiles(ybuf, out)
            write_out(g).start(priority=1)
            return c

        lax.fori_loop(0, n_blk, block, 0)

    @pl.when(e == pl.num_programs(0) - 1)
    def _():
        write_out(n_used - 1).wait()


def _experts(meta, xs, w_gate, w_up, w_down, layer):
    _, n_exp, d, hid = w_gate.shape
    rpt = d // 2 // LANES
    n_slots = xs.shape[0] // rpt

    def wspec(shape):
        return pl.BlockSpec((1, 1) + shape, lambda e, m: (layer, e, 0, 0))

    return pl.pallas_call(
        functools.partial(_expert_body, rpt=rpt),
        grid_spec=pltpu.PrefetchScalarGridSpec(
            num_scalar_prefetch=1,
            grid=(n_exp,),
            in_specs=[pl.BlockSpec(memory_space=pl.ANY),
                      wspec((d, hid)), wspec((d, hid)), wspec((hid, d))],
            out_specs=pl.BlockSpec(memory_space=pl.ANY),
            scratch_shapes=[pltpu.VMEM((d, hid), BF16), pltpu.VMEM((d, hid), BF16),
                            pltpu.VMEM((hid, d), BF16),
                            pltpu.VMEM((2, MOE_BLOCK * rpt, LANES), U32),
                            pltpu.VMEM((MOE_BLOCK * rpt, LANES), U32),
                            pltpu.SemaphoreType.DMA((2,)), pltpu.SemaphoreType.DMA],
        ),
        out_shape=jax.ShapeDtypeStruct((n_slots * rpt, LANES), U32),
        compiler_params=_params(("arbitrary",)),
        name="moe_experts",
    )(meta, xs, w_gate, w_up, w_down)


def _sc_mesh():
    return plsc.VectorSubcoreMesh(core_axis_name="core", subcore_axis_name="subcore")


def _sc_scatter_rows_twice(src, idx_a, idx_b, n_out):
    n = src.shape[0]
    win = _pick(n, SC_GATHER_WINDOW)

    @pl.kernel(out_type=jax.ShapeDtypeStruct((n_out, LANES), src.dtype), mesh=_sc_mesh(),
               scratch_types=[])
    def scatter_kernel(s_hbm, ia_hbm, ib_hbm, o_hbm):
        def window(s_vmem, ia_vmem, ib_vmem):
            pltpu.sync_copy(s_vmem, o_hbm.at[ia_vmem.at[0]])
            pltpu.sync_copy(s_vmem, o_hbm.at[ib_vmem.at[0]])

        pltpu.emit_pipeline(
            window,
            grid=(n // win,),
            in_specs=[pl.BlockSpec((win, LANES), index_map=lambda i: (i, 0)),
                      pl.BlockSpec((1, win), index_map=lambda i: (0, i)),
                      pl.BlockSpec((1, win), index_map=lambda i: (0, i))],
            out_specs=[],
            core_axis_name=("core", "subcore"),
            dimension_semantics=(pltpu.PARALLEL,),
        )(s_hbm, ia_hbm, ib_hbm)

    return scatter_kernel(src, idx_a.reshape(1, n), idx_b.reshape(1, n))


def _sc_gather_rows_twice(table, idx_a, idx_b):
    n_idx = idx_a.shape[0]
    win = _pick(n_idx, SC_GATHER_WINDOW // 2)
    out = jax.ShapeDtypeStruct((n_idx, LANES), table.dtype)

    @pl.kernel(out_type=(out, out), mesh=_sc_mesh())
    def gather_kernel(t_hbm, ia_hbm, ib_hbm, oa_hbm, ob_hbm):
        def window(ia_vmem, ib_vmem, oa_vmem, ob_vmem):
            pltpu.sync_copy(t_hbm.at[ia_vmem.at[0]], oa_vmem)
            pltpu.sync_copy(t_hbm.at[ib_vmem.at[0]], ob_vmem)

        idx_spec = pl.BlockSpec((1, win), index_map=lambda i: (0, i))
        out_spec = pl.BlockSpec((win, LANES), index_map=lambda i: (i, 0))
        pltpu.emit_pipeline(
            window,
            grid=(n_idx // win,),
            in_specs=[idx_spec, idx_spec],
            out_specs=[out_spec, out_spec],
            core_axis_name=("core", "subcore"),
            dimension_semantics=(pltpu.PARALLEL,),
        )(ia_hbm, ib_hbm, oa_hbm, ob_hbm)

    return gather_kernel(table, idx_a.reshape(1, n_idx), idx_b.reshape(1, n_idx))


def _combine_body(h_ref, ya_ref, yb_ref, rw_ref, gt_ref, fg_ref, o_ref, *, rpt, final):
    tm = h_ref.shape[1]
    dh = rpt * LANES
    wa = rw_ref[:, 0:1]
    wb = rw_ref[:, 1:2]
    a_lo, a_hi = _unpack_bf16_pairs(_load_token_tiles(ya_ref, tm, rpt))
    b_lo, b_hi = _unpack_bf16_pairs(_load_token_tiles(yb_ref, tm, rpt))
    gt = gt_ref[0]
    lo = h_ref[0, :, 0:dh] + gt[:, 0:dh] * (wa * a_lo + wb * b_lo)
    hi = h_ref[0, :, dh:2 * dh] + gt[:, dh:2 * dh] * (wa * a_hi + wb * b_hi)
    if final:
        ms = (jnp.sum(lo * lo, axis=-1, keepdims=True)
              + jnp.sum(hi * hi, axis=-1, keepdims=True)) / (2 * dh)
        inv = lax.rsqrt(ms + EPS)
        lo = lo * inv * fg_ref[:, 0:dh]
        hi = hi * inv * fg_ref[:, dh:2 * dh]
    o_ref[0, :, 0:dh] = lo
    o_ref[0, :, dh:2 * dh] = hi


def _combine(h, ya, yb, rw, mod3, row_fn, sec_gate, final_g, final):
    bsz, t, d = h.shape
    tm = _pick(t, 256)
    nt = t // tm
    rpt = d // 2 // LANES
    return pl.pallas_call(
        functools.partial(_combine_body, rpt=rpt, final=final),
        grid=(bsz, nt),
        in_specs=[
            pl.BlockSpec((1, tm, d), lambda b, i: (b, i, 0)),
            pl.BlockSpec((tm * rpt, LANES), lambda b, i: (b * nt + i, 0)),
            pl.BlockSpec((tm * rpt, LANES), lambda b, i: (b * nt + i, 0)),
            pl.BlockSpec((tm, LANES), lambda b, i: (b * nt + i, 0)),
            _mod_spec(d, row_fn, sec_gate),
            pl.BlockSpec((1, d), lambda b, i: (0, 0)),
        ],
        out_specs=pl.BlockSpec((1, tm, d), lambda b, i: (b, i, 0)),
        out_shape=jax.ShapeDtypeStruct((bsz, t, d), F32),
        compiler_params=_params(("arbitrary", "arbitrary")),
        name="moe_combine",
    )(h, ya, yb, rw, mod3, final_g.reshape(1, d))


def _slot_rows_body(ri_ref, start_ref, o_ref, *, rpt):
    tm = ri_ref.shape[0]
    ri = ri_ref[...].astype(F32)
    start = start_ref[...].astype(F32)
    lane = lax.broadcasted_iota(I32, (tm, LANES), 1).astype(F32)

    def first_row(e, rank):
        s = jnp.sum(jnp.where(lane == e, start, 0.0), axis=1, keepdims=True)
        return (s + rank) * rpt

    row_a = first_row(ri[:, 0:1], ri[:, 2:3])
    row_b = first_row(ri[:, 1:2], ri[:, 3:4])
    rows = jnp.where(lane < rpt, row_a + lane, jnp.where(lane < 2 * rpt, row_b + lane - rpt, 0.0))
    o_ref[...] = rows.astype(I32)


def _slot_rows(ri, start, rpt):
    n = ri.shape[0]
    tm = _pick(n, 512)
    return pl.pallas_call(
        functools.partial(_slot_rows_body, rpt=rpt),
        grid=(n // tm,),
        in_specs=[pl.BlockSpec((tm, LANES), lambda i: (i, 0)),
                  pl.BlockSpec((1, LANES), lambda i: (0, 0))],
        out_specs=pl.BlockSpec((tm, LANES), lambda i: (i, 0)),
        out_shape=jax.ShapeDtypeStruct((n, LANES), I32),
        compiler_params=_params(("arbitrary",)),
        name="slot_rows",
    )(ri, start)


def _hier_moe(h, g, mod3, row_fn, w_r1, b_r1, w_r2, b_r2, w_gate, w_up, w_down, layer, final_g, final):
    bsz, t, d = h.shape
    n = bsz * t
    w_r = jnp.concatenate([jnp.transpose(w_r2, (1, 0, 2)).reshape(d, MOE_EXPERTS), w_r1], axis=1)
    w_r = jnp.pad(w_r, ((0, 0), (0, LANES - w_r.shape[1])))
    b_r = jnp.pad(jnp.concatenate([b_r2.reshape(-1), b_r1]), (0, LANES - MOE_EXPERTS - MOE_GROUPS))
    bp, ri, rw, cnt = _router(h, g, mod3, row_fn, 3, 4, w_r, b_r.reshape(1, LANES))

    counts = cnt[0, :MOE_EXPERTS].astype(I32)
    padded = (counts + MOE_BLOCK - 1) // MOE_BLOCK * MOE_BLOCK
    pad_end = jnp.cumsum(padded)
    pad_start = pad_end - padded
    n_slots = (-(-(n * 2) // MOE_BLOCK) + MOE_EXPERTS) * MOE_BLOCK
    blk_off = jnp.concatenate([pad_start, pad_end[-1:]]) // MOE_BLOCK
    row = lambda v: jnp.pad(v.astype(I32), (0, LANES - v.shape[0]))
    meta = jnp.stack([row(blk_off), row(counts), row(blk_off[-1:])])
    rpt = d // 2 // LANES
    rows = _slot_rows(ri, row(pad_start).reshape(1, LANES), rpt)
    rows_a = rows[:, 0:rpt].reshape(-1)
    rows_b = rows[:, rpt:2 * rpt].reshape(-1)

    xs = _sc_scatter_rows_twice(bp, rows_a, rows_b, n_slots * rpt)
    ys = _experts(meta, xs, w_gate, w_up, w_down, layer)
    ya, yb = _sc_gather_rows_twice(ys, rows_a, rows_b)
    return _combine(h, ya, yb, rw, mod3, row_fn, 5, final_g, final)


def kernel(x, c, ctx, c_ctx, ada_w, ada_b, norm1_g, norm2_g, hgrn_w_in, hgrn_lb, hgrn_onorm_g,
           hgrn_w_out, conv_w_pw1, conv_w_dw, conv_b_dw, conv_ln_g, conv_ln_b, conv_w_pw2,
           moe_w_r1, moe_b_r1, moe_w_r2, moe_b_r2, moe_w_gate, moe_w_up, moe_w_down, final_g):
    bsz, t, d = x.shape
    t_ctx = ctx.shape[1]
    depth = ada_w.shape[0]
    assert depth == 2 and bsz < MOD_ROWS and d % (2 * LANES) == 0 and t % GRID_W == 0

    c_all = jnp.concatenate([c, c_ctx[None, :], jnp.zeros((MOD_ROWS - bsz - 1, d), F32)], axis=0)
    mod = _adaln(c_all, ada_w, ada_b)
    mod3 = mod.reshape(depth * MOD_ROWS, 1, 6 * d)
    lat_row = lambda l: (lambda b: l * MOD_ROWS + b)
    ctx_row = lambda l: (lambda b: l * MOD_ROWS + bsz)
    lb_all = jnp.cumsum(jax.nn.softmax(hgrn_lb.astype(F32), axis=1), axis=1)

    def moe(h, l, final):
        return _hier_moe(h, norm2_g[l], mod3, lat_row(l), moe_w_r1[l], moe_b_r1[l], moe_w_r2[l],
                         moe_b_r2[l], moe_w_gate, moe_w_up, moe_w_down, l, final_g, final)

    a_lat = _norm_mod(x, norm1_g[0], mod3, lat_row(0), 0, 1)
    a_ctx = _norm_mod(ctx, norm1_g[0], mod3, ctx_row(0), 0, 1)
    p_lat = _matmul(a_lat.reshape(bsz * t, d), hgrn_w_in[0]).reshape(bsz, t, 5 * d)
    p_ctx = _matmul(a_ctx.reshape(bsz * t_ctx, d), hgrn_w_in[0]).reshape(bsz, t_ctx, 5 * d)
    o2 = _scan(p_ctx, p_lat, lb_all[:, 0])
    h = _hgrn_readout(o2, p_lat, hgrn_onorm_g[0], hgrn_w_out[0], x, mod3, lat_row(0), 2)
    h = moe(h, 0, False)

    a = _norm_mod(h, norm1_g[1], mod3, lat_row(1), 0, 1)
    u = _matmul_glu(a.reshape(bsz * t, d), conv_w_pw1[0]).reshape(bsz, t, d)
    v = _dwconv(u, conv_w_dw[0], conv_b_dw[0])
    h = _conv_out(v, conv_ln_g[0], conv_ln_b[0], conv_w_pw2[0], h, mod3, lat_row(1), 2)
    return moe(h, 1, True)
```

```python
import functools

import jax
import jax.numpy as jnp
from jax import lax
from jax.experimental import pallas as pl
from jax.experimental.pallas import tpu as pltpu
from jax.experimental.pallas import tpu_sc as plsc

F32 = jnp.float32
BF16 = jnp.bfloat16
U32 = jnp.uint32
I32 = jnp.int32

EPS = 1e-6
LANES = 128
SUBLANES = 8
HEAD_DIM = 128
SCAN_CHUNK = 64
GRID_W = 64
CONV_WIDTH = 31
CONV_REACH = (16 - CONV_WIDTH // 2 + CONV_WIDTH - 1) // 8 * 8
MOE_GROUPS = 8
MOE_PER_GROUP = 8
MOE_EXPERTS = MOE_GROUPS * MOE_PER_GROUP
MOE_BLOCK = 256
SC_GATHER_WINDOW = 256
MOD_ROWS = 16
VMEM_LIMIT = 56 * 1024 * 1024
SAFE_CHUNK_LOG_DECAY = -80.0
HI_MASK = 0xFFFF0000


def _pick(n, pref):
    t = pref
    while n % t:
        t //= 2
    return t


def _params(sem):
    return pltpu.CompilerParams(dimension_semantics=sem, vmem_limit_bytes=VMEM_LIMIT)


def _silu(x):
    return x * jax.nn.sigmoid(x)


def _dot(a, b):
    return jnp.dot(a, b, preferred_element_type=F32)


def _dot_nt(a, b):
    return lax.dot_general(a, b, (((1,), (1,)), ((), ())), preferred_element_type=F32)


def _dot_tn(a, b):
    return lax.dot_general(a, b, (((0,), (0,)), ((), ())), preferred_element_type=F32)


def _pack_bf16_pairs(x):
    n = x.shape[1] // 2
    bits = lax.bitcast_convert_type(x.astype(BF16).astype(F32), U32)
    return (bits[:, :n] >> 16) | (bits[:, n:] & jnp.uint32(HI_MASK))


def _unpack_bf16_pairs(w):
    lo = lax.bitcast_convert_type(w << 16, F32)
    hi = lax.bitcast_convert_type(w & jnp.uint32(HI_MASK), F32)
    return lo, hi


def _store_token_groups(ref, w):
    m, n = w.shape[0], w.shape[1] // LANES
    for s in range(n):
        ref[:, s * SUBLANES:(s + 1) * SUBLANES, :] = (
            w[:, s * LANES:(s + 1) * LANES].reshape(m // SUBLANES, SUBLANES, LANES))


def _load_token_groups(ref, n):
    g = ref.shape[0]
    return jnp.concatenate(
        [ref[:, s * SUBLANES:(s + 1) * SUBLANES, :].reshape(g * SUBLANES, LANES) for s in range(n)],
        axis=1)


def _group_row(t, n):
    return (t // SUBLANES) * (SUBLANES * n) + t % SUBLANES


def _adaln_body(c_ref, w_ref, b_ref, o_ref):
    s = _silu(c_ref[...])
    o_ref[0] = _dot(s.astype(BF16), w_ref[0].astype(BF16)) + b_ref[0]


def _adaln(c_all, ada_w, ada_b):
    depth, d, n6 = ada_w.shape
    tn = _pick(n6, 1024)
    return pl.pallas_call(
        _adaln_body,
        grid=(depth, n6 // tn),
        in_specs=[
            pl.BlockSpec((MOD_ROWS, d), lambda l, j: (0, 0)),
            pl.BlockSpec((1, d, tn), lambda l, j: (l, 0, j)),
            pl.BlockSpec((1, 1, tn), lambda l, j: (l, 0, j)),
        ],
        out_specs=pl.BlockSpec((1, MOD_ROWS, tn), lambda l, j: (l, 0, j)),
        out_shape=jax.ShapeDtypeStruct((depth, MOD_ROWS, n6), F32),
        compiler_params=_params(("arbitrary", "arbitrary")),
        name="adaln",
    )(c_all, ada_w, ada_b.reshape(depth, 1, n6))


def _mod_spec(d, row_fn, sec):
    return pl.BlockSpec((1, 1, d), lambda b, i: (row_fn(b), 0, sec))


def _rms(x, g):
    return x * lax.rsqrt(jnp.mean(x * x, axis=-1, keepdims=True) + EPS) * g


def _norm_mod_body(x_ref, g_ref, sh_ref, sc_ref, o_ref):
    y = _rms(x_ref[0], g_ref[...])
    o_ref[0] = (y * (1.0 + sc_ref[0]) + sh_ref[0]).astype(o_ref.dtype)


def _norm_mod(x, g, mod3, row_fn, sec_shift, sec_scale):
    bsz, t, d = x.shape
    tm = _pick(t, 512)
    return pl.pallas_call(
        _norm_mod_body,
        grid=(bsz, t // tm),
        in_specs=[
            pl.BlockSpec((1, tm, d), lambda b, i: (b, i, 0)),
            pl.BlockSpec((1, d), lambda b, i: (0, 0)),
            _mod_spec(d, row_fn, sec_shift),
            _mod_spec(d, row_fn, sec_scale),
        ],
        out_specs=pl.BlockSpec((1, tm, d), lambda b, i: (b, i, 0)),
        out_shape=jax.ShapeDtypeStruct((bsz, t, d), BF16),
        compiler_params=_params(("arbitrary", "arbitrary")),
        name="norm_mod",
    )(x, g.reshape(1, d), mod3, mod3)


def _mm_body(a_ref, w_ref, o_ref, wb_ref):
    @pl.when(pl.program_id(1) == 0)
    def _():
        wb_ref[...] = w_ref[...].astype(BF16)

    o_ref[...] = _dot(a_ref[...], wb_ref[...]).astype(o_ref.dtype)


def _matmul(a, w):
    m, k = a.shape
    n = w.shape[1]
    tm, tn = _pick(m, 512), _pick(n, 1024)
    return pl.pallas_call(
        _mm_body,
        grid=(n // tn, m // tm),
        in_specs=[
            pl.BlockSpec((tm, k), lambda j, i: (i, 0)),
            pl.BlockSpec((k, tn), lambda j, i: (0, j)),
        ],
        out_specs=pl.BlockSpec((tm, tn), lambda j, i: (i, j)),
        out_shape=jax.ShapeDtypeStruct((m, n), BF16),
        scratch_shapes=[pltpu.VMEM((k, tn), BF16)],
        compiler_params=_params(("arbitrary", "arbitrary")),
        name="matmul",
    )(a, w)


def _mm_glu_body(a_ref, wv_ref, wg_ref, o_ref, wvb_ref, wgb_ref):
    @pl.when(pl.program_id(1) == 0)
    def _():
        wvb_ref[...] = wv_ref[...].astype(BF16)
        wgb_ref[...] = wg_ref[...].astype(BF16)

    a = a_ref[...]
    val = _dot(a, wvb_ref[...])
    gate = _dot(a, wgb_ref[...])
    o_ref[...] = (val * jax.nn.sigmoid(gate)).astype(o_ref.dtype)


def _matmul_glu(a, w):
    m, k = a.shape
    n = w.shape[1] // 2
    tm, tn = _pick(m, 512), _pick(n, 512)
    nj = n // tn
    return pl.pallas_call(
        _mm_glu_body,
        grid=(nj, m // tm),
        in_specs=[
            pl.BlockSpec((tm, k), lambda j, i: (i, 0)),
            pl.BlockSpec((k, tn), lambda j, i: (0, j)),
            pl.BlockSpec((k, tn), lambda j, i: (0, j + nj)),
        ],
        out_specs=pl.BlockSpec((tm, tn), lambda j, i: (i, j)),
        out_shape=jax.ShapeDtypeStruct((m, n), BF16),
        scratch_shapes=[pltpu.VMEM((k, tn), BF16), pltpu.VMEM((k, tn), BF16)],
        compiler_params=_params(("arbitrary", "arbitrary")),
        name="matmul_glu",
    )(a, w, w)


def _scan_body(qc_ref, vc_ref, zc_ref, ql_ref, vl_ref, zl_ref, lb_ref, o_ref,
               st_ref, flag_ref, b_sc, q_sc, k_sc, v_sc, *, nc, n_sub, heads):
    chunk = SCAN_CHUNK
    direction = pl.program_id(1)
    step = pl.program_id(2)
    fwd = direction == 0

    @pl.when(step == 0)
    def _():
        st_ref[0] = jnp.zeros(st_ref.shape[1:], F32)

    @pl.when(step > 0)
    def _():
        st_ref[0] = st_ref[n_sub]

    lb = lb_ref[0]
    row = lax.broadcasted_iota(I32, (chunk, chunk), 0)
    col = lax.broadcasted_iota(I32, (chunk, chunk), 1)
    sign = 1 - 2 * direction
    tri = (row - col) * sign >= 0
    tri_b = jnp.where(tri, 1.0, 0.0).astype(BF16)
    t_idx = lax.broadcasted_iota(I32, (chunk, 1), 0)

    def chunk_rows(ci):
        cidx = jnp.where(fwd, ci, n_sub - 1 - ci)
        return pl.ds(pl.multiple_of(cidx * chunk, chunk), chunk)

    def decay(z_ref, rows, sl):
        z = z_ref[0, rows, sl].astype(F32)
        f = lb[:, sl] + (1.0 - lb[:, sl]) * jax.nn.sigmoid(z)
        lf = jnp.log(f)
        hi = lf.astype(BF16)
        lo = (lf - hi.astype(F32)).astype(BF16)
        b = _dot(tri_b, hi) + _dot(tri_b, lo)
        tot = jnp.where(fwd, b[chunk - 1:chunk, :], b[0:1, :])
        return b, tot, 1.0 - f

    def fast_chunk(q_ref, v_ref, z_ref, ci, emit):
        rows = chunk_rows(ci)
        b, tot, kk = decay(z_ref, rows, slice(None))
        ke = kk * jnp.exp(tot - b)
        ke_b = ke.astype(BF16)
        etot = jnp.exp(tot)
        v = v_ref[0, rows, :]
        head_sl = [slice(h * HEAD_DIM, (h + 1) * HEAD_DIM) for h in range(heads)]
        st = [st_ref[ci, h] for h in range(heads)]
        kv = [_dot_tn(v[:, sl], ke_b[:, sl]) for sl in head_sl]
        if emit:
            qe = _silu(q_ref[0, rows, :].astype(F32)) * jnp.exp(b)
            qe_b = qe.astype(BF16)
            g = jnp.exp(-0.5 * tot)
            qh = (qe * g).astype(BF16)
            kh = (ke * g).astype(BF16)
            flag_ref[ci] = (jnp.min(tot) < SAFE_CHUNK_LOG_DECAY).astype(I32)
            sc = [_dot_nt(qh[:, sl], kh[:, sl]) for sl in head_sl]
            o_inter = [_dot_nt(qe_b[:, sl], st[h].astype(BF16)) for h, sl in enumerate(head_sl)]
            p = [jnp.where(tri, s, 0.0).astype(BF16) for s in sc]
            o_intra = [_dot(p[h], v[:, sl]) for h, sl in enumerate(head_sl)]
            for h, sl in enumerate(head_sl):
                o_ref[0, 0, rows, sl] = (o_inter[h] + o_intra[h]).astype(o_ref.dtype)
        for h, sl in enumerate(head_sl):
            st_ref[ci + 1, h] = st[h] * etot[:, sl] + kv[h]

    def exact_chunk(q_ref, v_ref, z_ref, ci):
        rows = chunk_rows(ci)
        b, _, kk = decay(z_ref, rows, slice(None))
        b_sc[...] = b
        q_sc[...] = _silu(q_ref[0, rows, :].astype(F32))
        k_sc[...] = kk
        v_sc[...] = v_ref[0, rows, :].astype(F32)

        def head_loop(h, c2):
            lanes = pl.ds(pl.multiple_of(h * HEAD_DIM, HEAD_DIM), HEAD_DIM)
            b_h = b_sc[:, lanes]
            q_h = q_sc[:, lanes]
            k_h = k_sc[:, lanes]
            v_h = v_sc[:, lanes]
            o_inter = _dot_nt((q_h * jnp.exp(b_h)).astype(BF16), st_ref[ci, h].astype(BF16))

            def key_loop(s, acc):
                pick = t_idx == s
                b_s = jnp.sum(jnp.where(pick, b_h, 0.0), axis=0, keepdims=True)
                k_s = jnp.sum(jnp.where(pick, k_h, 0.0), axis=0, keepdims=True)
                v_s = jnp.sum(jnp.where(pick, v_h, 0.0), axis=0, keepdims=True)
                valid = (t_idx - s) * sign >= 0
                dec = jnp.where(valid, jnp.exp(jnp.where(valid, b_h - b_s, 0.0)), 0.0)
                w = jnp.sum(q_h * dec * k_s, axis=1, keepdims=True)
                return acc + w * v_s

            o_intra = lax.fori_loop(0, chunk, key_loop, jnp.zeros((chunk, HEAD_DIM), F32))
            o_ref[0, 0, rows, lanes] = (o_inter + o_intra).astype(o_ref.dtype)
            return c2

        lax.fori_loop(0, heads, head_loop, 0)

    @pl.when(step < nc)
    def _():
        for ci in range(n_sub):
            fast_chunk(qc_ref, vc_ref, zc_ref, ci, False)

    @pl.when(step >= nc)
    def _():
        for ci in range(n_sub):
            fast_chunk(ql_ref, vl_ref, zl_ref, ci, True)

        def redo(ci, c):
            @pl.when(flag_ref[ci] == 1)
            def _():
                exact_chunk(ql_ref, vl_ref, zl_ref, ci)
            return c

        lax.fori_loop(0, n_sub, redo, 0)


def _scan(p_ctx, p_lat, lb):
    bsz, t_lat, d5 = p_lat.shape
    t_ctx = p_ctx.shape[1]
    d = d5 // 5
    heads = d // HEAD_DIM
    tb = _pick(t_ctx, 256)
    assert t_lat % tb == 0 and tb % SCAN_CHUNK == 0
    nc, nl = t_ctx // tb, t_lat // tb
    n_sub = tb // SCAN_CHUNK

    def ctx_blk(dr, s):
        sc = jnp.minimum(s, nc - 1)
        return jnp.where(dr == 0, sc, nc - 1 - sc)

    def lat_blk(dr, s):
        sl = jnp.maximum(s - nc, 0)
        return jnp.where(dr == 0, sl, nl - 1 - sl)

    def cspec(sec_fn):
        return pl.BlockSpec((1, tb, d), lambda b, dr, s: (b, ctx_blk(dr, s), sec_fn(dr)))

    def lspec(sec_fn):
        return pl.BlockSpec((1, tb, d), lambda b, dr, s: (b, lat_blk(dr, s), sec_fn(dr)))

    q_sec = lambda dr: 0
    v_sec = lambda dr: 1
    z_sec = lambda dr: 2 + dr
    return pl.pallas_call(
        functools.partial(_scan_body, nc=nc, n_sub=n_sub, heads=heads),
        grid=(bsz, 2, nc + nl),
        in_specs=[cspec(q_sec), cspec(v_sec), cspec(z_sec),
                  lspec(q_sec), lspec(v_sec), lspec(z_sec),
                  pl.BlockSpec((1, 1, d), lambda b, dr, s: (dr, 0, 0))],
        out_specs=pl.BlockSpec((1, 1, tb, d), lambda b, dr, s: (dr, b, lat_blk(dr, s), 0)),
        out_shape=jax.ShapeDtypeStruct((2, bsz, t_lat, d), BF16),
        scratch_shapes=[pltpu.VMEM((n_sub + 1, heads, HEAD_DIM, HEAD_DIM), F32),
                        pltpu.SMEM((n_sub,), I32),
                        pltpu.VMEM((SCAN_CHUNK, d), F32),
                        pltpu.VMEM((SCAN_CHUNK, d), F32),
                        pltpu.VMEM((SCAN_CHUNK, d), F32),
                        pltpu.VMEM((SCAN_CHUNK, d), F32)],
        compiler_params=_params(("arbitrary", "arbitrary", "arbitrary")),
        name="hgrn_scan",
    )(p_ctx, p_ctx, p_ctx, p_lat, p_lat, p_lat, lb.reshape(2, 1, d))


def _hgrn_readout_body(of_ref, ob_ref, g_ref, on_ref, w_ref, h_ref, gt_ref, o_ref,
                       wb_ref, a_ref, *, heads):
    @pl.when((pl.program_id(0) == 0) & (pl.program_id(1) == 0))
    def _():
        wb_ref[...] = w_ref[...].astype(BF16)

    on = on_ref[...]
    for h in range(heads):
        sl = slice(h * HEAD_DIM, (h + 1) * HEAD_DIM)
        o = of_ref[0, 0, :, sl].astype(F32) + ob_ref[0, 0, :, sl].astype(F32)
        o = _rms(o, on)
        a_ref[:, sl] = (o * _silu(g_ref[0, :, sl].astype(F32))).astype(BF16)
    y = _dot(a_ref[...], wb_ref[...])
    o_ref[0] = h_ref[0] + gt_ref[0] * y


def _hgrn_readout(o2, p_lat, onorm_g, w_out, h, mod3, row_fn, sec_gate):
    bsz, t, d = h.shape
    heads = d // HEAD_DIM
    tm = _pick(t, 256)
    return pl.pallas_call(
        functools.partial(_hgrn_readout_body, heads=heads),
        grid=(bsz, t // tm),
        in_specs=[
            pl.BlockSpec((1, 1, tm, d), lambda b, i: (0, b, i, 0)),
            pl.BlockSpec((1, 1, tm, d), lambda b, i: (1, b, i, 0)),
            pl.BlockSpec((1, tm, d), lambda b, i: (b, i, 4)),
            pl.BlockSpec((1, HEAD_DIM), lambda b, i: (0, 0)),
            pl.BlockSpec((d, d), lambda b, i: (0, 0), pipeline_mode=pl.Buffered(1)),
            pl.BlockSpec((1, tm, d), lambda b, i: (b, i, 0)),
            _mod_spec(d, row_fn, sec_gate),
        ],
        out_specs=pl.BlockSpec((1, tm, d), lambda b, i: (b, i, 0)),
        out_shape=jax.ShapeDtypeStruct((bsz, t, d), F32),
        scratch_shapes=[pltpu.VMEM((d, d), BF16), pltpu.VMEM((tm, d), BF16)],
        compiler_params=_params(("arbitrary", "arbitrary")),
        name="hgrn_readout",
    )(o2, o2, p_lat, onorm_g.reshape(1, HEAD_DIM), w_out, h, mod3)


def _conv_out_body(v_ref, lg_ref, lbias_ref, w_ref, h_ref, gt_ref, o_ref, wb_ref):
    @pl.when((pl.program_id(0) == 0) & (pl.program_id(1) == 0))
    def _():
        wb_ref[...] = w_ref[...].astype(BF16)

    x = v_ref[0].astype(F32)
    mu = jnp.mean(x, axis=-1, keepdims=True)
    xc = x - mu
    y = xc * lax.rsqrt(jnp.mean(xc * xc, axis=-1, keepdims=True) + EPS)
    y = _silu(y * lg_ref[...] + lbias_ref[...])
    o_ref[0] = h_ref[0] + gt_ref[0] * _dot(y.astype(BF16), wb_ref[...])


def _conv_out(v, ln_g, ln_b, w, h, mod3, row_fn, sec_gate):
    bsz, t, d = h.shape
    tm = _pick(t, 256)
    return pl.pallas_call(
        _conv_out_body,
        grid=(bsz, t // tm),
        in_specs=[
            pl.BlockSpec((1, tm, d), lambda b, i: (b, i, 0)),
            pl.BlockSpec((1, d), lambda b, i: (0, 0)),
            pl.BlockSpec((1, d), lambda b, i: (0, 0)),
            pl.BlockSpec((d, d), lambda b, i: (0, 0), pipeline_mode=pl.Buffered(1)),
            pl.BlockSpec((1, tm, d), lambda b, i: (b, i, 0)),
            _mod_spec(d, row_fn, sec_gate),
        ],
        out_specs=pl.BlockSpec((1, tm, d), lambda b, i: (b, i, 0)),
        out_shape=jax.ShapeDtypeStruct((bsz, t, d), F32),
        scratch_shapes=[pltpu.VMEM((d, d), BF16)],
        compiler_params=_params(("arbitrary", "arbitrary")),
        name="conv_out",
    )(v, ln_g.reshape(1, d), ln_b.reshape(1, d), w, h, mod3)


def _dwconv_body(u_ref, w_ref, b_ref, o_ref, pad_ref, shift_ref, *, rows, cb, half_blocks):
    half = CONV_WIDTH // 2
    j = pl.program_id(1)
    n_lane = cb // LANES
    bias = b_ref[...]

    def taps(read, r):
        for lb in range(n_lane):
            ln = slice(lb * LANES, (lb + 1) * LANES)
            acc = jnp.zeros((GRID_W, LANES), F32)
            for k in range(CONV_WIDTH):
                acc = acc + w_ref[k:k + 1, ln] * read(r, k, ln)
            o_ref[0, pl.ds(pl.multiple_of(r * GRID_W, GRID_W), GRID_W), ln] = (
                acc + bias[:, ln]).astype(o_ref.dtype)

    @pl.when(j < half_blocks)
    def _():
        zeros = jnp.zeros((rows, 16, cb), F32)
        pad_ref[0:rows, 0:16, :] = zeros
        pad_ref[0:rows, 16 + GRID_W:32 + GRID_W, :] = zeros
        pad_ref[0:rows, 16:16 + GRID_W, :] = u_ref[0].astype(F32).reshape(rows, GRID_W, cb)

        span = GRID_W + CONV_REACH

        def row_loop(r, c):
            for s in range(1, SUBLANES):
                shift_ref[s - 1] = pad_ref[r, s:s + span, :]

            def read(r_, k, ln):
                a, s = divmod(16 - half + k, SUBLANES)
                if s == 0:
                    return pad_ref[r_, a * SUBLANES:a * SUBLANES + GRID_W, ln]
                return shift_ref[s - 1, a * SUBLANES:a * SUBLANES + GRID_W, ln]

            taps(read, r)
            return c

        lax.fori_loop(0, rows, row_loop, 0)

    @pl.when(j >= half_blocks)
    def _():
        zeros = jnp.zeros((half, GRID_W, cb), F32)
        pad_ref[0:half, 0:GRID_W, :] = zeros
        pad_ref[half + rows:2 * half + rows, 0:GRID_W, :] = zeros
        pad_ref[half:half + rows, 0:GRID_W, :] = u_ref[0].astype(F32).reshape(rows, GRID_W, cb)

        def row_loop(r, c):
            taps(lambda r_, k, ln: pad_ref[r_ + k, 0:GRID_W, ln], r)
            return c

        lax.fori_loop(0, rows, row_loop, 0)


def _dwconv(u, w_dw, b_dw):
    bsz, t, d = u.shape
    rows = t // GRID_W
    cb = _pick(d // 2, 256)
    half_blocks = (d // 2) // cb
    return pl.pallas_call(
        functools.partial(_dwconv_body, rows=rows, cb=cb, half_blocks=half_blocks),
        grid=(bsz, d // cb),
        in_specs=[
            pl.BlockSpec((1, t, cb), lambda b, j: (b, 0, j)),
            pl.BlockSpec((CONV_WIDTH, cb), lambda b, j: (0, j)),
            pl.BlockSpec((1, cb), lambda b, j: (0, j)),
        ],
        out_specs=pl.BlockSpec((1, t, cb), lambda b, j: (b, 0, j)),
        out_shape=jax.ShapeDtypeStruct((bsz, t, d), BF16),
        scratch_shapes=[pltpu.VMEM((rows + 2 * (CONV_WIDTH // 2) + 2, GRID_W + 32, cb), F32),
                        pltpu.VMEM((SUBLANES - 1, GRID_W + CONV_REACH, cb), F32)],
        compiler_params=_params(("arbitrary", "arbitrary")),
        name="dwconv",
    )(u, w_dw, b_dw.reshape(1, d))


def _router_body(h_ref, g_ref, sh_ref, sc_ref, wr_ref, br_ref,
                 bp_ref, ri_ref, rw_ref, cnt_ref, carry_ref):
    tm = h_ref.shape[1]

    @pl.when((pl.program_id(0) == 0) & (pl.program_id(1) == 0))
    def _():
        carry_ref[...] = jnp.zeros_like(carry_ref)

    bmod = _rms(h_ref[0], g_ref[...]) * (1.0 + sc_ref[0]) + sh_ref[0]
    _store_token_groups(bp_ref, _pack_bf16_pairs(bmod))

    b_hi = bmod.astype(BF16)
    b_lo = (bmod - b_hi.astype(F32)).astype(BF16)
    w = wr_ref[...]
    w_hi = w.astype(BF16)
    w_lo = (w - w_hi.astype(F32)).astype(BF16)
    logits = _dot(b_hi, w_hi) + _dot(b_hi, w_lo) + _dot(b_lo, w_hi) + br_ref[...]

    lane_i = lax.broadcasted_iota(I32, (tm, LANES), 1)
    lane = lane_i.astype(F32)
    neg = jnp.float32(-jnp.inf)
    big = jnp.float32(1 << 20)
    is_grp = (lane_i >= MOE_EXPERTS) & (lane_i < MOE_EXPERTS + MOE_GROUPS)
    m1 = jnp.max(jnp.where(is_grp, logits, neg), axis=1, keepdims=True)
    grp = jnp.min(jnp.where(is_grp & (logits == m1), lane - MOE_EXPERTS, big), axis=1, keepdims=True)
    pg = 1.0 / jnp.sum(jnp.where(is_grp, jnp.exp(logits - m1), 0.0), axis=1, keepdims=True)

    sel = (lane >= grp * MOE_PER_GROUP) & (lane < (grp + 1) * MOE_PER_GROUP)
    m2 = jnp.max(jnp.where(sel, logits, neg), axis=1, keepdims=True)
    ea = jnp.min(jnp.where(sel & (logits == m2), lane, big), axis=1, keepdims=True)
    sel2 = sel & (lane != ea)
    m3 = jnp.max(jnp.where(sel2, logits, neg), axis=1, keepdims=True)
    eb = jnp.min(jnp.where(sel2 & (logits == m3), lane, big), axis=1, keepdims=True)
    tr = jnp.exp(m3 - m2)
    wa = pg / (1.0 + tr)
    wb = wa * tr

    hit_a = lane == ea
    hit_b = lane == eb
    onehot = jnp.where(hit_a | hit_b, 1.0, 0.0)
    r_i = lax.broadcasted_iota(I32, (tm, tm), 0)
    c_i = lax.broadcasted_iota(I32, (tm, tm), 1)
    lower = jnp.where(r_i > c_i, 1.0, 0.0).astype(BF16)
    before = _dot(lower, onehot.astype(BF16)) + carry_ref[...]
    rank_a = jnp.sum(jnp.where(hit_a, before, 0.0), axis=1, keepdims=True).astype(I32)
    rank_b = jnp.sum(jnp.where(hit_b, before, 0.0), axis=1, keepdims=True).astype(I32)
    carry = carry_ref[...] + jnp.sum(onehot, axis=0, keepdims=True)
    carry_ref[...] = carry
    cnt_ref[...] = carry

    ri_ref[...] = jnp.where(lane_i == 0, ea.astype(I32), jnp.where(lane_i == 1, eb.astype(I32),
                            jnp.where(lane_i == 2, rank_a, jnp.where(lane_i == 3, rank_b, 0))))
    rw_ref[...] = jnp.where(lane_i == 0, wa, jnp.where(lane_i == 1, wb, 0.0))


def _router(h, g, mod3, row_fn, sec_shift, sec_scale, w_r, b_r):
    bsz, t, d = h.shape
    tm = _pick(t, 256)
    nt = t // tm
    n = bsz * t
    rpt = d // 2 // LANES
    return pl.pallas_call(
        _router_body,
        grid=(bsz, nt),
        in_specs=[
            pl.BlockSpec((1, tm, d), lambda b, i: (b, i, 0)),
            pl.BlockSpec((1, d), lambda b, i: (0, 0)),
            _mod_spec(d, row_fn, sec_shift),
            _mod_spec(d, row_fn, sec_scale),
            pl.BlockSpec((d, LANES), lambda b, i: (0, 0)),
            pl.BlockSpec((1, LANES), lambda b, i: (0, 0)),
        ],
        out_specs=[
            pl.BlockSpec((tm // SUBLANES, SUBLANES * rpt, LANES), lambda b, i: (b * nt + i, 0, 0)),
            pl.BlockSpec((tm, LANES), lambda b, i: (b * nt + i, 0)),
            pl.BlockSpec((tm, LANES), lambda b, i: (b * nt + i, 0)),
            pl.BlockSpec((1, LANES), lambda b, i: (0, 0)),
        ],
        out_shape=[
            jax.ShapeDtypeStruct((n // SUBLANES, SUBLANES * rpt, LANES), U32),
            jax.ShapeDtypeStruct((n, LANES), I32),
            jax.ShapeDtypeStruct((n, LANES), F32),
            jax.ShapeDtypeStruct((1, LANES), F32),
        ],
        scratch_shapes=[pltpu.VMEM((1, LANES), F32)],
        compiler_params=_params(("arbitrary", "arbitrary")),
        name="router",
    )(h, g.reshape(1, d), mod3, mod3, w_r, b_r)


def _expert_body(meta_ref, xs_ref, wg_ref, wu_ref, wd_ref, y_ref,
                 wgb_ref, wub_ref, wdb_ref, xbuf, ybuf, xsem, ysem, *, rpt):
    e = pl.program_id(0)
    blk0 = meta_ref[0, e]
    n_blk = meta_ref[0, e + 1] - blk0
    n_used = meta_ref[2, 0]
    dh = rpt * LANES
    groups = MOE_BLOCK // SUBLANES

    def block_groups(ref, g):
        return ref.at[pl.ds(pl.multiple_of(g * groups, groups), groups)]

    def read_in(g, slot):
        return pltpu.make_async_copy(block_groups(xs_ref, g), xbuf.at[slot], xsem.at[slot])

    def write_out(g):
        return pltpu.make_async_copy(ybuf, block_groups(y_ref, g), ysem)

    @pl.when(e == 0)
    def _():
        read_in(0, 0).start()

    @pl.when(n_blk > 0)
    def _():
        wgb_ref[...] = wg_ref[0, 0].astype(BF16)
        wub_ref[...] = wu_ref[0, 0].astype(BF16)
        wdb_ref[...] = wd_ref[0, 0].astype(BF16)

        def block(j, c):
            g = blk0 + j
            slot = g % 2

            @pl.when(g + 1 < n_used)
            def _():
                read_in(g + 1, 1 - slot).start()

            read_in(g, slot).wait()
            x_lo, x_hi = _unpack_bf16_pairs(_load_token_groups(xbuf.at[slot], rpt))
            x_lo = x_lo.astype(BF16)
            x_hi = x_hi.astype(BF16)
            gate = _dot(x_lo, wgb_ref[0:dh, :]) + _dot(x_hi, wgb_ref[dh:2 * dh, :])
            up = _dot(x_lo, wub_ref[0:dh, :]) + _dot(x_hi, wub_ref[dh:2 * dh, :])
            mid = (_silu(gate) * up).astype(BF16)
            out = _pack_bf16_pairs(_dot(mid, wdb_ref[...]))

            @pl.when(g > 0)
            def _():
                write_out(g - 1).wait()

            _store_token_groups(ybuf, out)
            write_out(g).start()
            return c

        lax.fori_loop(0, n_blk, block, 0)

    @pl.when(e == pl.num_programs(0) - 1)
    def _():
        write_out(n_used - 1).wait()


def _experts(meta, xs, w_gate, w_up, w_down, layer):
    _, n_exp, d, hid = w_gate.shape
    rpt = d // 2 // LANES
    blk_shape = (MOE_BLOCK // SUBLANES, SUBLANES * rpt, LANES)

    def wspec(shape):
        return pl.BlockSpec((1, 1) + shape, lambda e, m: (layer, e, 0, 0))

    return pl.pallas_call(
        functools.partial(_expert_body, rpt=rpt),
        grid_spec=pltpu.PrefetchScalarGridSpec(
            num_scalar_prefetch=1,
            grid=(n_exp,),
            in_specs=[pl.BlockSpec(memory_space=pl.ANY),
                      wspec((d, hid)), wspec((d, hid)), wspec((hid, d))],
            out_specs=pl.BlockSpec(memory_space=pl.ANY),
            scratch_shapes=[pltpu.VMEM((d, hid), BF16), pltpu.VMEM((d, hid), BF16),
                            pltpu.VMEM((hid, d), BF16),
                            pltpu.VMEM((2,) + blk_shape, U32), pltpu.VMEM(blk_shape, U32),
                            pltpu.SemaphoreType.DMA((2,)), pltpu.SemaphoreType.DMA],
        ),
        out_shape=jax.ShapeDtypeStruct(xs.shape, U32),
        compiler_params=_params(("arbitrary",)),
        name="moe_experts",
    )(meta, xs, w_gate, w_up, w_down)


def _sc_mesh():
    return plsc.VectorSubcoreMesh(core_axis_name="core", subcore_axis_name="subcore")


def _sc_scatter_rows_twice(src, idx_a, idx_b, n_out):
    n = src.shape[0]
    win = _pick(n, SC_GATHER_WINDOW)

    @pl.kernel(out_type=jax.ShapeDtypeStruct((n_out, LANES), src.dtype), mesh=_sc_mesh(),
               scratch_types=[])
    def scatter_kernel(s_hbm, ia_hbm, ib_hbm, o_hbm):
        def window(s_vmem, ia_vmem, ib_vmem):
            pltpu.sync_copy(s_vmem, o_hbm.at[ia_vmem.at[0]])
            pltpu.sync_copy(s_vmem, o_hbm.at[ib_vmem.at[0]])

        pltpu.emit_pipeline(
            window,
            grid=(n // win,),
            in_specs=[pl.BlockSpec((win, LANES), index_map=lambda i: (i, 0)),
                      pl.BlockSpec((1, win), index_map=lambda i: (0, i)),
                      pl.BlockSpec((1, win), index_map=lambda i: (0, i))],
            out_specs=[],
            core_axis_name=("core", "subcore"),
            dimension_semantics=(pltpu.PARALLEL,),
        )(s_hbm, ia_hbm, ib_hbm)

    return scatter_kernel(src, idx_a.reshape(1, n), idx_b.reshape(1, n))


def _sc_gather_rows_twice(table, idx_a, idx_b):
    n_idx = idx_a.shape[0]
    win = _pick(n_idx, SC_GATHER_WINDOW // 2)
    out = jax.ShapeDtypeStruct((n_idx, LANES), table.dtype)

    @pl.kernel(out_type=(out, out), mesh=_sc_mesh())
    def gather_kernel(t_hbm, ia_hbm, ib_hbm, oa_hbm, ob_hbm):
        def window(ia_vmem, ib_vmem, oa_vmem, ob_vmem):
            pltpu.sync_copy(t_hbm.at[ia_vmem.at[0]], oa_vmem)
            pltpu.sync_copy(t_hbm.at[ib_vmem.at[0]], ob_vmem)

        idx_spec = pl.BlockSpec((1, win), index_map=lambda i: (0, i))
        out_spec = pl.BlockSpec((win, LANES), index_map=lambda i: (i, 0))
        pltpu.emit_pipeline(
            window,
            grid=(n_idx // win,),
            in_specs=[idx_spec, idx_spec],
            out_specs=[out_spec, out_spec],
            core_axis_name=("core", "subcore"),
            dimension_semantics=(pltpu.PARALLEL,),
        )(ia_hbm, ib_hbm, oa_hbm, ob_hbm)

    return gather_kernel(table, idx_a.reshape(1, n_idx), idx_b.reshape(1, n_idx))


def _combine_body(h_ref, ya_ref, yb_ref, rw_ref, gt_ref, fg_ref, o_ref, *, rpt, final):
    tm = h_ref.shape[1]
    dh = rpt * LANES
    wa = rw_ref[:, 0:1]
    wb = rw_ref[:, 1:2]
    a_lo, a_hi = _unpack_bf16_pairs(_load_token_groups(ya_ref, rpt))
    b_lo, b_hi = _unpack_bf16_pairs(_load_token_groups(yb_ref, rpt))
    gt = gt_ref[0]
    lo = h_ref[0, :, 0:dh] + gt[:, 0:dh] * (wa * a_lo + wb * b_lo)
    hi = h_ref[0, :, dh:2 * dh] + gt[:, dh:2 * dh] * (wa * a_hi + wb * b_hi)
    if final:
        ms = (jnp.sum(lo * lo, axis=-1, keepdims=True)
              + jnp.sum(hi * hi, axis=-1, keepdims=True)) / (2 * dh)
        inv = lax.rsqrt(ms + EPS)
        lo = lo * inv * fg_ref[:, 0:dh]
        hi = hi * inv * fg_ref[:, dh:2 * dh]
    o_ref[0, :, 0:dh] = lo
    o_ref[0, :, dh:2 * dh] = hi


def _combine(h, ya, yb, rw, mod3, row_fn, sec_gate, final_g, final):
    bsz, t, d = h.shape
    tm = _pick(t, 256)
    nt = t // tm
    rpt = d // 2 // LANES
    return pl.pallas_call(
        functools.partial(_combine_body, rpt=rpt, final=final),
        grid=(bsz, nt),
        in_specs=[
            pl.BlockSpec((1, tm, d), lambda b, i: (b, i, 0)),
            pl.BlockSpec((tm // SUBLANES, SUBLANES * rpt, LANES), lambda b, i: (b * nt + i, 0, 0)),
            pl.BlockSpec((tm // SUBLANES, SUBLANES * rpt, LANES), lambda b, i: (b * nt + i, 0, 0)),
            pl.BlockSpec((tm, LANES), lambda b, i: (b * nt + i, 0)),
            _mod_spec(d, row_fn, sec_gate),
            pl.BlockSpec((1, d), lambda b, i: (0, 0)),
        ],
        out_specs=pl.BlockSpec((1, tm, d), lambda b, i: (b, i, 0)),
        out_shape=jax.ShapeDtypeStruct((bsz, t, d), F32),
        compiler_params=_params(("arbitrary", "arbitrary")),
        name="moe_combine",
    )(h, ya, yb, rw, mod3, final_g.reshape(1, d))


def _slot_rows_body(ri_ref, start_ref, o_ref, *, rpt):
    tm = ri_ref.shape[0]
    ri = ri_ref[...].astype(F32)
    start = start_ref[...].astype(F32)
    lane_i = lax.broadcasted_iota(I32, (tm, LANES), 1)
    lane = lane_i.astype(F32)

    def first_row(e, rank):
        s = jnp.sum(jnp.where(lane == e, start, 0.0), axis=1, keepdims=True)
        return _group_row((s + rank).astype(I32), rpt)

    row_a = first_row(ri[:, 0:1], ri[:, 2:3])
    row_b = first_row(ri[:, 1:2], ri[:, 3:4])
    o_ref[...] = jnp.where(lane_i == 0, row_a, jnp.where(lane_i == 1, row_b, 0))


def _slot_rows(ri, start, rpt):
    n = ri.shape[0]
    tm = _pick(n, 512)
    return pl.pallas_call(
        functools.partial(_slot_rows_body, rpt=rpt),
        grid=(n // tm,),
        in_specs=[pl.BlockSpec((tm, LANES), lambda i: (i, 0)),
                  pl.BlockSpec((1, LANES), lambda i: (0, 0))],
        out_specs=pl.BlockSpec((tm, LANES), lambda i: (i, 0)),
        out_shape=jax.ShapeDtypeStruct((n, LANES), I32),
        compiler_params=_params(("arbitrary",)),
        name="slot_rows",
    )(ri, start)


def _hier_moe(h, g, mod3, row_fn, w_r1, b_r1, w_r2, b_r2, w_gate, w_up, w_down, layer, final_g, final):
    bsz, t, d = h.shape
    n = bsz * t
    w_r = jnp.concatenate([jnp.transpose(w_r2, (1, 0, 2)).reshape(d, MOE_EXPERTS), w_r1], axis=1)
    w_r = jnp.pad(w_r, ((0, 0), (0, LANES - w_r.shape[1])))
    b_r = jnp.pad(jnp.concatenate([b_r2.reshape(-1), b_r1]), (0, LANES - MOE_EXPERTS - MOE_GROUPS))
    bp, ri, rw, cnt = _router(h, g, mod3, row_fn, 3, 4, w_r, b_r.reshape(1, LANES))

    counts = cnt[0, :MOE_EXPERTS].astype(I32)
    padded = (counts + MOE_BLOCK - 1) // MOE_BLOCK * MOE_BLOCK
    pad_end = jnp.cumsum(padded)
    pad_start = pad_end - padded
    n_slots = (-(-(n * 2) // MOE_BLOCK) + MOE_EXPERTS) * MOE_BLOCK
    blk_off = jnp.concatenate([pad_start, pad_end[-1:]]) // MOE_BLOCK
    row = lambda v: jnp.pad(v.astype(I32), (0, LANES - v.shape[0]))
    meta = jnp.stack([row(blk_off), row(counts), row(blk_off[-1:])])
    rpt = d // 2 // LANES
    first = _slot_rows(ri, row(pad_start).reshape(1, LANES), rpt)
    piece = (SUBLANES * jnp.arange(rpt, dtype=I32)).reshape(1, rpt, 1)
    rows_a = (first[:, 0].reshape(n // SUBLANES, 1, SUBLANES) + piece).reshape(-1)
    rows_b = (first[:, 1].reshape(n // SUBLANES, 1, SUBLANES) + piece).reshape(-1)

    grouped = lambda a: a.reshape(-1, SUBLANES * rpt, LANES)
    xs = _sc_scatter_rows_twice(bp.reshape(-1, LANES), rows_a, rows_b, n_slots * rpt)
    ys = _experts(meta, grouped(xs), w_gate, w_up, w_down, layer)
    ya, yb = _sc_gather_rows_twice(ys.reshape(-1, LANES), rows_a, rows_b)
    return _combine(h, grouped(ya), grouped(yb), rw, mod3, row_fn, 5, final_g, final)


def kernel(x, c, ctx, c_ctx, ada_w, ada_b, norm1_g, norm2_g, hgrn_w_in, hgrn_lb, hgrn_onorm_g,
           hgrn_w_out, conv_w_pw1, conv_w_dw, conv_b_dw, conv_ln_g, conv_ln_b, conv_w_pw2,
           moe_w_r1, moe_b_r1, moe_w_r2, moe_b_r2, moe_w_gate, moe_w_up, moe_w_down, final_g):
    bsz, t, d = x.shape
    t_ctx = ctx.shape[1]
    depth = ada_w.shape[0]
    assert depth == 2 and bsz < MOD_ROWS and d % (2 * LANES) == 0 and t % GRID_W == 0

    c_all = jnp.concatenate([c, c_ctx[None, :], jnp.zeros((MOD_ROWS - bsz - 1, d), F32)], axis=0)
    mod = _adaln(c_all, ada_w, ada_b)
    mod3 = mod.reshape(depth * MOD_ROWS, 1, 6 * d)
    lat_row = lambda l: (lambda b: l * MOD_ROWS + b)
    ctx_row = lambda l: (lambda b: l * MOD_ROWS + bsz)
    lb_all = jnp.cumsum(jax.nn.softmax(hgrn_lb.astype(F32), axis=1), axis=1)

    def moe(h, l, final):
        return _hier_moe(h, norm2_g[l], mod3, lat_row(l), moe_w_r1[l], moe_b_r1[l], moe_w_r2[l],
                         moe_b_r2[l], moe_w_gate, moe_w_up, moe_w_down, l, final_g, final)

    a_lat = _norm_mod(x, norm1_g[0], mod3, lat_row(0), 0, 1)
    a_ctx = _norm_mod(ctx, norm1_g[0], mod3, ctx_row(0), 0, 1)
    p_lat = _matmul(a_lat.reshape(bsz * t, d), hgrn_w_in[0]).reshape(bsz, t, 5 * d)
    p_ctx = _matmul(a_ctx.reshape(bsz * t_ctx, d), hgrn_w_in[0]).reshape(bsz, t_ctx, 5 * d)
    o2 = _scan(p_ctx, p_lat, lb_all[:, 0])
    h = _hgrn_readout(o2, p_lat, hgrn_onorm_g[0], hgrn_w_out[0], x, mod3, lat_row(0), 2)
    h = moe(h, 0, False)

    a = _norm_mod(h, norm1_g[1], mod3, lat_row(1), 0, 1)
    u = _matmul_glu(a.reshape(bsz * t, d), conv_w_pw1[0]).reshape(bsz, t, d)
    v = _dwconv(u, conv_w_dw[0], conv_b_dw[0])
    h = _conv_out(v, conv_ln_g[0], conv_ln_b[0], conv_w_pw2[0], h, mod3, lat_row(1), 2)
    return moe(h, 1, True)
```

```python
import functools

import jax
import jax.numpy as jnp
from jax import lax
from jax.experimental import pallas as pl
from jax.experimental.pallas import tpu as pltpu
from jax.experimental.pallas import tpu_sc as plsc

F32 = jnp.float32
BF16 = jnp.bfloat16
U32 = jnp.uint32
I32 = jnp.int32

EPS = 1e-6
LANES = 128
SUBLANES = 8
HEAD_DIM = 128
SCAN_CHUNK = 64
GRID_W = 64
CONV_WIDTH = 31
CONV_REACH = (16 - CONV_WIDTH // 2 + CONV_WIDTH - 1) // 8 * 8
MOE_GROUPS = 8
MOE_PER_GROUP = 8
MOE_EXPERTS = MOE_GROUPS * MOE_PER_GROUP
MOE_BLOCK = 256
SC_GATHER_WINDOW = 256
MOD_ROWS = 16
VMEM_LIMIT = 56 * 1024 * 1024
SAFE_CHUNK_LOG_DECAY = -80.0
HI_MASK = 0xFFFF0000


def _pick(n, pref):
    t = pref
    while n % t:
        t //= 2
    return t


def _params(sem):
    return pltpu.CompilerParams(dimension_semantics=sem, vmem_limit_bytes=VMEM_LIMIT)


def _silu(x):
    return x * jax.nn.sigmoid(x)


def _dot(a, b):
    return jnp.dot(a, b, preferred_element_type=F32)


def _dot_nt(a, b):
    return lax.dot_general(a, b, (((1,), (1,)), ((), ())), preferred_element_type=F32)


def _dot_tn(a, b):
    return lax.dot_general(a, b, (((0,), (0,)), ((), ())), preferred_element_type=F32)


def _pack_bf16_pairs(x):
    n = x.shape[1] // 2
    bits = lax.bitcast_convert_type(x.astype(BF16).astype(F32), U32)
    return (bits[:, :n] >> 16) | (bits[:, n:] & jnp.uint32(HI_MASK))


def _unpack_bf16_pairs(w):
    lo = lax.bitcast_convert_type(w << 16, F32)
    hi = lax.bitcast_convert_type(w & jnp.uint32(HI_MASK), F32)
    return lo, hi


def _store_token_groups(ref, w):
    m, n = w.shape[0], w.shape[1] // LANES
    for s in range(n):
        ref[:, s * SUBLANES:(s + 1) * SUBLANES, :] = (
            w[:, s * LANES:(s + 1) * LANES].reshape(m // SUBLANES, SUBLANES, LANES))


def _load_token_groups(ref, n):
    g = ref.shape[0]
    return jnp.concatenate(
        [ref[:, s * SUBLANES:(s + 1) * SUBLANES, :].reshape(g * SUBLANES, LANES) for s in range(n)],
        axis=1)


def _group_row(t, n):
    return (t // SUBLANES) * (SUBLANES * n) + t % SUBLANES


def _adaln_body(c_ref, w_ref, b_ref, o_ref):
    s = _silu(c_ref[...])
    o_ref[0] = _dot(s.astype(BF16), w_ref[0].astype(BF16)) + b_ref[0]


def _adaln(c_all, ada_w, ada_b):
    depth, d, n6 = ada_w.shape
    tn = _pick(n6, 1024)
    return pl.pallas_call(
        _adaln_body,
        grid=(depth, n6 // tn),
        in_specs=[
            pl.BlockSpec((MOD_ROWS, d), lambda l, j: (0, 0)),
            pl.BlockSpec((1, d, tn), lambda l, j: (l, 0, j)),
            pl.BlockSpec((1, 1, tn), lambda l, j: (l, 0, j)),
        ],
        out_specs=pl.BlockSpec((1, MOD_ROWS, tn), lambda l, j: (l, 0, j)),
        out_shape=jax.ShapeDtypeStruct((depth, MOD_ROWS, n6), F32),
        compiler_params=_params(("arbitrary", "arbitrary")),
        name="adaln",
    )(c_all, ada_w, ada_b.reshape(depth, 1, n6))


def _mod_spec(d, row_fn, sec):
    return pl.BlockSpec((1, 1, d), lambda b, i: (row_fn(b), 0, sec))


def _rms(x, g):
    return x * lax.rsqrt(jnp.mean(x * x, axis=-1, keepdims=True) + EPS) * g


def _norm_mod_body(x_ref, g_ref, sh_ref, sc_ref, o_ref):
    y = _rms(x_ref[0], g_ref[...])
    o_ref[0] = (y * (1.0 + sc_ref[0]) + sh_ref[0]).astype(o_ref.dtype)


def _norm_mod(x, g, mod3, row_fn, sec_shift, sec_scale):
    bsz, t, d = x.shape
    tm = _pick(t, 512)
    return pl.pallas_call(
        _norm_mod_body,
        grid=(bsz, t // tm),
        in_specs=[
            pl.BlockSpec((1, tm, d), lambda b, i: (b, i, 0)),
            pl.BlockSpec((1, d), lambda b, i: (0, 0)),
            _mod_spec(d, row_fn, sec_shift),
            _mod_spec(d, row_fn, sec_scale),
        ],
        out_specs=pl.BlockSpec((1, tm, d), lambda b, i: (b, i, 0)),
        out_shape=jax.ShapeDtypeStruct((bsz, t, d), BF16),
        compiler_params=_params(("arbitrary", "arbitrary")),
        name="norm_mod",
    )(x, g.reshape(1, d), mod3, mod3)


def _mm_body(a_ref, w_ref, o_ref, wb_ref):
    @pl.when(pl.program_id(1) == 0)
    def _():
        wb_ref[...] = w_ref[...].astype(BF16)

    o_ref[...] = _dot(a_ref[...], wb_ref[...]).astype(o_ref.dtype)


def _matmul(a, w):
    m, k = a.shape
    n = w.shape[1]
    tm, tn = _pick(m, 1024), _pick(n, 1024)
    return pl.pallas_call(
        _mm_body,
        grid=(n // tn, m // tm),
        in_specs=[
            pl.BlockSpec((tm, k), lambda j, i: (i, 0)),
            pl.BlockSpec((k, tn), lambda j, i: (0, j)),
        ],
        out_specs=pl.BlockSpec((tm, tn), lambda j, i: (i, j)),
        out_shape=jax.ShapeDtypeStruct((m, n), BF16),
        scratch_shapes=[pltpu.VMEM((k, tn), BF16)],
        compiler_params=_params(("arbitrary", "arbitrary")),
        name="matmul",
    )(a, w)


def _mm_glu_body(a_ref, wv_ref, wg_ref, o_ref, wvb_ref, wgb_ref):
    @pl.when(pl.program_id(1) == 0)
    def _():
        wvb_ref[...] = wv_ref[...].astype(BF16)
        wgb_ref[...] = wg_ref[...].astype(BF16)

    a = a_ref[...]
    val = _dot(a, wvb_ref[...])
    gate = _dot(a, wgb_ref[...])
    o_ref[...] = (val * jax.nn.sigmoid(gate)).astype(o_ref.dtype)


def _matmul_glu(a, w):
    m, k = a.shape
    n = w.shape[1] // 2
    tm, tn = _pick(m, 512), _pick(n, 512)
    nj = n // tn
    return pl.pallas_call(
        _mm_glu_body,
        grid=(nj, m // tm),
        in_specs=[
            pl.BlockSpec((tm, k), lambda j, i: (i, 0)),
            pl.BlockSpec((k, tn), lambda j, i: (0, j)),
            pl.BlockSpec((k, tn), lambda j, i: (0, j + nj)),
        ],
        out_specs=pl.BlockSpec((tm, tn), lambda j, i: (i, j)),
        out_shape=jax.ShapeDtypeStruct((m, n), BF16),
        scratch_shapes=[pltpu.VMEM((k, tn), BF16), pltpu.VMEM((k, tn), BF16)],
        compiler_params=_params(("arbitrary", "arbitrary")),
        name="matmul_glu",
    )(a, w, w)


def _scan_body(qc_ref, vc_ref, zc_ref, ql_ref, vl_ref, zl_ref, lb_ref, o_ref,
               st_ref, flag_ref, b_sc, q_sc, k_sc, v_sc, *, nc, n_sub, heads):
    chunk = SCAN_CHUNK
    direction = pl.program_id(1)
    step = pl.program_id(2)
    fwd = direction == 0

    @pl.when(step == 0)
    def _():
        st_ref[0] = jnp.zeros(st_ref.shape[1:], F32)

    @pl.when(step > 0)
    def _():
        st_ref[0] = st_ref[n_sub]

    lb = lb_ref[0]
    row = lax.broadcasted_iota(I32, (chunk, chunk), 0)
    col = lax.broadcasted_iota(I32, (chunk, chunk), 1)
    sign = 1 - 2 * direction
    tri = (row - col) * sign >= 0
    tri_b = jnp.where(tri, 1.0, 0.0).astype(BF16)
    t_idx = lax.broadcasted_iota(I32, (chunk, 1), 0)

    def chunk_rows(ci):
        cidx = jnp.where(fwd, ci, n_sub - 1 - ci)
        return pl.ds(pl.multiple_of(cidx * chunk, chunk), chunk)

    def decay(z_ref, rows, sl):
        z = z_ref[0, rows, sl].astype(F32)
        f = lb[:, sl] + (1.0 - lb[:, sl]) * jax.nn.sigmoid(z)
        lf = jnp.log(f)
        hi = lf.astype(BF16)
        lo = (lf - hi.astype(F32)).astype(BF16)
        b = _dot(tri_b, hi) + _dot(tri_b, lo)
        tot = jnp.where(fwd, b[chunk - 1:chunk, :], b[0:1, :])
        return b, tot, 1.0 - f

    def fast_chunk(q_ref, v_ref, z_ref, ci, emit):
        rows = chunk_rows(ci)
        b, tot, kk = decay(z_ref, rows, slice(None))
        ke = kk * jnp.exp(tot - b)
        ke_b = ke.astype(BF16)
        etot = jnp.exp(tot)
        v = v_ref[0, rows, :]
        head_sl = [slice(h * HEAD_DIM, (h + 1) * HEAD_DIM) for h in range(heads)]
        st = [st_ref[ci, h] for h in range(heads)]
        kv = [_dot_tn(v[:, sl], ke_b[:, sl]) for sl in head_sl]
        if emit:
            qe = _silu(q_ref[0, rows, :].astype(F32)) * jnp.exp(b)
            qe_b = qe.astype(BF16)
            g = jnp.exp(-0.5 * tot)
            qh = (qe * g).astype(BF16)
            kh = (ke * g).astype(BF16)
            flag_ref[ci] = (jnp.min(tot) < SAFE_CHUNK_LOG_DECAY).astype(I32)
            sc = [_dot_nt(qh[:, sl], kh[:, sl]) for sl in head_sl]
            o_inter = [_dot_nt(qe_b[:, sl], st[h].astype(BF16)) for h, sl in enumerate(head_sl)]
            p = [jnp.where(tri, s, 0.0).astype(BF16) for s in sc]
            o_intra = [_dot(p[h], v[:, sl]) for h, sl in enumerate(head_sl)]
            for h, sl in enumerate(head_sl):
                o_ref[0, 0, rows, sl] = (o_inter[h] + o_intra[h]).astype(o_ref.dtype)
        for h, sl in enumerate(head_sl):
            st_ref[ci + 1, h] = st[h] * etot[:, sl] + kv[h]

    def exact_chunk(q_ref, v_ref, z_ref, ci):
        rows = chunk_rows(ci)
        b, _, kk = decay(z_ref, rows, slice(None))
        b_sc[...] = b
        q_sc[...] = _silu(q_ref[0, rows, :].astype(F32))
        k_sc[...] = kk
        v_sc[...] = v_ref[0, rows, :].astype(F32)

        def head_loop(h, c2):
            lanes = pl.ds(pl.multiple_of(h * HEAD_DIM, HEAD_DIM), HEAD_DIM)
            b_h = b_sc[:, lanes]
            q_h = q_sc[:, lanes]
            k_h = k_sc[:, lanes]
            v_h = v_sc[:, lanes]
            o_inter = _dot_nt((q_h * jnp.exp(b_h)).astype(BF16), st_ref[ci, h].astype(BF16))

            def key_loop(s, acc):
                pick = t_idx == s
                b_s = jnp.sum(jnp.where(pick, b_h, 0.0), axis=0, keepdims=True)
                k_s = jnp.sum(jnp.where(pick, k_h, 0.0), axis=0, keepdims=True)
                v_s = jnp.sum(jnp.where(pick, v_h, 0.0), axis=0, keepdims=True)
                valid = (t_idx - s) * sign >= 0
                dec = jnp.where(valid, jnp.exp(jnp.where(valid, b_h - b_s, 0.0)), 0.0)
                w = jnp.sum(q_h * dec * k_s, axis=1, keepdims=True)
                return acc + w * v_s

            o_intra = lax.fori_loop(0, chunk, key_loop, jnp.zeros((chunk, HEAD_DIM), F32))
            o_ref[0, 0, rows, lanes] = (o_inter + o_intra).astype(o_ref.dtype)
            return c2

        lax.fori_loop(0, heads, head_loop, 0)

    @pl.when(step < nc)
    def _():
        for ci in range(n_sub):
            fast_chunk(qc_ref, vc_ref, zc_ref, ci, False)

    @pl.when(step >= nc)
    def _():
        for ci in range(n_sub):
            fast_chunk(ql_ref, vl_ref, zl_ref, ci, True)

        def redo(ci, c):
            @pl.when(flag_ref[ci] == 1)
            def _():
                exact_chunk(ql_ref, vl_ref, zl_ref, ci)
            return c

        lax.fori_loop(0, n_sub, redo, 0)


def _scan(p_ctx, p_lat, lb):
    bsz, t_lat, d5 = p_lat.shape
    t_ctx = p_ctx.shape[1]
    d = d5 // 5
    heads = d // HEAD_DIM
    tb = _pick(t_ctx, 256)
    assert t_lat % tb == 0 and tb % SCAN_CHUNK == 0
    nc, nl = t_ctx // tb, t_lat // tb
    n_sub = tb // SCAN_CHUNK

    def ctx_blk(dr, s):
        sc = jnp.minimum(s, nc - 1)
        return jnp.where(dr == 0, sc, nc - 1 - sc)

    def lat_blk(dr, s):
        sl = jnp.maximum(s - nc, 0)
        return jnp.where(dr == 0, sl, nl - 1 - sl)

    def cspec(sec_fn):
        return pl.BlockSpec((1, tb, d), lambda b, dr, s: (b, ctx_blk(dr, s), sec_fn(dr)))

    def lspec(sec_fn):
        return pl.BlockSpec((1, tb, d), lambda b, dr, s: (b, lat_blk(dr, s), sec_fn(dr)))

    q_sec = lambda dr: 0
    v_sec = lambda dr: 1
    z_sec = lambda dr: 2 + dr
    return pl.pallas_call(
        functools.partial(_scan_body, nc=nc, n_sub=n_sub, heads=heads),
        grid=(bsz, 2, nc + nl),
        in_specs=[cspec(q_sec), cspec(v_sec), cspec(z_sec),
                  lspec(q_sec), lspec(v_sec), lspec(z_sec),
                  pl.BlockSpec((1, 1, d), lambda b, dr, s: (dr, 0, 0))],
        out_specs=pl.BlockSpec((1, 1, tb, d), lambda b, dr, s: (dr, b, lat_blk(dr, s), 0)),
        out_shape=jax.ShapeDtypeStruct((2, bsz, t_lat, d), BF16),
        scratch_shapes=[pltpu.VMEM((n_sub + 1, heads, HEAD_DIM, HEAD_DIM), F32),
                        pltpu.SMEM((n_sub,), I32),
                        pltpu.VMEM((SCAN_CHUNK, d), F32),
                        pltpu.VMEM((SCAN_CHUNK, d), F32),
                        pltpu.VMEM((SCAN_CHUNK, d), F32),
                        pltpu.VMEM((SCAN_CHUNK, d), F32)],
        compiler_params=_params(("arbitrary", "arbitrary", "arbitrary")),
        name="hgrn_scan",
    )(p_ctx, p_ctx, p_ctx, p_lat, p_lat, p_lat, lb.reshape(2, 1, d))


def _hgrn_readout_body(of_ref, ob_ref, g_ref, on_ref, w_ref, h_ref, gt_ref, o_ref,
                       wb_ref, a_ref, *, heads):
    @pl.when((pl.program_id(0) == 0) & (pl.program_id(1) == 0))
    def _():
        wb_ref[...] = w_ref[...].astype(BF16)

    on = on_ref[...]
    for h in range(heads):
        sl = slice(h * HEAD_DIM, (h + 1) * HEAD_DIM)
        o = of_ref[0, 0, :, sl].astype(F32) + ob_ref[0, 0, :, sl].astype(F32)
        o = _rms(o, on)
        a_ref[:, sl] = (o * _silu(g_ref[0, :, sl].astype(F32))).astype(BF16)
    y = _dot(a_ref[...], wb_ref[...])
    o_ref[0] = h_ref[0] + gt_ref[0] * y


def _hgrn_readout(o2, p_lat, onorm_g, w_out, h, mod3, row_fn, sec_gate):
    bsz, t, d = h.shape
    heads = d // HEAD_DIM
    tm = _pick(t, 256)
    return pl.pallas_call(
        functools.partial(_hgrn_readout_body, heads=heads),
        grid=(bsz, t // tm),
        in_specs=[
            pl.BlockSpec((1, 1, tm, d), lambda b, i: (0, b, i, 0)),
            pl.BlockSpec((1, 1, tm, d), lambda b, i: (1, b, i, 0)),
            pl.BlockSpec((1, tm, d), lambda b, i: (b, i, 4)),
            pl.BlockSpec((1, HEAD_DIM), lambda b, i: (0, 0)),
            pl.BlockSpec((d, d), lambda b, i: (0, 0), pipeline_mode=pl.Buffered(1)),
            pl.BlockSpec((1, tm, d), lambda b, i: (b, i, 0)),
            _mod_spec(d, row_fn, sec_gate),
        ],
        out_specs=pl.BlockSpec((1, tm, d), lambda b, i: (b, i, 0)),
        out_shape=jax.ShapeDtypeStruct((bsz, t, d), F32),
        scratch_shapes=[pltpu.VMEM((d, d), BF16), pltpu.VMEM((tm, d), BF16)],
        compiler_params=_params(("arbitrary", "arbitrary")),
        name="hgrn_readout",
    )(o2, o2, p_lat, onorm_g.reshape(1, HEAD_DIM), w_out, h, mod3)


def _conv_out_body(v_ref, lg_ref, lbias_ref, w_ref, h_ref, gt_ref, o_ref, wb_ref):
    @pl.when((pl.program_id(0) == 0) & (pl.program_id(1) == 0))
    def _():
        wb_ref[...] = w_ref[...].astype(BF16)

    x = v_ref[0].astype(F32)
    mu = jnp.mean(x, axis=-1, keepdims=True)
    xc = x - mu
    y = xc * lax.rsqrt(jnp.mean(xc * xc, axis=-1, keepdims=True) + EPS)
    y = _silu(y * lg_ref[...] + lbias_ref[...])
    o_ref[0] = h_ref[0] + gt_ref[0] * _dot(y.astype(BF16), wb_ref[...])


def _conv_out(v, ln_g, ln_b, w, h, mod3, row_fn, sec_gate):
    bsz, t, d = h.shape
    tm = _pick(t, 256)
    return pl.pallas_call(
        _conv_out_body,
        grid=(bsz, t // tm),
        in_specs=[
            pl.BlockSpec((1, tm, d), lambda b, i: (b, i, 0)),
            pl.BlockSpec((1, d), lambda b, i: (0, 0)),
            pl.BlockSpec((1, d), lambda b, i: (0, 0)),
            pl.BlockSpec((d, d), lambda b, i: (0, 0), pipeline_mode=pl.Buffered(1)),
            pl.BlockSpec((1, tm, d), lambda b, i: (b, i, 0)),
            _mod_spec(d, row_fn, sec_gate),
        ],
        out_specs=pl.BlockSpec((1, tm, d), lambda b, i: (b, i, 0)),
        out_shape=jax.ShapeDtypeStruct((bsz, t, d), F32),
        scratch_shapes=[pltpu.VMEM((d, d), BF16)],
        compiler_params=_params(("arbitrary", "arbitrary")),
        name="conv_out",
    )(v, ln_g.reshape(1, d), ln_b.reshape(1, d), w, h, mod3)


def _dwconv_body(u_ref, w_ref, b_ref, o_ref, pad_ref, shift_ref, *, rows, cb, half_blocks):
    half = CONV_WIDTH // 2
    j = pl.program_id(1)
    n_lane = cb // LANES
    bias = b_ref[...]

    def taps(read, r):
        for lb in range(n_lane):
            ln = slice(lb * LANES, (lb + 1) * LANES)
            acc = jnp.zeros((GRID_W, LANES), F32)
            for k in range(CONV_WIDTH):
                acc = acc + w_ref[k:k + 1, ln] * read(r, k, ln)
            o_ref[0, pl.ds(pl.multiple_of(r * GRID_W, GRID_W), GRID_W), ln] = (
                acc + bias[:, ln]).astype(o_ref.dtype)

    @pl.when(j < half_blocks)
    def _():
        zeros = jnp.zeros((rows, 16, cb), F32)
        pad_ref[0:rows, 0:16, :] = zeros
        pad_ref[0:rows, 16 + GRID_W:32 + GRID_W, :] = zeros
        pad_ref[0:rows, 16:16 + GRID_W, :] = u_ref[0].astype(F32).reshape(rows, GRID_W, cb)

        span = GRID_W + CONV_REACH

        def row_loop(r, c):
            for s in range(1, SUBLANES):
                shift_ref[s - 1] = pad_ref[r, s:s + span, :]

            def read(r_, k, ln):
                a, s = divmod(16 - half + k, SUBLANES)
                if s == 0:
                    return pad_ref[r_, a * SUBLANES:a * SUBLANES + GRID_W, ln]
                return shift_ref[s - 1, a * SUBLANES:a * SUBLANES + GRID_W, ln]

            taps(read, r)
            return c

        lax.fori_loop(0, rows, row_loop, 0)

    @pl.when(j >= half_blocks)
    def _():
        zeros = jnp.zeros((half, GRID_W, cb), F32)
        pad_ref[0:half, 0:GRID_W, :] = zeros
        pad_ref[half + rows:2 * half + rows, 0:GRID_W, :] = zeros
        pad_ref[half:half + rows, 0:GRID_W, :] = u_ref[0].astype(F32).reshape(rows, GRID_W, cb)

        def row_loop(r, c):
            taps(lambda r_, k, ln: pad_ref[r_ + k, 0:GRID_W, ln], r)
            return c

        lax.fori_loop(0, rows, row_loop, 0)


def _dwconv(u, w_dw, b_dw):
    bsz, t, d = u.shape
    rows = t // GRID_W
    cb = _pick(d // 2, 256)
    half_blocks = (d // 2) // cb
    return pl.pallas_call(
        functools.partial(_dwconv_body, rows=rows, cb=cb, half_blocks=half_blocks),
        grid=(bsz, d // cb),
        in_specs=[
            pl.BlockSpec((1, t, cb), lambda b, j: (b, 0, j)),
            pl.BlockSpec((CONV_WIDTH, cb), lambda b, j: (0, j)),
            pl.BlockSpec((1, cb), lambda b, j: (0, j)),
        ],
        out_specs=pl.BlockSpec((1, t, cb), lambda b, j: (b, 0, j)),
        out_shape=jax.ShapeDtypeStruct((bsz, t, d), BF16),
        scratch_shapes=[pltpu.VMEM((rows + 2 * (CONV_WIDTH // 2) + 2, GRID_W + 32, cb), F32),
                        pltpu.VMEM((SUBLANES - 1, GRID_W + CONV_REACH, cb), F32)],
        compiler_params=_params(("arbitrary", "arbitrary")),
        name="dwconv",
    )(u, w_dw, b_dw.reshape(1, d))


def _router_body(h_ref, g_ref, sh_ref, sc_ref, wr_ref, br_ref,
                 bp_ref, ri_ref, rw_ref, cnt_ref, carry_ref):
    tm = h_ref.shape[1]

    @pl.when((pl.program_id(0) == 0) & (pl.program_id(1) == 0))
    def _():
        carry_ref[...] = jnp.zeros_like(carry_ref)

    bmod = _rms(h_ref[0], g_ref[...]) * (1.0 + sc_ref[0]) + sh_ref[0]
    _store_token_groups(bp_ref, _pack_bf16_pairs(bmod))

    b_hi = bmod.astype(BF16)
    b_lo = (bmod - b_hi.astype(F32)).astype(BF16)
    w = wr_ref[...]
    w_hi = w.astype(BF16)
    w_lo = (w - w_hi.astype(F32)).astype(BF16)
    logits = _dot(b_hi, w_hi) + _dot(b_hi, w_lo) + _dot(b_lo, w_hi) + br_ref[...]

    lane_i = lax.broadcasted_iota(I32, (tm, LANES), 1)
    lane = lane_i.astype(F32)
    neg = jnp.float32(-jnp.inf)
    big = jnp.float32(1 << 20)
    is_grp = (lane_i >= MOE_EXPERTS) & (lane_i < MOE_EXPERTS + MOE_GROUPS)
    m1 = jnp.max(jnp.where(is_grp, logits, neg), axis=1, keepdims=True)
    grp = jnp.min(jnp.where(is_grp & (logits == m1), lane - MOE_EXPERTS, big), axis=1, keepdims=True)
    pg = 1.0 / jnp.sum(jnp.where(is_grp, jnp.exp(logits - m1), 0.0), axis=1, keepdims=True)

    sel = (lane >= grp * MOE_PER_GROUP) & (lane < (grp + 1) * MOE_PER_GROUP)
    m2 = jnp.max(jnp.where(sel, logits, neg), axis=1, keepdims=True)
    ea = jnp.min(jnp.where(sel & (logits == m2), lane, big), axis=1, keepdims=True)
    sel2 = sel & (lane != ea)
    m3 = jnp.max(jnp.where(sel2, logits, neg), axis=1, keepdims=True)
    eb = jnp.min(jnp.where(sel2 & (logits == m3), lane, big), axis=1, keepdims=True)
    tr = jnp.exp(m3 - m2)
    wa = pg / (1.0 + tr)
    wb = wa * tr

    hit_a = lane == ea
    hit_b = lane == eb
    onehot = jnp.where(hit_a | hit_b, 1.0, 0.0)
    r_i = lax.broadcasted_iota(I32, (tm, tm), 0)
    c_i = lax.broadcasted_iota(I32, (tm, tm), 1)
    lower = jnp.where(r_i > c_i, 1.0, 0.0).astype(BF16)
    before = _dot(lower, onehot.astype(BF16)) + carry_ref[...]
    rank_a = jnp.sum(jnp.where(hit_a, before, 0.0), axis=1, keepdims=True).astype(I32)
    rank_b = jnp.sum(jnp.where(hit_b, before, 0.0), axis=1, keepdims=True).astype(I32)
    carry = carry_ref[...] + jnp.sum(onehot, axis=0, keepdims=True)
    carry_ref[...] = carry
    cnt_ref[...] = carry

    ri_ref[...] = jnp.where(lane_i == 0, ea.astype(I32), jnp.where(lane_i == 1, eb.astype(I32),
                            jnp.where(lane_i == 2, rank_a, jnp.where(lane_i == 3, rank_b, 0))))
    rw_ref[...] = jnp.where(lane_i == 0, wa, jnp.where(lane_i == 1, wb, 0.0))


def _router(h, g, mod3, row_fn, sec_shift, sec_scale, w_r, b_r):
    bsz, t, d = h.shape
    tm = _pick(t, 256)
    nt = t // tm
    n = bsz * t
    rpt = d // 2 // LANES
    return pl.pallas_call(
        _router_body,
        grid=(bsz, nt),
        in_specs=[
            pl.BlockSpec((1, tm, d), lambda b, i: (b, i, 0)),
            pl.BlockSpec((1, d), lambda b, i: (0, 0)),
            _mod_spec(d, row_fn, sec_shift),
            _mod_spec(d, row_fn, sec_scale),
            pl.BlockSpec((d, LANES), lambda b, i: (0, 0)),
            pl.BlockSpec((1, LANES), lambda b, i: (0, 0)),
        ],
        out_specs=[
            pl.BlockSpec((tm // SUBLANES, SUBLANES * rpt, LANES), lambda b, i: (b * nt + i, 0, 0)),
            pl.BlockSpec((tm, LANES), lambda b, i: (b * nt + i, 0)),
            pl.BlockSpec((tm, LANES), lambda b, i: (b * nt + i, 0)),
            pl.BlockSpec((1, LANES), lambda b, i: (0, 0)),
        ],
        out_shape=[
            jax.ShapeDtypeStruct((n // SUBLANES, SUBLANES * rpt, LANES), U32),
            jax.ShapeDtypeStruct((n, LANES), I32),
            jax.ShapeDtypeStruct((n, LANES), F32),
            jax.ShapeDtypeStruct((1, LANES), F32),
        ],
        scratch_shapes=[pltpu.VMEM((1, LANES), F32)],
        compiler_params=_params(("arbitrary", "arbitrary")),
        name="router",
    )(h, g.reshape(1, d), mod3, mod3, w_r, b_r)


def _expert_body(meta_ref, xs_ref, wg_ref, wu_ref, wd_ref, y_ref,
                 wgb_ref, wub_ref, wdb_ref, xbuf, ybuf, xsem, ysem, *, rpt):
    e = pl.program_id(0)
    blk0 = meta_ref[0, e]
    n_blk = meta_ref[0, e + 1] - blk0
    n_used = meta_ref[2, 0]
    dh = rpt * LANES
    groups = MOE_BLOCK // SUBLANES

    def block_groups(ref, g):
        return ref.at[pl.ds(pl.multiple_of(g * groups, groups), groups)]

    def read_in(g, slot):
        return pltpu.make_async_copy(block_groups(xs_ref, g), xbuf.at[slot], xsem.at[slot])

    def write_out(g):
        return pltpu.make_async_copy(ybuf, block_groups(y_ref, g), ysem)

    @pl.when(e == 0)
    def _():
        read_in(0, 0).start()

    @pl.when(n_blk > 0)
    def _():
        wgb_ref[...] = wg_ref[0, 0].astype(BF16)
        wub_ref[...] = wu_ref[0, 0].astype(BF16)
        wdb_ref[...] = wd_ref[0, 0].astype(BF16)

        def block(j, c):
            g = blk0 + j
            slot = g % 2

            @pl.when(g + 1 < n_used)
            def _():
                read_in(g + 1, 1 - slot).start()

            read_in(g, slot).wait()
            x_lo, x_hi = _unpack_bf16_pairs(_load_token_groups(xbuf.at[slot], rpt))
            x_lo = x_lo.astype(BF16)
            x_hi = x_hi.astype(BF16)
            gate = _dot(x_lo, wgb_ref[0:dh, :]) + _dot(x_hi, wgb_ref[dh:2 * dh, :])
            up = _dot(x_lo, wub_ref[0:dh, :]) + _dot(x_hi, wub_ref[dh:2 * dh, :])
            mid = (_silu(gate) * up).astype(BF16)
            out = _pack_bf16_pairs(_dot(mid, wdb_ref[...]))

            @pl.when(g > 0)
            def _():
                write_out(g - 1).wait()

            _store_token_groups(ybuf, out)
            write_out(g).start()
            return c

        lax.fori_loop(0, n_blk, block, 0)

    @pl.when(e == pl.num_programs(0) - 1)
    def _():
        write_out(n_used - 1).wait()


def _experts(meta, xs, w_gate, w_up, w_down, layer):
    _, n_exp, d, hid = w_gate.shape
    rpt = d // 2 // LANES
    blk_shape = (MOE_BLOCK // SUBLANES, SUBLANES * rpt, LANES)

    def wspec(shape):
        return pl.BlockSpec((1, 1) + shape, lambda e, m: (layer, e, 0, 0))

    return pl.pallas_call(
        functools.partial(_expert_body, rpt=rpt),
        grid_spec=pltpu.PrefetchScalarGridSpec(
            num_scalar_prefetch=1,
            grid=(n_exp,),
            in_specs=[pl.BlockSpec(memory_space=pl.ANY),
                      wspec((d, hid)), wspec((d, hid)), wspec((hid, d))],
            out_specs=pl.BlockSpec(memory_space=pl.ANY),
            scratch_shapes=[pltpu.VMEM((d, hid), BF16), pltpu.VMEM((d, hid), BF16),
                            pltpu.VMEM((hid, d), BF16),
                            pltpu.VMEM((2,) + blk_shape, U32), pltpu.VMEM(blk_shape, U32),
                            pltpu.SemaphoreType.DMA((2,)), pltpu.SemaphoreType.DMA],
        ),
        out_shape=jax.ShapeDtypeStruct(xs.shape, U32),
        compiler_params=_params(("arbitrary",)),
        name="moe_experts",
    )(meta, xs, w_gate, w_up, w_down)


def _sc_mesh():
    return plsc.VectorSubcoreMesh(core_axis_name="core", subcore_axis_name="subcore")


def _sc_scatter_rows_twice(src, idx_a, idx_b, n_out):
    n = src.shape[0]
    win = _pick(n, SC_GATHER_WINDOW)

    @pl.kernel(out_type=jax.ShapeDtypeStruct((n_out, LANES), src.dtype), mesh=_sc_mesh(),
               scratch_types=[])
    def scatter_kernel(s_hbm, ia_hbm, ib_hbm, o_hbm):
        def window(s_vmem, ia_vmem, ib_vmem):
            pltpu.sync_copy(s_vmem, o_hbm.at[ia_vmem.at[0]])
            pltpu.sync_copy(s_vmem, o_hbm.at[ib_vmem.at[0]])

        pltpu.emit_pipeline(
            window,
            grid=(n // win,),
            in_specs=[pl.BlockSpec((win, LANES), index_map=lambda i: (i, 0)),
                      pl.BlockSpec((1, win), index_map=lambda i: (0, i)),
                      pl.BlockSpec((1, win), index_map=lambda i: (0, i))],
            out_specs=[],
            core_axis_name=("core", "subcore"),
            dimension_semantics=(pltpu.PARALLEL,),
        )(s_hbm, ia_hbm, ib_hbm)

    return scatter_kernel(src, idx_a.reshape(1, n), idx_b.reshape(1, n))


def _sc_gather_rows_twice(table, idx_a, idx_b):
    n_idx = idx_a.shape[0]
    win = _pick(n_idx, SC_GATHER_WINDOW // 2)
    out = jax.ShapeDtypeStruct((n_idx, LANES), table.dtype)

    @pl.kernel(out_type=(out, out), mesh=_sc_mesh())
    def gather_kernel(t_hbm, ia_hbm, ib_hbm, oa_hbm, ob_hbm):
        def window(ia_vmem, ib_vmem, oa_vmem, ob_vmem):
            pltpu.sync_copy(t_hbm.at[ia_vmem.at[0]], oa_vmem)
            pltpu.sync_copy(t_hbm.at[ib_vmem.at[0]], ob_vmem)

        idx_spec = pl.BlockSpec((1, win), index_map=lambda i: (0, i))
        out_spec = pl.BlockSpec((win, LANES), index_map=lambda i: (i, 0))
        pltpu.emit_pipeline(
            window,
            grid=(n_idx // win,),
            in_specs=[idx_spec, idx_spec],
            out_specs=[out_spec, out_spec],
            core_axis_name=("core", "subcore"),
            dimension_semantics=(pltpu.PARALLEL,),
        )(ia_hbm, ib_hbm, oa_hbm, ob_hbm)

    return gather_kernel(table, idx_a.reshape(1, n_idx), idx_b.reshape(1, n_idx))


def _combine_body(h_ref, ya_ref, yb_ref, rw_ref, gt_ref, fg_ref, o_ref, *, rpt, final):
    tm = h_ref.shape[1]
    dh = rpt * LANES
    wa = rw_ref[:, 0:1]
    wb = rw_ref[:, 1:2]
    a_lo, a_hi = _unpack_bf16_pairs(_load_token_groups(ya_ref, rpt))
    b_lo, b_hi = _unpack_bf16_pairs(_load_token_groups(yb_ref, rpt))
    gt = gt_ref[0]
    lo = h_ref[0, :, 0:dh] + gt[:, 0:dh] * (wa * a_lo + wb * b_lo)
    hi = h_ref[0, :, dh:2 * dh] + gt[:, dh:2 * dh] * (wa * a_hi + wb * b_hi)
    if final:
        ms = (jnp.sum(lo * lo, axis=-1, keepdims=True)
              + jnp.sum(hi * hi, axis=-1, keepdims=True)) / (2 * dh)
        inv = lax.rsqrt(ms + EPS)
        lo = lo * inv * fg_ref[:, 0:dh]
        hi = hi * inv * fg_ref[:, dh:2 * dh]
    o_ref[0, :, 0:dh] = lo
    o_ref[0, :, dh:2 * dh] = hi


def _combine(h, ya, yb, rw, mod3, row_fn, sec_gate, final_g, final):
    bsz, t, d = h.shape
    tm = _pick(t, 256)
    nt = t // tm
    rpt = d // 2 // LANES
    return pl.pallas_call(
        functools.partial(_combine_body, rpt=rpt, final=final),
        grid=(bsz, nt),
        in_specs=[
            pl.BlockSpec((1, tm, d), lambda b, i: (b, i, 0)),
            pl.BlockSpec((tm // SUBLANES, SUBLANES * rpt, LANES), lambda b, i: (b * nt + i, 0, 0)),
            pl.BlockSpec((tm // SUBLANES, SUBLANES * rpt, LANES), lambda b, i: (b * nt + i, 0, 0)),
            pl.BlockSpec((tm, LANES), lambda b, i: (b * nt + i, 0)),
            _mod_spec(d, row_fn, sec_gate),
            pl.BlockSpec((1, d), lambda b, i: (0, 0)),
        ],
        out_specs=pl.BlockSpec((1, tm, d), lambda b, i: (b, i, 0)),
        out_shape=jax.ShapeDtypeStruct((bsz, t, d), F32),
        compiler_params=_params(("arbitrary", "arbitrary")),
        name="moe_combine",
    )(h, ya, yb, rw, mod3, final_g.reshape(1, d))


def _slot_rows_body(ri_ref, start_ref, o_ref, *, rpt):
    tm = ri_ref.shape[0]
    ri = ri_ref[...].astype(F32)
    start = start_ref[...].astype(F32)
    lane_i = lax.broadcasted_iota(I32, (tm, LANES), 1)
    lane = lane_i.astype(F32)

    def first_row(e, rank):
        s = jnp.sum(jnp.where(lane == e, start, 0.0), axis=1, keepdims=True)
        return _group_row((s + rank).astype(I32), rpt)

    row_a = first_row(ri[:, 0:1], ri[:, 2:3])
    row_b = first_row(ri[:, 1:2], ri[:, 3:4])
    o_ref[...] = jnp.where(lane_i == 0, row_a, jnp.where(lane_i == 1, row_b, 0))


def _slot_rows(ri, start, rpt):
    n = ri.shape[0]
    tm = _pick(n, 512)
    return pl.pallas_call(
        functools.partial(_slot_rows_body, rpt=rpt),
        grid=(n // tm,),
        in_specs=[pl.BlockSpec((tm, LANES), lambda i: (i, 0)),
                  pl.BlockSpec((1, LANES), lambda i: (0, 0))],
        out_specs=pl.BlockSpec((tm, LANES), lambda i: (i, 0)),
        out_shape=jax.ShapeDtypeStruct((n, LANES), I32),
        compiler_params=_params(("arbitrary",)),
        name="slot_rows",
    )(ri, start)


def _hier_moe(h, g, mod3, row_fn, w_r1, b_r1, w_r2, b_r2, w_gate, w_up, w_down, layer, final_g, final):
    bsz, t, d = h.shape
    n = bsz * t
    w_r = jnp.concatenate([jnp.transpose(w_r2, (1, 0, 2)).reshape(d, MOE_EXPERTS), w_r1], axis=1)
    w_r = jnp.pad(w_r, ((0, 0), (0, LANES - w_r.shape[1])))
    b_r = jnp.pad(jnp.concatenate([b_r2.reshape(-1), b_r1]), (0, LANES - MOE_EXPERTS - MOE_GROUPS))
    bp, ri, rw, cnt = _router(h, g, mod3, row_fn, 3, 4, w_r, b_r.reshape(1, LANES))

    counts = cnt[0, :MOE_EXPERTS].astype(I32)
    padded = (counts + MOE_BLOCK - 1) // MOE_BLOCK * MOE_BLOCK
    pad_end = jnp.cumsum(padded)
    pad_start = pad_end - padded
    n_slots = (-(-(n * 2) // MOE_BLOCK) + MOE_EXPERTS) * MOE_BLOCK
    blk_off = jnp.concatenate([pad_start, pad_end[-1:]]) // MOE_BLOCK
    row = lambda v: jnp.pad(v.astype(I32), (0, LANES - v.shape[0]))
    meta = jnp.stack([row(blk_off), row(counts), row(blk_off[-1:])])
    rpt = d // 2 // LANES
    first = _slot_rows(ri, row(pad_start).reshape(1, LANES), rpt)
    piece = (SUBLANES * jnp.arange(rpt, dtype=I32)).reshape(1, rpt, 1)
    rows_a = (first[:, 0].reshape(n // SUBLANES, 1, SUBLANES) + piece).reshape(-1)
    rows_b = (first[:, 1].reshape(n // SUBLANES, 1, SUBLANES) + piece).reshape(-1)

    grouped = lambda a: a.reshape(-1, SUBLANES * rpt, LANES)
    xs = _sc_scatter_rows_twice(bp.reshape(-1, LANES), rows_a, rows_b, n_slots * rpt)
    ys = _experts(meta, grouped(xs), w_gate, w_up, w_down, layer)
    ya, yb = _sc_gather_rows_twice(ys.reshape(-1, LANES), rows_a, rows_b)
    return _combine(h, grouped(ya), grouped(yb), rw, mod3, row_fn, 5, final_g, final)


def kernel(x, c, ctx, c_ctx, ada_w, ada_b, norm1_g, norm2_g, hgrn_w_in, hgrn_lb, hgrn_onorm_g,
           hgrn_w_out, conv_w_pw1, conv_w_dw, conv_b_dw, conv_ln_g, conv_ln_b, conv_w_pw2,
           moe_w_r1, moe_b_r1, moe_w_r2, moe_b_r2, moe_w_gate, moe_w_up, moe_w_down, final_g):
    bsz, t, d = x.shape
    t_ctx = ctx.shape[1]
    depth = ada_w.shape[0]
    assert depth == 2 and bsz < MOD_ROWS and d % (2 * LANES) == 0 and t % GRID_W == 0

    c_all = jnp.concatenate([c, c_ctx[None, :], jnp.zeros((MOD_ROWS - bsz - 1, d), F32)], axis=0)
    mod = _adaln(c_all, ada_w, ada_b)
    mod3 = mod.reshape(depth * MOD_ROWS, 1, 6 * d)
    lat_row = lambda l: (lambda b: l * MOD_ROWS + b)
    ctx_row = lambda l: (lambda b: l * MOD_ROWS + bsz)
    lb_all = jnp.cumsum(jax.nn.softmax(hgrn_lb.astype(F32), axis=1), axis=1)

    def moe(h, l, final):
        return _hier_moe(h, norm2_g[l], mod3, lat_row(l), moe_w_r1[l], moe_b_r1[l], moe_w_r2[l],
                         moe_b_r2[l], moe_w_gate, moe_w_up, moe_w_down, l, final_g, final)

    a_lat = _norm_mod(x, norm1_g[0], mod3, lat_row(0), 0, 1)
    a_ctx = _norm_mod(ctx, norm1_g[0], mod3, ctx_row(0), 0, 1)
    p_lat = _matmul(a_lat.reshape(bsz * t, d), hgrn_w_in[0]).reshape(bsz, t, 5 * d)
    p_ctx = _matmul(a_ctx.reshape(bsz * t_ctx, d), hgrn_w_in[0]).reshape(bsz, t_ctx, 5 * d)
    o2 = _scan(p_ctx, p_lat, lb_all[:, 0])
    h = _hgrn_readout(o2, p_lat, hgrn_onorm_g[0], hgrn_w_out[0], x, mod3, lat_row(0), 2)
    h = moe(h, 0, False)

    a = _norm_mod(h, norm1_g[1], mod3, lat_row(1), 0, 1)
    u = _matmul_glu(a.reshape(bsz * t, d), conv_w_pw1[0]).reshape(bsz, t, d)
    v = _dwconv(u, conv_w_dw[0], conv_b_dw[0])
    h = _conv_out(v, conv_ln_g[0], conv_ln_b[0], conv_w_pw2[0], h, mod3, lat_row(1), 2)
    return moe(h, 1, True)
```
